```python
import jax, jax.numpy as jnp
from jax import lax
import numpy as np

D_MODEL = 1024
BATCH = 2
SEQ = 16384
DEPTH = 4

GRID_W = 64
CTX_LEN = 256
N_MIXERS = 2
POS_BASE = 10000.0
EPS = 1e-6
CONV_WIDTH = 31
CONV_PAD = CONV_WIDTH // 2
GLA_HEADS = 4
GLA_KEY_DIM = D_MODEL // 2
GLA_VAL_DIM = D_MODEL
GLA_HEAD_K = GLA_KEY_DIM // GLA_HEADS
GLA_HEAD_V = GLA_VAL_DIM // GLA_HEADS
GLA_GATE_RANK = 16
GLA_GATE_NORM = 16.0
GLA_CHUNK = 64
GLA_IN_DIM = 2 * GLA_KEY_DIM + 2 * GLA_VAL_DIM + 2 * GLA_GATE_RANK
MOE_GROUPS = 4
MOE_PER_GROUP = 8
MOE_EXPERTS = MOE_GROUPS * MOE_PER_GROUP
MOE_TOP_K = 2
MOE_FF = D_MODEL // 2
MOE_BLOCK = 128
N_CONV_LAYERS = (DEPTH + 1) // 2
N_GLA_LAYERS = DEPTH // 2

kernel_name = 'hybrid_conv_gla_hmoe_prefix_dit'


def rmsnorm(x, g):
    xf = x.astype(jnp.float32)
    y = xf * lax.rsqrt(jnp.mean(xf * xf, axis=-1, keepdims=True) + EPS)
    return (y * g).astype(x.dtype)


def layernorm(x, g, b):
    xf = x.astype(jnp.float32)
    mu = jnp.mean(xf, axis=-1, keepdims=True)
    var = jnp.mean(jnp.square(xf - mu), axis=-1, keepdims=True)
    return ((xf - mu) * lax.rsqrt(var + EPS) * g + b).astype(x.dtype)


def modulate(h, shift, scale):
    return h * (1 + scale) + shift


def sincos_2d(rows, width, dim):
    quarter = dim // 4
    omega = 1.0 / (POS_BASE ** (jnp.arange(quarter, dtype=jnp.float32) / quarter))
    r = jnp.broadcast_to(jnp.arange(rows, dtype=jnp.float32)[:, None], (rows, width)).reshape(-1)
    col = jnp.broadcast_to(jnp.arange(width, dtype=jnp.float32)[None, :], (rows, width)).reshape(-1)
    ar = r[:, None] * omega
    ac = col[:, None] * omega
    return jnp.concatenate([jnp.sin(ar), jnp.cos(ar), jnp.sin(ac), jnp.cos(ac)], axis=-1)


def conv_module(h, pw1_w, pw1_b, dw_w, dw_b, ln_g, ln_b, pw2_w, pw2_b):
    d = h.shape[-1]
    a, g = jnp.split(h @ pw1_w + pw1_b, 2, axis=-1)
    u = a * jax.nn.sigmoid(g)
    u = lax.conv_general_dilated(u, dw_w[:, None, :], window_strides=(1,),
                                 padding=[(CONV_PAD, CONV_PAD)],
                                 dimension_numbers=('NWC', 'WIO', 'NWC'),
                                 feature_group_count=d) + dw_b
    u = jax.nn.silu(layernorm(u, ln_g, ln_b))
    return u @ pw2_w + pw2_b


def gla_project(h, w_in, gk_up, gk_b):
    b, l, _ = h.shape
    z = h @ w_in
    idx = [GLA_KEY_DIM, 2 * GLA_KEY_DIM, 2 * GLA_KEY_DIM + GLA_VAL_DIM,
           2 * GLA_KEY_DIM + 2 * GLA_VAL_DIM, 2 * GLA_KEY_DIM + 2 * GLA_VAL_DIM + GLA_GATE_RANK]
    q, k, v, gate, lr_f, lr_b = jnp.split(z, idx, axis=-1)

    def heads(t, d):
        return t.reshape(b, l, GLA_HEADS, d).transpose(0, 2, 1, 3).astype(jnp.float32)

    log_f = jax.nn.log_sigmoid((lr_f @ gk_up[0] + gk_b[0]).astype(jnp.float32)) / GLA_GATE_NORM
    log_b = jax.nn.log_sigmoid((lr_b @ gk_up[1] + gk_b[1]).astype(jnp.float32)) / GLA_GATE_NORM
    q = heads(q, GLA_HEAD_K) * (GLA_HEAD_K ** -0.5)
    return (q, heads(k, GLA_HEAD_K), heads(v, GLA_HEAD_V), gate,
            heads(log_f, GLA_HEAD_K), heads(log_b, GLA_HEAD_K))


def gla_scan(q, k, v, logg, s0, with_out):
    b, h, l, _ = q.shape
    dv = v.shape[-1]
    n = l // GLA_CHUNK
    mask = jnp.tril(jnp.ones((GLA_CHUNK, GLA_CHUNK), dtype=bool))

    def chunks(t):
        return t.reshape(b, h, n, GLA_CHUNK, t.shape[-1]).transpose(2, 0, 1, 3, 4)

    def body(s, inp):
        qc, kc, vc, gc = inp
        cum = jnp.cumsum(gc, axis=-2)
        last = cum[..., -1:, :]
        s_new = (jnp.exp(last)[..., 0, :, None] * s
                 + jnp.einsum('bhcd,bhce->bhde', kc * jnp.exp(last - cum), vc))
        if not with_out:
            return s_new, None
        ref = cum[..., GLA_CHUNK // 2:GLA_CHUNK // 2 + 1, :]
        att = jnp.einsum('bhid,bhjd->bhij', qc * jnp.exp(cum - ref), kc * jnp.exp(ref - cum))
        att = jnp.where(mask, att, 0.0)
        o = (jnp.einsum('bhij,bhje->bhie', att, vc)
             + jnp.einsum('bhid,bhde->bhie', qc * jnp.exp(cum), s))
        return s_new, o

    s_fin, o = lax.scan(body, s0, (chunks(q), chunks(k), chunks(v), chunks(logg)))
    if not with_out:
        return s_fin, None
    return s_fin, o.transpose(1, 2, 0, 3, 4).reshape(b, h, l, dv)


def gla_bidir(q, k, v, log_f, log_b, s0_f, s0_b, with_out):
    flip = lambda t: t[:, :, ::-1]
    s_f, o_f = gla_scan(q, k, v, log_f, s0_f, with_out)
    s_b, o_b = gla_scan(flip(q), flip(k), flip(v), flip(log_b), s0_b, with_out)
    if not with_out:
        return s_f, s_b, None
    return s_f, s_b, o_f + flip(o_b)


def gla_output(o, gate, head_g, w_o):
    b, h, l, dv = o.shape
    on = o * lax.rsqrt(jnp.mean(o * o, axis=-1, keepdims=True) + EPS)
    on = on.transpose(0, 2, 1, 3).reshape(b, l, h * dv) * head_g
    return (on.astype(gate.dtype) * jax.nn.silu(gate)) @ w_o


def gla_mixer(h_ctx, h_lat, w_in, gk_up, gk_b, head_g, w_o, ctx_out):
    b = h_lat.shape[0]
    qc, kc, vc, gate_c, fc, bc = gla_project(h_ctx, w_in, gk_up, gk_b)
    zero = jnp.zeros((b, GLA_HEADS, GLA_HEAD_K, GLA_HEAD_V), jnp.float32)
    s_f, s_b, o_ctx = gla_bidir(qc, kc, vc, fc, bc, zero, zero, ctx_out)
    ql, kl, vl, gate_l, fl, bl = gla_project(h_lat, w_in, gk_up, gk_b)
    _, _, o_lat = gla_bidir(ql, kl, vl, fl, bl, s_f, s_b, True)
    y_lat = gla_output(o_lat, gate_l, head_g, w_o)
    y_ctx = gla_output(o_ctx, gate_c, head_g, w_o) if ctx_out else None
    return y_ctx, y_lat


def hier_moe(h, wg, bg, wr, br, w1, w3, w2):
    t, d = h.shape
    hf = h.astype(jnp.float32)
    g_logits = hf @ wg.astype(jnp.float32) + bg.astype(jnp.float32)
    g_prob = jax.nn.softmax(g_logits, axis=-1)
    g_idx = jnp.argmax(g_logits, axis=-1)
    g_w = jnp.take_along_axis(g_prob, g_idx[:, None], axis=-1)
    inner = (hf @ wr.astype(jnp.float32) + br.astype(jnp.float32)).reshape(t, MOE_GROUPS, MOE_PER_GROUP)
    sel = jnp.broadcast_to(g_idx[:, None, None], (t, 1, MOE_PER_GROUP))
    inner = jnp.take_along_axis(inner, sel, axis=1)[:, 0]
    top_v, top_i = lax.top_k(inner, MOE_TOP_K)
    top_w = jax.nn.softmax(top_v, axis=-1) * g_w
    expert = g_idx[:, None] * MOE_PER_GROUP + top_i

    a = t * MOE_TOP_K
    e_flat = expert.reshape(-1)
    w_flat = top_w.reshape(-1)
    tok_flat = jnp.repeat(jnp.arange(t, dtype=jnp.int32), MOE_TOP_K)
    order = jnp.argsort(e_flat)
    se, st, sw = e_flat[order], tok_flat[order], w_flat[order]
    counts = jnp.bincount(e_flat, length=MOE_EXPERTS)
    start = jnp.cumsum(counts) - counts
    padded = ((counts + MOE_BLOCK - 1) // MOE_BLOCK) * MOE_BLOCK
    pend = jnp.cumsum(padded)
    pstart = pend - padded
    dest = pstart[se] + (jnp.arange(a) - start[se])
    n_blocks = -(-a // MOE_BLOCK) + MOE_EXPERTS
    p = n_blocks * MOE_BLOCK
    tok_buf = jnp.full((p,), t, jnp.int32).at[dest].set(st)
    w_buf = jnp.zeros((p,), jnp.float32).at[dest].set(sw)
    blk_e = jnp.clip(jnp.searchsorted(pend, jnp.arange(n_blocks) * MOE_BLOCK, side='right'),
                     0, MOE_EXPERTS - 1)
    h_pad = jnp.concatenate([h, jnp.zeros((1, d), h.dtype)], axis=0)

    def block_ffn(args):
        e, idx = args
        xb = h_pad[idx]
        return (jax.nn.silu(xb @ w1[e]) * (xb @ w3[e])) @ w2[e]

    y = lax.map(block_ffn, (blk_e, tok_buf.reshape(n_blocks, MOE_BLOCK)))
    out = jnp.zeros((t + 1, d), jnp.float32).at[tok_buf].add(
        y.reshape(p, d).astype(jnp.float32) * w_buf[:, None])
    return out[:t].astype(h.dtype)


def setup_inputs(seed: int = 0) -> dict:
    key = jax.random.key(seed)
    ks = iter(jax.random.split(key, 32))
    nrm = lambda shape, scale: jax.random.normal(next(ks), shape, jnp.float32) * scale
    d, l, lc, lg = D_MODEL, DEPTH, N_CONV_LAYERS, N_GLA_LAYERS
    e, f = MOE_EXPERTS, MOE_FF
    return {
        'x': nrm((BATCH, SEQ, d), 1.0),
        'c': nrm((BATCH, d), 1.0),
        'ctx': nrm((BATCH, CTX_LEN, d), 1.0),
        'c_ctx': nrm((d,), 1.0),
        'ada_w': nrm((l, d, 6 * d), 0.5 * d ** -0.5),
        'ada_b': nrm((l, 6 * d), 0.01),
        'norm_g': 1.0 + nrm((l, 2, d), 0.01),
        'final_g': 1.0 + nrm((d,), 0.01),
        'conv_pw1_w': nrm((lc, d, 2 * d), d ** -0.5),
        'conv_pw1_b': nrm((lc, 2 * d), 0.01),
        'conv_dw_w': nrm((lc, CONV_WIDTH, d), CONV_WIDTH ** -0.5),
        'conv_dw_b': nrm((lc, d), 0.01),
        'conv_ln_g': 1.0 + nrm((lc, d), 0.01),
        'conv_ln_b': nrm((lc, d), 0.01),
        'conv_pw2_w': nrm((lc, d, d), d ** -0.5),
        'conv_pw2_b': nrm((lc, d), 0.01),
        'gla_w_in': nrm((lg, d, GLA_IN_DIM), d ** -0.5),
        'gla_gk_up': nrm((lg, 2, GLA_GATE_RANK, GLA_KEY_DIM), GLA_GATE_RANK ** -0.5),
        'gla_gk_b': nrm((lg, 2, GLA_KEY_DIM), 0.1),
        'gla_head_g': 1.0 + nrm((lg, GLA_VAL_DIM), 0.01),
        'gla_w_o': nrm((lg, GLA_VAL_DIM, d), GLA_VAL_DIM ** -0.5),
        'moe_wg': nrm((l, d, MOE_GROUPS), d ** -0.5),
        'moe_bg': nrm((l, MOE_GROUPS), 0.01),
        'moe_wr': nrm((l, d, e), d ** -0.5),
        'moe_br': nrm((l, e), 0.01),
        'moe_w1': nrm((l, e, d, f), d ** -0.5),
        'moe_w3': nrm((l, e, d, f), d ** -0.5),
        'moe_w2': nrm((l, e, f, d), f ** -0.5),
    }


def reference(x, c, ctx, c_ctx, ada_w, ada_b, norm_g, final_g,
              conv_pw1_w, conv_pw1_b, conv_dw_w, conv_dw_b, conv_ln_g, conv_ln_b,
              conv_pw2_w, conv_pw2_b, gla_w_in, gla_gk_up, gla_gk_b, gla_head_g, gla_w_o,
              moe_wg, moe_bg, moe_wr, moe_br, moe_w1, moe_w3, moe_w2):
    b, l, d = x.shape
    ROWS = l // GRID_W
    x = x + sincos_2d(ROWS, GRID_W, d).astype(x.dtype)[None]
    xc = ctx
    s_lat = jax.nn.silu(c)
    s_ctx = jax.nn.silu(c_ctx)
    for i in range(DEPTH):
        mixer, j = i % N_MIXERS, i // N_MIXERS
        ctx_live = any((m % N_MIXERS) == 1 for m in range(i + 1, DEPTH))
        mod_l = (s_lat @ ada_w[i] + ada_b[i])[:, None, :]
        mod_c = (s_ctx @ ada_w[i] + ada_b[i])[None, None, :]
        sh1, sc1, g1, sh2, sc2, g2 = jnp.split(mod_l, 6, axis=-1)
        csh1, csc1, cg1, csh2, csc2, cg2 = jnp.split(mod_c, 6, axis=-1)

        h_lat = modulate(rmsnorm(x, norm_g[i, 0]), sh1, sc1)
        if mixer == 0:
            cw = (conv_pw1_w[j], conv_pw1_b[j], conv_dw_w[j], conv_dw_b[j],
                  conv_ln_g[j], conv_ln_b[j], conv_pw2_w[j], conv_pw2_b[j])
            y_lat = conv_module(h_lat, *cw)
            if ctx_live:
                y_ctx = conv_module(modulate(rmsnorm(xc, norm_g[i, 0]), csh1, csc1), *cw)
        else:
            h_ctx = modulate(rmsnorm(xc, norm_g[i, 0]), csh1, csc1)
            y_ctx, y_lat = gla_mixer(h_ctx, h_lat, gla_w_in[j], gla_gk_up[j], gla_gk_b[j],
                                     gla_head_g[j], gla_w_o[j], ctx_live)
        x = x + g1 * y_lat
        if ctx_live:
            xc = xc + cg1 * y_ctx

        mw = (moe_wg[i], moe_bg[i], moe_wr[i], moe_br[i], moe_w1[i], moe_w3[i], moe_w2[i])
        h_lat = modulate(rmsnorm(x, norm_g[i, 1]), sh2, sc2)
        if ctx_live:
            h_ctx = modulate(rmsnorm(xc, norm_g[i, 1]), csh2, csc2)
            n_ctx = xc.shape[0] * xc.shape[1]
            out = hier_moe(jnp.concatenate([h_ctx.reshape(-1, d), h_lat.reshape(-1, d)], axis=0), *mw)
            xc = xc + cg2 * out[:n_ctx].reshape(xc.shape)
            x = x + g2 * out[n_ctx:].reshape(x.shape)
        else:
            x = x + g2 * hier_moe(h_lat.reshape(-1, d), *mw).reshape(x.shape)
    return rmsnorm(x, final_g)
```

```python
import functools

import jax
import jax.numpy as jnp
from jax import lax
from jax.experimental import pallas as pl
from jax.experimental.pallas import tpu as pltpu

F32 = jnp.float32
BF16 = jnp.bfloat16
HIGHEST = lax.Precision.HIGHEST

D = 1024
BATCH = 2
SEQ = 16384
CTX_LEN = 256
DEPTH = 4
GRID_W = 64
POS_BASE = 10000.0
EPS = 1e-6
CONV_WIDTH = 31
CONV_PAD = CONV_WIDTH // 2
HEADS = 4
KEY_DIM = D // 2
HEAD_K = KEY_DIM // HEADS
HEAD_V = D // HEADS
GATE_RANK = 16
GATE_NORM = 16.0
CHUNK = 64
GROUPS = 4
PER_GROUP = 8
EXPERTS = GROUPS * PER_GROUP
FF = D // 2

TM = 256
T_CTX = BATCH * CTX_LEN
T_LAT = BATCH * SEQ
T = T_CTX + T_LAT
NT = T // TM
CTX_TILES = T_CTX // TM
LAT_TILES = SEQ // TM
HALO = 16
CONV_ROWS = 64
BM = 256
A = T * 2
NB = A // BM + EXPERTS
P = NB * BM
GR = 256
LANES = 128
W_IN_PAD = 2 * KEY_DIM + 2 * D + LANES
VMEM_LIMIT = 48 * 1024 * 1024

_NT_DIMS = (((1,), (1,)), ((), ()))
_TN_DIMS = (((0,), (0,)), ((), ()))


def _params(*sem):
    return pltpu.CompilerParams(dimension_semantics=sem, vmem_limit_bytes=VMEM_LIMIT)


def _seg(i):
    return jnp.where(i < CTX_TILES, 0, jnp.where(i < CTX_TILES + LAT_TILES, 1, 2))


def _mod_spec():
    return pl.BlockSpec((None, 6, D), lambda i: (_seg(i), 0, 0))


def _row_spec(width):
    return pl.BlockSpec((TM, width), lambda i: (i, 0))


def _full_spec(shape):
    return pl.BlockSpec(shape, lambda i: (0,) * len(shape))


def _normmod(x, g, shift, scale):
    ms = jnp.mean(x * x, axis=-1, keepdims=True)
    return (x * lax.rsqrt(ms + EPS) * g) * (1.0 + scale) + shift


def _sigmoid(x):
    return 1.0 / (1.0 + jnp.exp(-x))


def _ada_kernel(c_ref, w_ref, b_ref, o_ref):
    c = c_ref[...]
    s = c * _sigmoid(c)
    o_ref[...] = jnp.dot(s, w_ref[...], precision=HIGHEST, preferred_element_type=F32) + b_ref[...]


def _ada_mods(c8, ada_w, ada_b):
    out = pl.pallas_call(
        _ada_kernel,
        grid=(DEPTH, 6),
        in_specs=[pl.BlockSpec((8, D), lambda l, n: (0, 0)),
                  pl.BlockSpec((None, D, D), lambda l, n: (l, 0, n)),
                  pl.BlockSpec((None, 1, D), lambda l, n: (l, 0, n))],
        out_specs=pl.BlockSpec((None, 8, D), lambda l, n: (l, 0, n)),
        out_shape=jax.ShapeDtypeStruct((DEPTH, 8, 6 * D), F32),
        compiler_params=_params("arbitrary", "arbitrary"),
        name="ada_mods",
    )(c8, ada_w, ada_b.reshape(DEPTH, 1, 6 * D))
    return out[:, :3].reshape(DEPTH, 3, 6, D)


def _pw1_kernel(x_ref, g_ref, mod_ref, w_ref, b_ref, u_ref):
    h = _normmod(x_ref[...], g_ref[...], mod_ref[0:1, :], mod_ref[1:2, :]).astype(BF16)
    a = jnp.dot(h, w_ref[:, :D], preferred_element_type=F32) + b_ref[:, :D]
    g = jnp.dot(h, w_ref[:, D:], preferred_element_type=F32) + b_ref[:, D:]
    u_ref[...] = a * _sigmoid(g)


def _pw1(x, g, mods, w, b):
    return pl.pallas_call(
        _pw1_kernel,
        grid=(NT,),
        in_specs=[_row_spec(D), _full_spec((1, D)), _mod_spec(),
                  _full_spec((D, 2 * D)), _full_spec((1, 2 * D))],
        out_specs=_row_spec(D),
        out_shape=jax.ShapeDtypeStruct((T, D), F32),
        compiler_params=_params("arbitrary"),
        name="conv_pw1_glu",
    )(x, g, mods, w, b)


def _conv_kernel(prev_ref, cur_ref, next_ref, x_ref, mod_ref, dww_ref, dwb_ref, lng_ref, lnb_ref,
                 w_ref, b_ref, o_ref, ext_ref, conv_ref):
    i = pl.program_id(0)
    lat0 = CTX_TILES
    lat1 = CTX_TILES + LAT_TILES
    is_start = (i < CTX_TILES) | (i == lat0) | (i == lat1)
    is_end = (i < CTX_TILES) | (i == lat1 - 1) | (i == NT - 1)
    ext_ref[0:HALO, :] = jnp.where(is_start, 0.0, prev_ref[...])
    ext_ref[HALO:HALO + TM, :] = cur_ref[...]
    ext_ref[HALO + TM:, :] = jnp.where(is_end, 0.0, next_ref[...])

    def lane_block(j, carry):
        lo = pl.multiple_of(j * LANES, LANES)
        for rb in range(TM // CONV_ROWS):
            acc = jnp.broadcast_to(dwb_ref[:, pl.ds(lo, LANES)], (CONV_ROWS, LANES))
            for k in range(CONV_WIDTH):
                r0 = rb * CONV_ROWS + k + HALO - CONV_PAD
                acc = acc + dww_ref[k:k + 1, pl.ds(lo, LANES)] * ext_ref[r0:r0 + CONV_ROWS, pl.ds(lo, LANES)]
            conv_ref[rb * CONV_ROWS:(rb + 1) * CONV_ROWS, pl.ds(lo, LANES)] = acc
        return carry

    lax.fori_loop(0, D // LANES, lane_block, 0)
    v = conv_ref[...]
    mu = jnp.mean(v, axis=-1, keepdims=True)
    vc = v - mu
    var = jnp.mean(vc * vc, axis=-1, keepdims=True)
    y = vc * lax.rsqrt(var + EPS) * lng_ref[...] + lnb_ref[...]
    y = (y * _sigmoid(y)).astype(BF16)
    out = jnp.dot(y, w_ref[...], preferred_element_type=F32) + b_ref[...]
    o_ref[...] = x_ref[...] + mod_ref[2:3, :] * out


def _conv(u, x, mods, dww, dwb, lng, lnb, w, b):
    per = TM // HALO
    return pl.pallas_call(
        _conv_kernel,
        grid=(NT,),
        in_specs=[pl.BlockSpec((HALO, D), lambda i: (jnp.maximum(i * per - 1, 0), 0)),
                  _row_spec(D),
                  pl.BlockSpec((HALO, D), lambda i: (jnp.minimum((i + 1) * per, T // HALO - 1), 0)),
                  _row_spec(D), _mod_spec(),
                  _full_spec((CONV_WIDTH + 1, D)), _full_spec((1, D)), _full_spec((1, D)),
                  _full_spec((1, D)), _full_spec((D, D)), _full_spec((1, D))],
        out_specs=_row_spec(D),
        out_shape=jax.ShapeDtypeStruct((T, D), F32),
        scratch_shapes=[pltpu.VMEM((TM + 2 * HALO, D), F32), pltpu.VMEM((TM, D), F32)],
        compiler_params=_params("arbitrary"),
        name="conv_dw_ln_pw2",
    )(u, u, u, x, mods, dww, dwb, lng, lnb, w, b)


def _log_sigmoid(z):
    return jnp.minimum(z, 0.0) - jnp.log(1.0 + jnp.exp(-jnp.abs(z)))


def _gla_proj_kernel(x_ref, g_ref, mod_ref, w_ref, gk_ref, gkb_ref,
                     q_ref, k_ref, v_ref, gate_ref, lf_ref, lb_ref):
    h = _normmod(x_ref[...], g_ref[...], mod_ref[0:1, :], mod_ref[1:2, :]).astype(BF16)
    kd = KEY_DIM
    q_ref[...] = jnp.dot(h, w_ref[:, 0:kd], preferred_element_type=F32) * (HEAD_K ** -0.5)
    k_ref[...] = jnp.dot(h, w_ref[:, kd:2 * kd], preferred_element_type=F32)
    v_ref[...] = jnp.dot(h, w_ref[:, 2 * kd:2 * kd + D], preferred_element_type=F32).astype(BF16)
    gate_ref[...] = jnp.dot(h, w_ref[:, 2 * kd + D:2 * kd + 2 * D], preferred_element_type=F32)
    codes = jnp.dot(h, w_ref[:, 2 * kd + 2 * D:], preferred_element_type=F32)
    z = jnp.dot(codes, gk_ref[...], precision=HIGHEST, preferred_element_type=F32) + gkb_ref[...]
    ls = _log_sigmoid(z) * (1.0 / GATE_NORM)
    lf_ref[...] = ls[:, :kd]
    lb_ref[...] = ls[:, kd:]


def _gla_proj(x, g, mods, w, gk, gkb):
    shapes = [jax.ShapeDtypeStruct((T, KEY_DIM), F32), jax.ShapeDtypeStruct((T, KEY_DIM), F32),
              jax.ShapeDtypeStruct((T, D), BF16), jax.ShapeDtypeStruct((T, D), F32),
              jax.ShapeDtypeStruct((T, KEY_DIM), F32), jax.ShapeDtypeStruct((T, KEY_DIM), F32)]
    return pl.pallas_call(
        _gla_proj_kernel,
        grid=(NT,),
        in_specs=[_row_spec(D), _full_spec((1, D)), _mod_spec(), _full_spec((D, W_IN_PAD)),
                  _full_spec((LANES, 2 * KEY_DIM)), _full_spec((1, 2 * KEY_DIM))],
        out_specs=[_row_spec(KEY_DIM), _row_spec(KEY_DIM), _row_spec(D), _row_spec(D),
                   _row_spec(KEY_DIM), _row_spec(KEY_DIM)],
        out_shape=shapes,
        compiler_params=_params("arbitrary"),
        name="gla_proj",
    )(x, g, mods, w, gk, gkb)


def _gla_scan_kernel(*refs, reverse):
    if reverse:
        q_ref, k_ref, v_ref, lg_ref, of_ref, gate_ref, hg_ref, o_ref, st_ref, cum_ref = refs
    else:
        q_ref, k_ref, v_ref, lg_ref, o_ref, st_ref, cum_ref = refs
    j = pl.program_id(1)

    @pl.when(j == 0)
    def _():
        st_ref[...] = jnp.zeros_like(st_ref)

    r = lax.broadcasted_iota(jnp.int32, (TM, TM), 0)
    c = lax.broadcasted_iota(jnp.int32, (TM, TM), 1)
    shift = CHUNK.bit_length() - 1
    same = (r >> shift) == (c >> shift)
    tri = jnp.where(same & ((c >= r) if reverse else (c <= r)), 1.0, 0.0).astype(BF16)
    lg = lg_ref[...]
    hi = lg.astype(BF16)
    r1 = lg - hi.astype(F32)
    mid = r1.astype(BF16)
    lo = (r1 - mid.astype(F32)).astype(BF16)
    cum_ref[...] = (jnp.dot(tri, hi, preferred_element_type=F32)
                    + jnp.dot(tri, mid, preferred_element_type=F32)
                    + jnp.dot(tri, lo, preferred_element_type=F32))

    rr = lax.broadcasted_iota(jnp.int32, (CHUNK, CHUNK), 0)
    cc = lax.broadcasted_iota(jnp.int32, (CHUNK, CHUNK), 1)
    mask = (cc >= rr) if reverse else (cc <= rr)
    i_last = 0 if reverse else CHUNK - 1
    i_ref = CHUNK // 2 - 1 if reverse else CHUNK // 2
    chunks = range(TM // CHUNK)
    for ci in (reversed(chunks) if reverse else chunks):
        r0 = ci * CHUNK
        for h in range(HEADS):
            k0, v0 = h * HEAD_K, h * HEAD_V
            cm = cum_ref[r0:r0 + CHUNK, k0:k0 + HEAD_K]
            last = cum_ref[r0 + i_last:r0 + i_last + 1, k0:k0 + HEAD_K]
            ref = cum_ref[r0 + i_ref:r0 + i_ref + 1, k0:k0 + HEAD_K]
            qc = q_ref[r0:r0 + CHUNK, k0:k0 + HEAD_K]
            kc = k_ref[r0:r0 + CHUNK, k0:k0 + HEAD_K]
            vc = v_ref[r0:r0 + CHUNK, v0:v0 + HEAD_V]
            qr = (qc * jnp.exp(cm - ref)).astype(BF16)
            kr = (kc * jnp.exp(ref - cm)).astype(BF16)
            att = lax.dot_general(qr, kr, _NT_DIMS, preferred_element_type=F32)
            att = jnp.where(mask, att, 0.0).astype(BF16)
            qd = (qc * jnp.exp(cm)).astype(BF16)
            st = st_ref[h]
            o = (jnp.dot(att, vc, preferred_element_type=F32)
                 + lax.dot_general(qd, st.astype(BF16), _NT_DIMS, preferred_element_type=F32))
            kdec = (kc * jnp.exp(last - cm)).astype(BF16)
            st_ref[h] = st * jnp.exp(last) + lax.dot_general(vc, kdec, _TN_DIMS,
                                                             preferred_element_type=F32)
            if reverse:
                o = o + of_ref[r0:r0 + CHUNK, v0:v0 + HEAD_V]
                on = o * lax.rsqrt(jnp.mean(o * o, axis=-1, keepdims=True) + EPS)
                on = on * hg_ref[:, v0:v0 + HEAD_V]
                gt = gate_ref[r0:r0 + CHUNK, v0:v0 + HEAD_V]
                o_ref[r0:r0 + CHUNK, v0:v0 + HEAD_V] = (on * (gt * _sigmoid(gt))).astype(BF16)
            else:
                o_ref[r0:r0 + CHUNK, v0:v0 + HEAD_V] = o


def _gla_scan(q, k, v, lg, extra, reverse):
    if reverse:
        def tile(b, j):
            return jnp.where(j == 0, b, CTX_TILES + LAT_TILES * (b + 1) - j)
    else:
        def tile(b, j):
            return jnp.where(j == 0, b, CTX_TILES + LAT_TILES * b + j - 1)

    def rows(width):
        return pl.BlockSpec((TM, width), lambda b, j: (tile(b, j), 0))

    in_specs = [rows(KEY_DIM), rows(KEY_DIM), rows(D), rows(KEY_DIM)]
    args = [q, k, v, lg]
    if reverse:
        in_specs += [rows(D), rows(D), pl.BlockSpec((1, D), lambda b, j: (0, 0))]
        args += list(extra)
    return pl.pallas_call(
        functools.partial(_gla_scan_kernel, reverse=reverse),
        grid=(BATCH, 1 + LAT_TILES),
        in_specs=in_specs,
        out_specs=rows(D),
        out_shape=jax.ShapeDtypeStruct((T, D), BF16 if reverse else F32),
        scratch_shapes=[pltpu.VMEM((HEADS, HEAD_V, HEAD_K), F32), pltpu.VMEM((TM, KEY_DIM), F32)],
        compiler_params=_params("arbitrary", "arbitrary"),
        name="gla_scan_bwd" if reverse else "gla_scan_fwd",
    )(*args)


def _matmul_res_kernel(a_ref, w_ref, x_ref, mod_ref, o_ref):
    y = jnp.dot(a_ref[...], w_ref[...], preferred_element_type=F32)
    o_ref[...] = x_ref[...] + mod_ref[2:3, :] * y


def _matmul_res(a, w, x, mods):
    return pl.pallas_call(
        _matmul_res_kernel,
        grid=(NT,),
        in_specs=[_row_spec(D), _full_spec((D, D)), _row_spec(D), _mod_spec()],
        out_specs=_row_spec(D),
        out_shape=jax.ShapeDtypeStruct((T, D), F32),
        compiler_params=_params("arbitrary"),
        name="gla_out_proj",
    )(a, w, x, mods)


def _router_kernel(x_ref, g_ref, mod_ref, w_ref, b_ref, h_ref, slab_ref):
    h = _normmod(x_ref[...], g_ref[...], mod_ref[3:4, :], mod_ref[4:5, :])
    h_ref[...] = h
    lg = jnp.dot(h, w_ref[...], precision=HIGHEST, preferred_element_type=F32) + b_ref[...]
    lane = lax.broadcasted_iota(jnp.int32, (TM, LANES), 1).astype(F32)
    neg = -jnp.inf
    far = float(LANES)
    is_group = (lane >= EXPERTS) & (lane < EXPERTS + GROUPS)
    glog = jnp.where(is_group, lg, neg)
    gmax = jnp.max(glog, axis=-1, keepdims=True)
    gidx = jnp.min(jnp.where(glog == gmax, lane, far), axis=-1, keepdims=True) - EXPERTS
    g_w = 1.0 / jnp.sum(jnp.exp(glog - gmax), axis=-1, keepdims=True)
    lo = gidx * PER_GROUP
    vals = jnp.where((lane >= lo) & (lane < lo + PER_GROUP), lg, neg)
    m1 = jnp.max(vals, axis=-1, keepdims=True)
    i1 = jnp.min(jnp.where(vals == m1, lane, far), axis=-1, keepdims=True)
    vals2 = jnp.where(lane == i1, neg, vals)
    m2 = jnp.max(vals2, axis=-1, keepdims=True)
    i2 = jnp.min(jnp.where(vals2 == m2, lane, far), axis=-1, keepdims=True)
    e = jnp.exp(m2 - m1)
    w1 = g_w / (1.0 + e)
    w2 = g_w * e / (1.0 + e)
    slab_ref[...] = jnp.where(lane == 0, i1, jnp.where(lane == 1, i2, jnp.where(
        lane == 2, w1, jnp.where(lane == 3, w2, 0.0))))


def _router(x, g, mods, w, b):
    return pl.pallas_call(
        _router_kernel,
        grid=(NT,),
        in_specs=[_row_spec(D), _full_spec((1, D)), _mod_spec(), _full_spec((D, LANES)),
                  _full_spec((1, LANES))],
        out_specs=[_row_spec(D), _row_spec(LANES)],
        out_shape=[jax.ShapeDtypeStruct((T, D), F32), jax.ShapeDtypeStruct((T, LANES), F32)],
        compiler_params=_params("arbitrary"),
        name="moe_router",
    )(x, g, mods, w, b)


def _gather_kernel(idx_ref, src_ref, dst_ref, sem):
    base = pl.program_id(0) * GR

    def issue(r, carry):
        row = idx_ref[base + r]
        pltpu.make_async_copy(src_ref.at[pl.ds(row, 1)], dst_ref.at[pl.ds(base + r, 1)], sem).start()
        return carry

    lax.fori_loop(0, GR, issue, 0)
    pltpu.make_async_copy(src_ref.at[pl.ds(0, GR)], dst_ref.at[pl.ds(base, GR)], sem).wait()


def _gather_rows(src, idx):
    n = idx.shape[0]
    return pl.pallas_call(
        _gather_kernel,
        grid_spec=pltpu.PrefetchScalarGridSpec(
            num_scalar_prefetch=1,
            grid=(n // GR,),
            in_specs=[pl.BlockSpec(memory_space=pl.ANY)],
            out_specs=pl.BlockSpec(memory_space=pl.ANY),
            scratch_shapes=[pltpu.SemaphoreType.DMA(())]),
        out_shape=jax.ShapeDtypeStruct((n, src.shape[1]), src.dtype),
        compiler_params=_params("arbitrary"),
        name="row_gather",
    )(idx, src)


def _ffn_kernel(be_ref, nu_ref, xs_ref, w1_ref, w3_ref, w2_ref, y_ref, w1b, w3b, w2b):
    i = pl.program_id(0)
    new_expert = (i == 0) | (be_ref[i] != be_ref[jnp.maximum(i - 1, 0)])

    @pl.when(new_expert)
    def _():
        w1b[...] = w1_ref[...].astype(BF16)
        w3b[...] = w3_ref[...].astype(BF16)
        w2b[...] = w2_ref[...].astype(BF16)

    @pl.when(i < nu_ref[0])
    def _():
        xb = xs_ref[...].astype(BF16)
        a = jnp.dot(xb, w1b[...], preferred_element_type=F32)
        b = jnp.dot(xb, w3b[...], preferred_element_type=F32)
        hh = (a * _sigmoid(a) * b).astype(BF16)
        y_ref[...] = jnp.dot(hh, w2b[...], preferred_element_type=F32)

    @pl.when(i >= nu_ref[0])
    def _():
        y_ref[...] = jnp.zeros_like(y_ref)


def _ffn(blk_e, n_used, xs, w1, w3, w2):
    return pl.pallas_call(
        _ffn_kernel,
        grid_spec=pltpu.PrefetchScalarGridSpec(
            num_scalar_prefetch=2,
            grid=(NB,),
            in_specs=[pl.BlockSpec((BM, D), lambda i, be, nu: (i, 0)),
                      pl.BlockSpec((None, D, FF), lambda i, be, nu: (be[i], 0, 0)),
                      pl.BlockSpec((None, D, FF), lambda i, be, nu: (be[i], 0, 0)),
                      pl.BlockSpec((None, FF, D), lambda i, be, nu: (be[i], 0, 0))],
            out_specs=pl.BlockSpec((BM, D), lambda i, be, nu: (i, 0)),
            scratch_shapes=[pltpu.VMEM((D, FF), BF16), pltpu.VMEM((D, FF), BF16),
                            pltpu.VMEM((FF, D), BF16)]),
        out_shape=jax.ShapeDtypeStruct((P, D), F32),
        compiler_params=_params("arbitrary"),
        name="moe_expert_ffn",
    )(blk_e, n_used, xs, w1, w3, w2)


def _combine_kernel(x_ref, y0_ref, y1_ref, slab_ref, mod_ref, fg_ref, o_ref, *, final):
    w0 = slab_ref[:, 2:3]
    w1 = slab_ref[:, 3:4]
    x = x_ref[...] + mod_ref[5:6, :] * (w0 * y0_ref[...] + w1 * y1_ref[...])
    if final:
        x = x * lax.rsqrt(jnp.mean(x * x, axis=-1, keepdims=True) + EPS) * fg_ref[...]
    o_ref[...] = x


def _combine(x, yg, slab, mods, final_g, final):
    return pl.pallas_call(
        functools.partial(_combine_kernel, final=final),
        grid=(NT,),
        in_specs=[_row_spec(D), _row_spec(D), pl.BlockSpec((TM, D), lambda i: (i + NT, 0)),
                  _row_spec(LANES), _mod_spec(), _full_spec((1, D))],
        out_specs=_row_spec(D),
        out_shape=jax.ShapeDtypeStruct((T, D), F32),
        compiler_params=_params("arbitrary"),
        name="moe_combine",
    )(x, yg, yg, slab, mods, final_g)


def _dispatch_plan(slab):
    e_flat = slab[:, 0:2].astype(jnp.int32).reshape(-1)
    onehot = (e_flat[:, None] == jnp.arange(EXPERTS, dtype=jnp.int32)[None, :]).astype(jnp.int32)
    csum = jnp.cumsum(onehot, axis=0)
    counts = csum[-1]
    rank = jnp.take_along_axis(csum, e_flat[:, None], axis=1)[:, 0] - 1
    padded = ((counts + BM - 1) // BM) * BM
    pend = jnp.cumsum(padded)
    pstart = pend - padded
    dest = pstart[e_flat] + rank
    tok = jnp.arange(A, dtype=jnp.int32) // 2
    tok_buf = jnp.zeros((P,), jnp.int32).at[dest].set(tok)
    blk_e = jnp.clip(jnp.searchsorted(pend, jnp.arange(NB, dtype=jnp.int32) * BM, side='right'),
                     0, EXPERTS - 1).astype(jnp.int32)
    n_used = (pend[-1:] // BM).astype(jnp.int32)
    dest2 = dest.reshape(T, 2).T.reshape(-1)
    return tok_buf, dest2, blk_e, n_used


def _moe(x, g, mods, wr_cat, br_cat, w1, w3, w2, final_g, final):
    h, slab = _router(x, g, mods, wr_cat, br_cat)
    tok_buf, dest2, blk_e, n_used = _dispatch_plan(slab)
    xs = _gather_rows(h, tok_buf)
    y = _ffn(blk_e, n_used, xs, w1, w3, w2)
    yg = _gather_rows(y, dest2)
    return _combine(x, yg, slab, mods, final_g, final)


def _sincos_2d(rows, width, dim):
    quarter = dim // 4
    omega = 1.0 / (POS_BASE ** (jnp.arange(quarter, dtype=F32) / quarter))
    ar = jnp.arange(rows, dtype=F32)[:, None] * omega
    ac = jnp.arange(width, dtype=F32)[:, None] * omega
    row_part = jnp.concatenate([jnp.sin(ar), jnp.cos(ar)], axis=-1)
    col_part = jnp.concatenate([jnp.sin(ac), jnp.cos(ac)], axis=-1)
    row_part = jnp.broadcast_to(row_part[:, None, :], (rows, width, dim // 2))
    col_part = jnp.broadcast_to(col_part[None, :, :], (rows, width, dim // 2))
    return jnp.concatenate([row_part, col_part], axis=-1).reshape(rows * width, dim)


def kernel(x, c, ctx, c_ctx, ada_w, ada_b, norm_g, final_g, conv_pw1_w, conv_pw1_b, conv_dw_w, conv_dw_b, conv_ln_g, conv_ln_b, conv_pw2_w, conv_pw2_b, gla_w_in, gla_gk_up, gla_gk_b, gla_head_g, gla_w_o, moe_wg, moe_bg, moe_wr, moe_br, moe_w1, moe_w3, moe_w2):
    assert x.shape == (BATCH, SEQ, D) and ctx.shape == (BATCH, CTX_LEN, D)
    pos = _sincos_2d(SEQ // GRID_W, GRID_W, D)
    xt = jnp.concatenate([ctx.reshape(T_CTX, D), (x + pos[None]).reshape(T_LAT, D)], axis=0)

    c8 = jnp.concatenate([c_ctx[None], c, jnp.zeros((8 - 1 - BATCH, D), F32)], axis=0)
    mods = _ada_mods(c8, ada_w, ada_b)
    fg = final_g.reshape(1, D)

    for i in range(DEPTH):
        j = i // 2
        g_mix = norm_g[i, 0].reshape(1, D)
        if i % 2 == 0:
            u = _pw1(xt, g_mix, mods[i], conv_pw1_w[j].astype(BF16), conv_pw1_b[j].reshape(1, 2 * D))
            dww = jnp.concatenate([conv_dw_w[j], jnp.zeros((1, D), F32)], axis=0)
            xt = _conv(u, xt, mods[i], dww, conv_dw_b[j].reshape(1, D), conv_ln_g[j].reshape(1, D),
                       conv_ln_b[j].reshape(1, D), conv_pw2_w[j].astype(BF16),
                       conv_pw2_b[j].reshape(1, D))
        else:
            w_in = jnp.pad(gla_w_in[j], ((0, 0), (0, W_IN_PAD - gla_w_in.shape[-1]))).astype(BF16)
            gk = jnp.zeros((LANES, 2 * KEY_DIM), F32)
            gk = gk.at[0:GATE_RANK, 0:KEY_DIM].set(gla_gk_up[j, 0])
            gk = gk.at[GATE_RANK:2 * GATE_RANK, KEY_DIM:].set(gla_gk_up[j, 1])
            gkb = gla_gk_b[j].reshape(1, 2 * KEY_DIM)
            q, k, v, gate, lf, lb = _gla_proj(xt, g_mix, mods[i], w_in, gk, gkb)
            o_f = _gla_scan(q, k, v, lf, None, reverse=False)
            og = _gla_scan(q, k, v, lb, (o_f, gate, gla_head_g[j].reshape(1, D)), reverse=True)
            xt = _matmul_res(og, gla_w_o[j].astype(BF16), xt, mods[i])
        wr_cat = jnp.concatenate([moe_wr[i], moe_wg[i],
                                  jnp.zeros((D, LANES - EXPERTS - GROUPS), F32)], axis=1)
        br_cat = jnp.concatenate([moe_br[i], moe_bg[i],
                                  jnp.zeros((LANES - EXPERTS - GROUPS,), F32)]).reshape(1, LANES)
        xt = _moe(xt, norm_g[i, 1].reshape(1, D), mods[i], wr_cat, br_cat,
                  moe_w1[i], moe_w3[i], moe_w2[i], fg, final=(i == DEPTH - 1))
    return xt[T_CTX:].reshape(BATCH, SEQ, D)
```

```python
import functools

import jax
import jax.numpy as jnp
from jax import lax
from jax.experimental import pallas as pl
from jax.experimental.pallas import tpu as pltpu

F32 = jnp.float32
BF16 = jnp.bfloat16
HIGHEST = lax.Precision.HIGHEST

D = 1024
BATCH = 2
SEQ = 16384
CTX_LEN = 256
DEPTH = 4
GRID_W = 64
POS_BASE = 10000.0
EPS = 1e-6
CONV_WIDTH = 31
CONV_PAD = CONV_WIDTH // 2
HEADS = 4
KEY_DIM = D // 2
HEAD_K = KEY_DIM // HEADS
HEAD_V = D // HEADS
GATE_RANK = 16
GATE_NORM = 16.0
CHUNK = 64
GROUPS = 4
PER_GROUP = 8
EXPERTS = GROUPS * PER_GROUP
FF = D // 2

TM = 256
T_CTX = BATCH * CTX_LEN
T_LAT = BATCH * SEQ
T = T_CTX + T_LAT
NT = T // TM
CTX_TILES = T_CTX // TM
LAT_TILES = SEQ // TM
HALO = 16
CONV_ROWS = 64
BM = 256
A = T * 2
NB = A // BM + EXPERTS
P = NB * BM
LANES = 128
SUB = D // LANES
W_IN_PAD = 2 * KEY_DIM + 2 * D + LANES
VMEM_LIMIT = 48 * 1024 * 1024

_NT_DIMS = (((1,), (1,)), ((), ()))
_TN_DIMS = (((0,), (0,)), ((), ()))


def _params(*sem):
    return pltpu.CompilerParams(dimension_semantics=sem, vmem_limit_bytes=VMEM_LIMIT)


def _seg(i):
    return jnp.where(i < CTX_TILES, 0, jnp.where(i < CTX_TILES + LAT_TILES, 1, 2))


def _mod_spec():
    return pl.BlockSpec((None, 6, D), lambda i: (_seg(i), 0, 0))


def _row_spec(width):
    return pl.BlockSpec((TM, width), lambda i: (i, 0))


def _full_spec(shape):
    return pl.BlockSpec(shape, lambda i: (0,) * len(shape))


def _normmod(x, g, shift, scale):
    ms = jnp.mean(x * x, axis=-1, keepdims=True)
    return (x * lax.rsqrt(ms + EPS) * g) * (1.0 + scale) + shift


def _sigmoid(x):
    return 1.0 / (1.0 + jnp.exp(-x))


def _ada_kernel(c_ref, w_ref, b_ref, o_ref):
    c = c_ref[...]
    s = c * _sigmoid(c)
    o_ref[...] = jnp.dot(s, w_ref[...], precision=HIGHEST, preferred_element_type=F32) + b_ref[...]


def _ada_mods(c8, ada_w, ada_b):
    out = pl.pallas_call(
        _ada_kernel,
        grid=(DEPTH, 6),
        in_specs=[pl.BlockSpec((8, D), lambda l, n: (0, 0)),
                  pl.BlockSpec((None, D, D), lambda l, n: (l, 0, n)),
                  pl.BlockSpec((None, 1, D), lambda l, n: (l, 0, n))],
        out_specs=pl.BlockSpec((None, 8, D), lambda l, n: (l, 0, n)),
        out_shape=jax.ShapeDtypeStruct((DEPTH, 8, 6 * D), F32),
        compiler_params=_params("arbitrary", "arbitrary"),
        name="ada_mods",
    )(c8, ada_w, ada_b.reshape(DEPTH, 1, 6 * D))
    return out[:, :3].reshape(DEPTH, 3, 6, D)


def _pw1_kernel(x_ref, g_ref, mod_ref, w_ref, b_ref, u_ref):
    h = _normmod(x_ref[...], g_ref[...], mod_ref[0:1, :], mod_ref[1:2, :]).astype(BF16)
    a = jnp.dot(h, w_ref[:, :D], preferred_element_type=F32) + b_ref[:, :D]
    g = jnp.dot(h, w_ref[:, D:], preferred_element_type=F32) + b_ref[:, D:]
    u_ref[...] = a * _sigmoid(g)


def _pw1(x, g, mods, w, b):
    return pl.pallas_call(
        _pw1_kernel,
        grid=(NT,),
        in_specs=[_row_spec(D), _full_spec((1, D)), _mod_spec(),
                  _full_spec((D, 2 * D)), _full_spec((1, 2 * D))],
        out_specs=_row_spec(D),
        out_shape=jax.ShapeDtypeStruct((T, D), F32),
        compiler_params=_params("arbitrary"),
        name="conv_pw1_glu",
    )(x, g, mods, w, b)


def _conv_kernel(prev_ref, cur_ref, next_ref, x_ref, mod_ref, dww_ref, dwb_ref, lng_ref, lnb_ref,
                 w_ref, b_ref, o_ref, ext_ref, conv_ref):
    i = pl.program_id(0)
    lat0 = CTX_TILES
    lat1 = CTX_TILES + LAT_TILES
    is_start = (i < CTX_TILES) | (i == lat0) | (i == lat1)
    is_end = (i < CTX_TILES) | (i == lat1 - 1) | (i == NT - 1)
    ext_ref[0:HALO, :] = jnp.where(is_start, 0.0, prev_ref[...])
    ext_ref[HALO:HALO + TM, :] = cur_ref[...]
    ext_ref[HALO + TM:, :] = jnp.where(is_end, 0.0, next_ref[...])

    def lane_block(j, carry):
        lo = pl.multiple_of(j * LANES, LANES)
        for rb in range(TM // CONV_ROWS):
            acc = jnp.broadcast_to(dwb_ref[:, pl.ds(lo, LANES)], (CONV_ROWS, LANES))
            for k in range(CONV_WIDTH):
                r0 = rb * CONV_ROWS + k + HALO - CONV_PAD
                acc = acc + dww_ref[k:k + 1, pl.ds(lo, LANES)] * ext_ref[r0:r0 + CONV_ROWS, pl.ds(lo, LANES)]
            conv_ref[rb * CONV_ROWS:(rb + 1) * CONV_ROWS, pl.ds(lo, LANES)] = acc
        return carry

    lax.fori_loop(0, D // LANES, lane_block, 0)
    v = conv_ref[...]
    mu = jnp.mean(v, axis=-1, keepdims=True)
    vc = v - mu
    var = jnp.mean(vc * vc, axis=-1, keepdims=True)
    y = vc * lax.rsqrt(var + EPS) * lng_ref[...] + lnb_ref[...]
    y = (y * _sigmoid(y)).astype(BF16)
    out = jnp.dot(y, w_ref[...], preferred_element_type=F32) + b_ref[...]
    o_ref[...] = x_ref[...] + mod_ref[2:3, :] * out


def _conv(u, x, mods, dww, dwb, lng, lnb, w, b):
    per = TM // HALO
    return pl.pallas_call(
        _conv_kernel,
        grid=(NT,),
        in_specs=[pl.BlockSpec((HALO, D), lambda i: (jnp.maximum(i * per - 1, 0), 0)),
                  _row_spec(D),
                  pl.BlockSpec((HALO, D), lambda i: (jnp.minimum((i + 1) * per, T // HALO - 1), 0)),
                  _row_spec(D), _mod_spec(),
                  _full_spec((CONV_WIDTH + 1, D)), _full_spec((1, D)), _full_spec((1, D)),
                  _full_spec((1, D)), _full_spec((D, D)), _full_spec((1, D))],
        out_specs=_row_spec(D),
        out_shape=jax.ShapeDtypeStruct((T, D), F32),
        scratch_shapes=[pltpu.VMEM((TM + 2 * HALO, D), F32), pltpu.VMEM((TM, D), F32)],
        compiler_params=_params("arbitrary"),
        name="conv_dw_ln_pw2",
    )(u, u, u, x, mods, dww, dwb, lng, lnb, w, b)


def _log_sigmoid(z):
    return jnp.minimum(z, 0.0) - jnp.log(1.0 + jnp.exp(-jnp.abs(z)))


def _gla_proj_kernel(x_ref, g_ref, mod_ref, w_ref, gk_ref, gkb_ref,
                     q_ref, k_ref, v_ref, gate_ref, lf_ref, lb_ref):
    h = _normmod(x_ref[...], g_ref[...], mod_ref[0:1, :], mod_ref[1:2, :]).astype(BF16)
    kd = KEY_DIM
    q_ref[...] = jnp.dot(h, w_ref[:, 0:kd], preferred_element_type=F32) * (HEAD_K ** -0.5)
    k_ref[...] = jnp.dot(h, w_ref[:, kd:2 * kd], preferred_element_type=F32)
    v_ref[...] = jnp.dot(h, w_ref[:, 2 * kd:2 * kd + D], preferred_element_type=F32).astype(BF16)
    gate_ref[...] = jnp.dot(h, w_ref[:, 2 * kd + D:2 * kd + 2 * D], preferred_element_type=F32)
    codes = jnp.dot(h, w_ref[:, 2 * kd + 2 * D:], preferred_element_type=F32)
    z = jnp.dot(codes, gk_ref[...], precision=HIGHEST, preferred_element_type=F32) + gkb_ref[...]
    ls = _log_sigmoid(z) * (1.0 / GATE_NORM)
    lf_ref[...] = ls[:, :kd]
    lb_ref[...] = ls[:, kd:]


def _gla_proj(x, g, mods, w, gk, gkb):
    shapes = [jax.ShapeDtypeStruct((T, KEY_DIM), F32), jax.ShapeDtypeStruct((T, KEY_DIM), F32),
              jax.ShapeDtypeStruct((T, D), BF16), jax.ShapeDtypeStruct((T, D), F32),
              jax.ShapeDtypeStruct((T, KEY_DIM), F32), jax.ShapeDtypeStruct((T, KEY_DIM), F32)]
    return pl.pallas_call(
        _gla_proj_kernel,
        grid=(NT,),
        in_specs=[_row_spec(D), _full_spec((1, D)), _mod_spec(), _full_spec((D, W_IN_PAD)),
                  _full_spec((LANES, 2 * KEY_DIM)), _full_spec((1, 2 * KEY_DIM))],
        out_specs=[_row_spec(KEY_DIM), _row_spec(KEY_DIM), _row_spec(D), _row_spec(D),
                   _row_spec(KEY_DIM), _row_spec(KEY_DIM)],
        out_shape=shapes,
        compiler_params=_params("arbitrary"),
        name="gla_proj",
    )(x, g, mods, w, gk, gkb)


def _gla_scan_kernel(*refs, reverse):
    if reverse:
        q_ref, k_ref, v_ref, lg_ref, of_ref, gate_ref, hg_ref, o_ref, st_ref, cum_ref = refs
    else:
        q_ref, k_ref, v_ref, lg_ref, o_ref, st_ref, cum_ref = refs
    j = pl.program_id(1)

    @pl.when(j == 0)
    def _():
        st_ref[...] = jnp.zeros_like(st_ref)

    r = lax.broadcasted_iota(jnp.int32, (TM, TM), 0)
    c = lax.broadcasted_iota(jnp.int32, (TM, TM), 1)
    shift = CHUNK.bit_length() - 1
    same = (r >> shift) == (c >> shift)
    tri = jnp.where(same & ((c >= r) if reverse else (c <= r)), 1.0, 0.0).astype(BF16)
    lg = lg_ref[...]
    hi = lg.astype(BF16)
    r1 = lg - hi.astype(F32)
    mid = r1.astype(BF16)
    lo = (r1 - mid.astype(F32)).astype(BF16)
    cum_ref[...] = (jnp.dot(tri, hi, preferred_element_type=F32)
                    + jnp.dot(tri, mid, preferred_element_type=F32)
                    + jnp.dot(tri, lo, preferred_element_type=F32))

    rr = lax.broadcasted_iota(jnp.int32, (CHUNK, CHUNK), 0)
    cc = lax.broadcasted_iota(jnp.int32, (CHUNK, CHUNK), 1)
    mask = (cc >= rr) if reverse else (cc <= rr)
    i_last = 0 if reverse else CHUNK - 1
    i_ref = CHUNK // 2 - 1 if reverse else CHUNK // 2
    chunks = range(TM // CHUNK)
    for ci in (reversed(chunks) if reverse else chunks):
        r0 = ci * CHUNK
        for h in range(HEADS):
            k0, v0 = h * HEAD_K, h * HEAD_V
            cm = cum_ref[r0:r0 + CHUNK, k0:k0 + HEAD_K]
            last = cum_ref[r0 + i_last:r0 + i_last + 1, k0:k0 + HEAD_K]
            ref = cum_ref[r0 + i_ref:r0 + i_ref + 1, k0:k0 + HEAD_K]
            qc = q_ref[r0:r0 + CHUNK, k0:k0 + HEAD_K]
            kc = k_ref[r0:r0 + CHUNK, k0:k0 + HEAD_K]
            vc = v_ref[r0:r0 + CHUNK, v0:v0 + HEAD_V]
            qr = (qc * jnp.exp(cm - ref)).astype(BF16)
            kr = (kc * jnp.exp(ref - cm)).astype(BF16)
            att = lax.dot_general(qr, kr, _NT_DIMS, preferred_element_type=F32)
            att = jnp.where(mask, att, 0.0).astype(BF16)
            qd = (qc * jnp.exp(cm)).astype(BF16)
            st = st_ref[h]
            o = (jnp.dot(att, vc, preferred_element_type=F32)
                 + lax.dot_general(qd, st.astype(BF16), _NT_DIMS, preferred_element_type=F32))
            kdec = (kc * jnp.exp(last - cm)).astype(BF16)
            st_ref[h] = st * jnp.exp(last) + lax.dot_general(vc, kdec, _TN_DIMS,
                                                             preferred_element_type=F32)
            if reverse:
                o = o + of_ref[r0:r0 + CHUNK, v0:v0 + HEAD_V]
                on = o * lax.rsqrt(jnp.mean(o * o, axis=-1, keepdims=True) + EPS)
                on = on * hg_ref[:, v0:v0 + HEAD_V]
                gt = gate_ref[r0:r0 + CHUNK, v0:v0 + HEAD_V]
                o_ref[r0:r0 + CHUNK, v0:v0 + HEAD_V] = (on * (gt * _sigmoid(gt))).astype(BF16)
            else:
                o_ref[r0:r0 + CHUNK, v0:v0 + HEAD_V] = o


def _gla_scan(q, k, v, lg, extra, reverse):
    if reverse:
        def tile(b, j):
            return jnp.where(j == 0, b, CTX_TILES + LAT_TILES * (b + 1) - j)
    else:
        def tile(b, j):
            return jnp.where(j == 0, b, CTX_TILES + LAT_TILES * b + j - 1)

    def rows(width):
        return pl.BlockSpec((TM, width), lambda b, j: (tile(b, j), 0))

    in_specs = [rows(KEY_DIM), rows(KEY_DIM), rows(D), rows(KEY_DIM)]
    args = [q, k, v, lg]
    if reverse:
        in_specs += [rows(D), rows(D), pl.BlockSpec((1, D), lambda b, j: (0, 0))]
        args += list(extra)
    return pl.pallas_call(
        functools.partial(_gla_scan_kernel, reverse=reverse),
        grid=(BATCH, 1 + LAT_TILES),
        in_specs=in_specs,
        out_specs=rows(D),
        out_shape=jax.ShapeDtypeStruct((T, D), BF16 if reverse else F32),
        scratch_shapes=[pltpu.VMEM((HEADS, HEAD_V, HEAD_K), F32), pltpu.VMEM((TM, KEY_DIM), F32)],
        compiler_params=_params("arbitrary", "arbitrary"),
        name="gla_scan_bwd" if reverse else "gla_scan_fwd",
    )(*args)


def _matmul_res_kernel(a_ref, w_ref, x_ref, mod_ref, o_ref):
    y = jnp.dot(a_ref[...], w_ref[...], preferred_element_type=F32)
    o_ref[...] = x_ref[...] + mod_ref[2:3, :] * y


def _matmul_res(a, w, x, mods):
    return pl.pallas_call(
        _matmul_res_kernel,
        grid=(NT,),
        in_specs=[_row_spec(D), _full_spec((D, D)), _row_spec(D), _mod_spec()],
        out_specs=_row_spec(D),
        out_shape=jax.ShapeDtypeStruct((T, D), F32),
        compiler_params=_params("arbitrary"),
        name="gla_out_proj",
    )(a, w, x, mods)


def _router_kernel(x_ref, g_ref, mod_ref, w_ref, b_ref, h_ref, slab_ref):
    h = _normmod(x_ref[...], g_ref[...], mod_ref[3:4, :], mod_ref[4:5, :])
    for s in range(SUB):
        h_ref[pl.ds(s, TM, stride=SUB), :] = h[:, s * LANES:(s + 1) * LANES]
    lg = jnp.dot(h, w_ref[...], precision=HIGHEST, preferred_element_type=F32) + b_ref[...]
    lane = lax.broadcasted_iota(jnp.int32, (TM, LANES), 1).astype(F32)
    neg = -jnp.inf
    far = float(LANES)
    is_group = (lane >= EXPERTS) & (lane < EXPERTS + GROUPS)
    glog = jnp.where(is_group, lg, neg)
    gmax = jnp.max(glog, axis=-1, keepdims=True)
    gidx = jnp.min(jnp.where(glog == gmax, lane, far), axis=-1, keepdims=True) - EXPERTS
    g_w = 1.0 / jnp.sum(jnp.exp(glog - gmax), axis=-1, keepdims=True)
    lo = gidx * PER_GROUP
    vals = jnp.where((lane >= lo) & (lane < lo + PER_GROUP), lg, neg)
    m1 = jnp.max(vals, axis=-1, keepdims=True)
    i1 = jnp.min(jnp.where(vals == m1, lane, far), axis=-1, keepdims=True)
    vals2 = jnp.where(lane == i1, neg, vals)
    m2 = jnp.max(vals2, axis=-1, keepdims=True)
    i2 = jnp.min(jnp.where(vals2 == m2, lane, far), axis=-1, keepdims=True)
    e = jnp.exp(m2 - m1)
    w1 = g_w / (1.0 + e)
    w2 = g_w * e / (1.0 + e)
    slab_ref[...] = jnp.where(lane == 0, i1, jnp.where(lane == 1, i2, jnp.where(
        lane == 2, w1, jnp.where(lane == 3, w2, 0.0))))


def _router(x, g, mods, w, b):
    return pl.pallas_call(
        _router_kernel,
        grid=(NT,),
        in_specs=[_row_spec(D), _full_spec((1, D)), _mod_spec(), _full_spec((D, LANES)),
                  _full_spec((1, LANES))],
        out_specs=[pl.BlockSpec((TM * SUB, LANES), lambda i: (i, 0)), _row_spec(LANES)],
        out_shape=[jax.ShapeDtypeStruct((T * SUB, LANES), F32), jax.ShapeDtypeStruct((T, LANES), F32)],
        compiler_params=_params("arbitrary"),
        name="moe_router",
    )(x, g, mods, w, b)


def _ffn_kernel(be_ref, nv_ref, asg_ref, hs_ref, w1_ref, w3_ref, w2_ref, ys_ref,
                xg, yb, gsem, ssem, w1b, w3b, w2b):
    i = pl.program_id(0)
    slot = i % 2

    def row_copy_in(blk, sl, r):
        tok = asg_ref[blk * BM + r] >> 1
        return pltpu.make_async_copy(hs_ref.at[pl.ds(pl.multiple_of(tok * SUB, SUB), SUB)],
                                     xg.at[sl, pl.ds(pl.multiple_of(r * SUB, SUB), SUB)], gsem.at[sl])

    def row_copy_out(blk, sl, r):
        a = asg_ref[blk * BM + r]
        orow = (a & 1) * T + (a >> 1)
        return pltpu.make_async_copy(yb.at[sl, pl.ds(pl.multiple_of(r * SUB, SUB), SUB)],
                                     ys_ref.at[pl.ds(pl.multiple_of(orow * SUB, SUB), SUB)], ssem.at[sl])

    def start_rows(copy, blk, sl):
        def body(r, carry):
            copy(blk, sl, r).start()
            return carry
        lax.fori_loop(0, nv_ref[blk], body, 0)

    def wait_gather(blk, sl):
        n = nv_ref[blk]

        @pl.when(n > 0)
        def _():
            pltpu.make_async_copy(hs_ref.at[pl.ds(0, n * SUB)], xg.at[sl, pl.ds(0, n * SUB)],
                                  gsem.at[sl]).wait()

    def wait_scatter(blk, sl):
        n = nv_ref[blk]

        @pl.when(n > 0)
        def _():
            pltpu.make_async_copy(yb.at[sl, pl.ds(0, n * SUB)], ys_ref.at[pl.ds(0, n * SUB)],
                                  ssem.at[sl]).wait()

    @pl.when(i == 0)
    def _():
        xg[...] = jnp.zeros_like(xg)
        start_rows(row_copy_in, 0, 0)

    @pl.when(i + 1 < NB)
    def _():
        start_rows(row_copy_in, i + 1, 1 - slot)

    new_expert = (i == 0) | (be_ref[i] != be_ref[jnp.maximum(i - 1, 0)])

    @pl.when(new_expert)
    def _():
        w1b[...] = w1_ref[...].astype(BF16)
        w3b[...] = w3_ref[...].astype(BF16)
        w2b[...] = w2_ref[...].astype(BF16)

    wait_gather(i, slot)

    @pl.when(i >= 2)
    def _():
        wait_scatter(i - 2, slot)

    @pl.when(nv_ref[i] > 0)
    def _():
        xb = jnp.concatenate([xg[slot, pl.ds(s, BM, stride=SUB), :] for s in range(SUB)],
                             axis=-1).astype(BF16)
        a = jnp.dot(xb, w1b[...], preferred_element_type=F32)
        b = jnp.dot(xb, w3b[...], preferred_element_type=F32)
        hh = (a * _sigmoid(a) * b).astype(BF16)
        y = jnp.dot(hh, w2b[...], preferred_element_type=F32)
        for s in range(SUB):
            yb[slot, pl.ds(s, BM, stride=SUB), :] = y[:, s * LANES:(s + 1) * LANES]
        start_rows(row_copy_out, i, slot)

    @pl.when(i == NB - 1)
    def _():
        wait_scatter(NB - 2, 1 - slot)
        wait_scatter(NB - 1, slot)


def _ffn(blk_e, n_valid, asg_buf, hs, w1, w3, w2):
    w_in = pl.BlockSpec((None, D, FF), lambda i, be, nv, asg: (be[i], 0, 0))
    w_out = pl.BlockSpec((None, FF, D), lambda i, be, nv, asg: (be[i], 0, 0))
    return pl.pallas_call(
        _ffn_kernel,
        grid_spec=pltpu.PrefetchScalarGridSpec(
            num_scalar_prefetch=3,
            grid=(NB,),
            in_specs=[pl.BlockSpec(memory_space=pl.ANY), w_in, w_in, w_out],
            out_specs=pl.BlockSpec(memory_space=pl.ANY),
            scratch_shapes=[pltpu.VMEM((2, BM * SUB, LANES), F32), pltpu.VMEM((2, BM * SUB, LANES), F32),
                            pltpu.SemaphoreType.DMA((2,)), pltpu.SemaphoreType.DMA((2,)),
                            pltpu.VMEM((D, FF), BF16), pltpu.VMEM((D, FF), BF16),
                            pltpu.VMEM((FF, D), BF16)]),
        out_shape=jax.ShapeDtypeStruct((A * SUB, LANES), F32),
        compiler_params=_params("arbitrary"),
        name="moe_expert_ffn",
    )(blk_e, n_valid, asg_buf, hs, w1, w3, w2)


def _combine_kernel(x_ref, y0_ref, y1_ref, slab_ref, mod_ref, fg_ref, o_ref, *, final):
    w0 = slab_ref[:, 2:3]
    w1 = slab_ref[:, 3:4]
    y = jnp.concatenate([w0 * y0_ref[pl.ds(s, TM, stride=SUB), :] + w1 * y1_ref[pl.ds(s, TM, stride=SUB), :]
                         for s in range(SUB)], axis=-1)
    x = x_ref[...] + mod_ref[5:6, :] * y
    if final:
        x = x * lax.rsqrt(jnp.mean(x * x, axis=-1, keepdims=True) + EPS) * fg_ref[...]
    o_ref[...] = x


def _combine(x, ys, slab, mods, final_g, final):
    return pl.pallas_call(
        functools.partial(_combine_kernel, final=final),
        grid=(NT,),
        in_specs=[_row_spec(D), pl.BlockSpec((TM * SUB, LANES), lambda i: (i, 0)),
                  pl.BlockSpec((TM * SUB, LANES), lambda i: (i + NT, 0)),
                  _row_spec(LANES), _mod_spec(), _full_spec((1, D))],
        out_specs=_row_spec(D),
        out_shape=jax.ShapeDtypeStruct((T, D), F32),
        compiler_params=_params("arbitrary"),
        name="moe_combine",
    )(x, ys, ys, slab, mods, final_g)


def _dispatch_plan(slab):
    e_flat = slab[:, 0:2].astype(jnp.int32).reshape(-1)
    onehot = (e_flat[:, None] == jnp.arange(EXPERTS, dtype=jnp.int32)[None, :]).astype(jnp.int32)
    csum = jnp.cumsum(onehot, axis=0)
    counts = csum[-1]
    rank = jnp.take_along_axis(csum, e_flat[:, None], axis=1)[:, 0] - 1
    padded = ((counts + BM - 1) // BM) * BM
    pend = jnp.cumsum(padded)
    pstart = pend - padded
    dest = pstart[e_flat] + rank
    asg_buf = jnp.zeros((P,), jnp.int32).at[dest].set(jnp.arange(A, dtype=jnp.int32))
    blk0 = jnp.arange(NB, dtype=jnp.int32) * BM
    blk_e = jnp.clip(jnp.searchsorted(pend, blk0, side='right'), 0, EXPERTS - 1).astype(jnp.int32)
    n_valid = jnp.clip(counts[blk_e] - (blk0 - pstart[blk_e]), 0, BM).astype(jnp.int32)
    return asg_buf, blk_e, n_valid


def _moe(x, g, mods, wr_cat, br_cat, w1, w3, w2, final_g, final):
    hs, slab = _router(x, g, mods, wr_cat, br_cat)
    asg_buf, blk_e, n_valid = _dispatch_plan(slab)
    ys = _ffn(blk_e, n_valid, asg_buf, hs, w1, w3, w2)
    return _combine(x, ys, slab, mods, final_g, final)


def _sincos_2d(rows, width, dim):
    quarter = dim // 4
    omega = 1.0 / (POS_BASE ** (jnp.arange(quarter, dtype=F32) / quarter))
    ar = jnp.arange(rows, dtype=F32)[:, None] * omega
    ac = jnp.arange(width, dtype=F32)[:, None] * omega
    row_part = jnp.concatenate([jnp.sin(ar), jnp.cos(ar)], axis=-1)
    col_part = jnp.concatenate([jnp.sin(ac), jnp.cos(ac)], axis=-1)
    row_part = jnp.broadcast_to(row_part[:, None, :], (rows, width, dim // 2))
    col_part = jnp.broadcast_to(col_part[None, :, :], (rows, width, dim // 2))
    return jnp.concatenate([row_part, col_part], axis=-1).reshape(rows * width, dim)


def kernel(x, c, ctx, c_ctx, ada_w, ada_b, norm_g, final_g, conv_pw1_w, conv_pw1_b, conv_dw_w, conv_dw_b, conv_ln_g, conv_ln_b, conv_pw2_w, conv_pw2_b, gla_w_in, gla_gk_up, gla_gk_b, gla_head_g, gla_w_o, moe_wg, moe_bg, moe_wr, moe_br, moe_w1, moe_w3, moe_w2):
    assert x.shape == (BATCH, SEQ, D) and ctx.shape == (BATCH, CTX_LEN, D)
    pos = _sincos_2d(SEQ // GRID_W, GRID_W, D)
    xt = jnp.concatenate([ctx.reshape(T_CTX, D), (x + pos[None]).reshape(T_LAT, D)], axis=0)

    c8 = jnp.concatenate([c_ctx[None], c, jnp.zeros((8 - 1 - BATCH, D), F32)], axis=0)
    mods = _ada_mods(c8, ada_w, ada_b)
    fg = final_g.reshape(1, D)

    for i in range(DEPTH):
        j = i // 2
        g_mix = norm_g[i, 0].reshape(1, D)
        if i % 2 == 0:
            u = _pw1(xt, g_mix, mods[i], conv_pw1_w[j].astype(BF16), conv_pw1_b[j].reshape(1, 2 * D))
            dww = jnp.concatenate([conv_dw_w[j], jnp.zeros((1, D), F32)], axis=0)
            xt = _conv(u, xt, mods[i], dww, conv_dw_b[j].reshape(1, D), conv_ln_g[j].reshape(1, D),
                       conv_ln_b[j].reshape(1, D), conv_pw2_w[j].astype(BF16),
                       conv_pw2_b[j].reshape(1, D))
        else:
            w_in = jnp.pad(gla_w_in[j], ((0, 0), (0, W_IN_PAD - gla_w_in.shape[-1]))).astype(BF16)
            gk = jnp.zeros((LANES, 2 * KEY_DIM), F32)
            gk = gk.at[0:GATE_RANK, 0:KEY_DIM].set(gla_gk_up[j, 0])
            gk = gk.at[GATE_RANK:2 * GATE_RANK, KEY_DIM:].set(gla_gk_up[j, 1])
            gkb = gla_gk_b[j].reshape(1, 2 * KEY_DIM)
            q, k, v, gate, lf, lb = _gla_proj(xt, g_mix, mods[i], w_in, gk, gkb)
            o_f = _gla_scan(q, k, v, lf, None, reverse=False)
            og = _gla_scan(q, k, v, lb, (o_f, gate, gla_head_g[j].reshape(1, D)), reverse=True)
            xt = _matmul_res(og, gla_w_o[j].astype(BF16), xt, mods[i])
        wr_cat = jnp.concatenate([moe_wr[i], moe_wg[i],
                                  jnp.zeros((D, LANES - EXPERTS - GROUPS), F32)], axis=1)
        br_cat = jnp.concatenate([moe_br[i], moe_bg[i],
                                  jnp.zeros((LANES - EXPERTS - GROUPS,), F32)]).reshape(1, LANES)
        xt = _moe(xt, norm_g[i, 1].reshape(1, D), mods[i], wr_cat, br_cat,
                  moe_w1[i], moe_w3[i], moe_w2[i], fg, final=(i == DEPTH - 1))
    return xt[T_CTX:].reshape(BATCH, SEQ, D)
```

```python
import functools

import jax
import jax.numpy as jnp
from jax import lax
from jax.experimental import pallas as pl
from jax.experimental.pallas import tpu as pltpu

F32 = jnp.float32
BF16 = jnp.bfloat16
HIGHEST = lax.Precision.HIGHEST

D = 1024
BATCH = 2
SEQ = 16384
CTX_LEN = 256
DEPTH = 4
GRID_W = 64
POS_BASE = 10000.0
EPS = 1e-6
CONV_WIDTH = 31
CONV_PAD = CONV_WIDTH // 2
HEADS = 4
KEY_DIM = D // 2
HEAD_K = KEY_DIM // HEADS
HEAD_V = D // HEADS
GATE_RANK = 16
GATE_NORM = 16.0
CHUNK = 64
GROUPS = 4
PER_GROUP = 8
EXPERTS = GROUPS * PER_GROUP
FF = D // 2

TM = 256
T_CTX = BATCH * CTX_LEN
T_LAT = BATCH * SEQ
T = T_CTX + T_LAT
NT = T // TM
CTX_TILES = T_CTX // TM
LAT_TILES = SEQ // TM
HALO = 16
CONV_ROWS = 128
CONV_WIN = CONV_ROWS + 2 * HALO
SUBLANES = 8
BM = 256
PAIRS = PER_GROUP * (PER_GROUP - 1) // 2
NCLS = GROUPS * PAIRS
NB = T // BM + NCLS
P = NB * BM
LANES = 128
SUB = D // LANES
YROWS = 2 * SUB
W_IN_PAD = 2 * KEY_DIM + 2 * D + LANES
VMEM_LIMIT = 48 * 1024 * 1024

_NT_DIMS = (((1,), (1,)), ((), ()))
_TN_DIMS = (((0,), (0,)), ((), ()))


def _params(*sem):
    return pltpu.CompilerParams(dimension_semantics=sem, vmem_limit_bytes=VMEM_LIMIT)


def _seg(i):
    return jnp.where(i < CTX_TILES, 0, jnp.where(i < CTX_TILES + LAT_TILES, 1, 2))


def _mod_spec():
    return pl.BlockSpec((None, 6, D), lambda i: (_seg(i), 0, 0))


def _row_spec(width):
    return pl.BlockSpec((TM, width), lambda i: (i, 0))


def _full_spec(shape):
    return pl.BlockSpec(shape, lambda i: (0,) * len(shape))


def _normmod(x, g, shift, scale):
    ms = jnp.mean(x * x, axis=-1, keepdims=True)
    return (x * lax.rsqrt(ms + EPS) * g) * (1.0 + scale) + shift


def _sigmoid(x):
    return 1.0 / (1.0 + jnp.exp(-x))


def _ada_kernel(c_ref, w_ref, b_ref, o_ref):
    c = c_ref[...]
    s = c * _sigmoid(c)
    o_ref[...] = jnp.dot(s, w_ref[...], precision=HIGHEST, preferred_element_type=F32) + b_ref[...]


def _ada_mods(c8, ada_w, ada_b):
    out = pl.pallas_call(
        _ada_kernel,
        grid=(DEPTH, 6),
        in_specs=[pl.BlockSpec((8, D), lambda l, n: (0, 0)),
                  pl.BlockSpec((None, D, D), lambda l, n: (l, 0, n)),
                  pl.BlockSpec((None, 1, D), lambda l, n: (l, 0, n))],
        out_specs=pl.BlockSpec((None, 8, D), lambda l, n: (l, 0, n)),
        out_shape=jax.ShapeDtypeStruct((DEPTH, 8, 6 * D), F32),
        compiler_params=_params("arbitrary", "arbitrary"),
        name="ada_mods",
    )(c8, ada_w, ada_b.reshape(DEPTH, 1, 6 * D))
    return out[:, :3].reshape(DEPTH, 3, 6, D)


def _pw1_kernel(x_ref, g_ref, mod_ref, w_ref, b_ref, u_ref):
    h = _normmod(x_ref[...], g_ref[...], mod_ref[0:1, :], mod_ref[1:2, :]).astype(BF16)
    a = jnp.dot(h, w_ref[:, :D], preferred_element_type=F32) + b_ref[:, :D]
    g = jnp.dot(h, w_ref[:, D:], preferred_element_type=F32) + b_ref[:, D:]
    u_ref[...] = a * _sigmoid(g)


def _pw1(x, g, mods, w, b):
    return pl.pallas_call(
        _pw1_kernel,
        grid=(NT,),
        in_specs=[_row_spec(D), _full_spec((1, D)), _mod_spec(),
                  _full_spec((D, 2 * D)), _full_spec((1, 2 * D))],
        out_specs=_row_spec(D),
        out_shape=jax.ShapeDtypeStruct((T, D), F32),
        compiler_params=_params("arbitrary"),
        name="conv_pw1_glu",
    )(x, g, mods, w, b)


def _conv_kernel(prev_ref, cur_ref, next_ref, x_ref, mod_ref, dww_ref, dwb_ref, lng_ref, lnb_ref,
                 w_ref, b_ref, o_ref, ext_ref, conv_ref):
    i = pl.program_id(0)
    lat0 = CTX_TILES
    lat1 = CTX_TILES + LAT_TILES
    is_start = (i < CTX_TILES) | (i == lat0) | (i == lat1)
    is_end = (i < CTX_TILES) | (i == lat1 - 1) | (i == NT - 1)
    ext_ref[0:HALO, :] = jnp.where(is_start, 0.0, prev_ref[...])
    ext_ref[HALO:HALO + TM, :] = cur_ref[...]
    ext_ref[HALO + TM:, :] = jnp.where(is_end, 0.0, next_ref[...])

    def lane_block(j, carry):
        lo = pl.multiple_of(j * LANES, LANES)
        for rb in range(TM // CONV_ROWS):
            r0 = rb * CONV_ROWS
            win = ext_ref[r0:r0 + CONV_WIN, pl.ds(lo, LANES)]
            acc = jnp.broadcast_to(dwb_ref[:, pl.ds(lo, LANES)], (CONV_ROWS, LANES))
            for rho in range(SUBLANES):
                shifted = win if rho == 0 else pltpu.roll(win, shift=CONV_WIN - rho, axis=0)
                for k in range(CONV_WIDTH):
                    off = k + HALO - CONV_PAD - rho
                    if off % SUBLANES == 0:
                        acc = acc + dww_ref[k:k + 1, pl.ds(lo, LANES)] * shifted[off:off + CONV_ROWS]
            conv_ref[r0:r0 + CONV_ROWS, pl.ds(lo, LANES)] = acc
        return carry

    lax.fori_loop(0, D // LANES, lane_block, 0)
    v = conv_ref[...]
    mu = jnp.mean(v, axis=-1, keepdims=True)
    vc = v - mu
    var = jnp.mean(vc * vc, axis=-1, keepdims=True)
    y = vc * lax.rsqrt(var + EPS) * lng_ref[...] + lnb_ref[...]
    y = (y * _sigmoid(y)).astype(BF16)
    out = jnp.dot(y, w_ref[...], preferred_element_type=F32) + b_ref[...]
    o_ref[...] = x_ref[...] + mod_ref[2:3, :] * out


def _conv(u, x, mods, dww, dwb, lng, lnb, w, b):
    per = TM // HALO
    return pl.pallas_call(
        _conv_kernel,
        grid=(NT,),
        in_specs=[pl.BlockSpec((HALO, D), lambda i: (jnp.maximum(i * per - 1, 0), 0)),
                  _row_spec(D),
                  pl.BlockSpec((HALO, D), lambda i: (jnp.minimum((i + 1) * per, T // HALO - 1), 0)),
                  _row_spec(D), _mod_spec(),
                  _full_spec((CONV_WIDTH + 1, D)), _full_spec((1, D)), _full_spec((1, D)),
                  _full_spec((1, D)), _full_spec((D, D)), _full_spec((1, D))],
        out_specs=_row_spec(D),
        out_shape=jax.ShapeDtypeStruct((T, D), F32),
        scratch_shapes=[pltpu.VMEM((TM + 2 * HALO, D), F32), pltpu.VMEM((TM, D), F32)],
        compiler_params=_params("arbitrary"),
        name="conv_dw_ln_pw2",
    )(u, u, u, x, mods, dww, dwb, lng, lnb, w, b)


def _log_sigmoid(z):
    return jnp.minimum(z, 0.0) - jnp.log(1.0 + jnp.exp(-jnp.abs(z)))


def _gla_proj_kernel(x_ref, g_ref, mod_ref, w_ref, gk_ref, gkb_ref,
                     q_ref, k_ref, v_ref, gate_ref, lf_ref, lb_ref):
    h = _normmod(x_ref[...], g_ref[...], mod_ref[0:1, :], mod_ref[1:2, :]).astype(BF16)
    kd = KEY_DIM
    q_ref[...] = jnp.dot(h, w_ref[:, 0:kd], preferred_element_type=F32) * (HEAD_K ** -0.5)
    k_ref[...] = jnp.dot(h, w_ref[:, kd:2 * kd], preferred_element_type=F32)
    v_ref[...] = jnp.dot(h, w_ref[:, 2 * kd:2 * kd + D], preferred_element_type=F32).astype(BF16)
    gate_ref[...] = jnp.dot(h, w_ref[:, 2 * kd + D:2 * kd + 2 * D], preferred_element_type=F32)
    codes = jnp.dot(h, w_ref[:, 2 * kd + 2 * D:], preferred_element_type=F32)
    z = jnp.dot(codes, gk_ref[...], precision=HIGHEST, preferred_element_type=F32) + gkb_ref[...]
    ls = _log_sigmoid(z) * (1.0 / GATE_NORM)
    lf_ref[...] = ls[:, :kd]
    lb_ref[...] = ls[:, kd:]


def _gla_proj(x, g, mods, w, gk, gkb):
    shapes = [jax.ShapeDtypeStruct((T, KEY_DIM), F32), jax.ShapeDtypeStruct((T, KEY_DIM), F32),
              jax.ShapeDtypeStruct((T, D), BF16), jax.ShapeDtypeStruct((T, D), F32),
              jax.ShapeDtypeStruct((T, KEY_DIM), F32), jax.ShapeDtypeStruct((T, KEY_DIM), F32)]
    return pl.pallas_call(
        _gla_proj_kernel,
        grid=(NT,),
        in_specs=[_row_spec(D), _full_spec((1, D)), _mod_spec(), _full_spec((D, W_IN_PAD)),
                  _full_spec((LANES, 2 * KEY_DIM)), _full_spec((1, 2 * KEY_DIM))],
        out_specs=[_row_spec(KEY_DIM), _row_spec(KEY_DIM), _row_spec(D), _row_spec(D),
                   _row_spec(KEY_DIM), _row_spec(KEY_DIM)],
        out_shape=shapes,
        compiler_params=_params("arbitrary"),
        name="gla_proj",
    )(x, g, mods, w, gk, gkb)


def _gla_scan_kernel(*refs, reverse):
    if reverse:
        q_ref, k_ref, v_ref, lg_ref, of_ref, gate_ref, hg_ref, o_ref, st_ref, cum_ref = refs
    else:
        q_ref, k_ref, v_ref, lg_ref, o_ref, st_ref, cum_ref = refs
    j = pl.program_id(1)

    @pl.when(j == 0)
    def _():
        st_ref[...] = jnp.zeros_like(st_ref)

    r = lax.broadcasted_iota(jnp.int32, (TM, TM), 0)
    c = lax.broadcasted_iota(jnp.int32, (TM, TM), 1)
    shift = CHUNK.bit_length() - 1
    same = (r >> shift) == (c >> shift)
    tri = jnp.where(same & ((c >= r) if reverse else (c <= r)), 1.0, 0.0).astype(BF16)
    lg = lg_ref[...]
    hi = lg.astype(BF16)
    r1 = lg - hi.astype(F32)
    mid = r1.astype(BF16)
    lo = (r1 - mid.astype(F32)).astype(BF16)
    cum_ref[...] = (jnp.dot(tri, hi, preferred_element_type=F32)
                    + jnp.dot(tri, mid, preferred_element_type=F32)
                    + jnp.dot(tri, lo, preferred_element_type=F32))

    rr = lax.broadcasted_iota(jnp.int32, (CHUNK, CHUNK), 0)
    cc = lax.broadcasted_iota(jnp.int32, (CHUNK, CHUNK), 1)
    mask = (cc >= rr) if reverse else (cc <= rr)
    i_last = 0 if reverse else CHUNK - 1
    i_ref = CHUNK // 2 - 1 if reverse else CHUNK // 2
    chunks = range(TM // CHUNK)
    for ci in (reversed(chunks) if reverse else chunks):
        r0 = ci * CHUNK
        for h in range(HEADS):
            k0, v0 = h * HEAD_K, h * HEAD_V
            cm = cum_ref[r0:r0 + CHUNK, k0:k0 + HEAD_K]
            last = cum_ref[r0 + i_last:r0 + i_last + 1, k0:k0 + HEAD_K]
            ref = cum_ref[r0 + i_ref:r0 + i_ref + 1, k0:k0 + HEAD_K]
            qc = q_ref[r0:r0 + CHUNK, k0:k0 + HEAD_K]
            kc = k_ref[r0:r0 + CHUNK, k0:k0 + HEAD_K]
            vc = v_ref[r0:r0 + CHUNK, v0:v0 + HEAD_V]
            qr = (qc * jnp.exp(cm - ref)).astype(BF16)
            kr = (kc * jnp.exp(ref - cm)).astype(BF16)
            att = lax.dot_general(qr, kr, _NT_DIMS, preferred_element_type=F32)
            att = jnp.where(mask, att, 0.0).astype(BF16)
            qd = (qc * jnp.exp(cm)).astype(BF16)
            st = st_ref[h]
            o = (jnp.dot(att, vc, preferred_element_type=F32)
                 + lax.dot_general(qd, st.astype(BF16), _NT_DIMS, preferred_element_type=F32))
            kdec = (kc * jnp.exp(last - cm)).astype(BF16)
            st_ref[h] = st * jnp.exp(last) + lax.dot_general(vc, kdec, _TN_DIMS,
                                                             preferred_element_type=F32)
            if reverse:
                o = o + of_ref[r0:r0 + CHUNK, v0:v0 + HEAD_V]
                on = o * lax.rsqrt(jnp.mean(o * o, axis=-1, keepdims=True) + EPS)
                on = on * hg_ref[:, v0:v0 + HEAD_V]
                gt = gate_ref[r0:r0 + CHUNK, v0:v0 + HEAD_V]
                o_ref[r0:r0 + CHUNK, v0:v0 + HEAD_V] = (on * (gt * _sigmoid(gt))).astype(BF16)
            else:
                o_ref[r0:r0 + CHUNK, v0:v0 + HEAD_V] = o


def _gla_scan(q, k, v, lg, extra, reverse):
    if reverse:
        def tile(b, j):
            return jnp.where(j == 0, b, CTX_TILES + LAT_TILES * (b + 1) - j)
    else:
        def tile(b, j):
            return jnp.where(j == 0, b, CTX_TILES + LAT_TILES * b + j - 1)

    def rows(width):
        return pl.BlockSpec((TM, width), lambda b, j: (tile(b, j), 0))

    in_specs = [rows(KEY_DIM), rows(KEY_DIM), rows(D), rows(KEY_DIM)]
    args = [q, k, v, lg]
    if reverse:
        in_specs += [rows(D), rows(D), pl.BlockSpec((1, D), lambda b, j: (0, 0))]
        args += list(extra)
    return pl.pallas_call(
        functools.partial(_gla_scan_kernel, reverse=reverse),
        grid=(BATCH, 1 + LAT_TILES),
        in_specs=in_specs,
        out_specs=rows(D),
        out_shape=jax.ShapeDtypeStruct((T, D), BF16 if reverse else F32),
        scratch_shapes=[pltpu.VMEM((HEADS, HEAD_V, HEAD_K), F32), pltpu.VMEM((TM, KEY_DIM), F32)],
        compiler_params=_params("arbitrary", "arbitrary"),
        name="gla_scan_bwd" if reverse else "gla_scan_fwd",
    )(*args)


def _matmul_res_kernel(a_ref, w_ref, x_ref, mod_ref, o_ref):
    y = jnp.dot(a_ref[...], w_ref[...], preferred_element_type=F32)
    o_ref[...] = x_ref[...] + mod_ref[2:3, :] * y


def _matmul_res(a, w, x, mods):
    return pl.pallas_call(
        _matmul_res_kernel,
        grid=(NT,),
        in_specs=[_row_spec(D), _full_spec((D, D)), _row_spec(D), _mod_spec()],
        out_specs=_row_spec(D),
        out_shape=jax.ShapeDtypeStruct((T, D), F32),
        compiler_params=_params("arbitrary"),
        name="gla_out_proj",
    )(a, w, x, mods)


def _router_kernel(x_ref, g_ref, mod_ref, w_ref, b_ref, h_ref, slab_ref, counts_ref, cnt_ref):
    h = _normmod(x_ref[...], g_ref[...], mod_ref[3:4, :], mod_ref[4:5, :])
    for s in range(SUB):
        h_ref[pl.ds(s, TM, stride=SUB), :] = h[:, s * LANES:(s + 1) * LANES]
    lg = jnp.dot(h, w_ref[...], precision=HIGHEST, preferred_element_type=F32) + b_ref[...]
    lane = lax.broadcasted_iota(jnp.int32, (TM, LANES), 1).astype(F32)
    neg = -jnp.inf
    far = float(LANES)
    is_group = (lane >= EXPERTS) & (lane < EXPERTS + GROUPS)
    glog = jnp.where(is_group, lg, neg)
    gmax = jnp.max(glog, axis=-1, keepdims=True)
    gidx = jnp.min(jnp.where(glog == gmax, lane, far), axis=-1, keepdims=True) - EXPERTS
    g_w = 1.0 / jnp.sum(jnp.exp(glog - gmax), axis=-1, keepdims=True)
    lo = gidx * PER_GROUP
    vals = jnp.where((lane >= lo) & (lane < lo + PER_GROUP), lg, neg)
    m1 = jnp.max(vals, axis=-1, keepdims=True)
    i1 = jnp.min(jnp.where(vals == m1, lane, far), axis=-1, keepdims=True)
    vals2 = jnp.where(lane == i1, neg, vals)
    m2 = jnp.max(vals2, axis=-1, keepdims=True)
    i2 = jnp.min(jnp.where(vals2 == m2, lane, far), axis=-1, keepdims=True)
    e = jnp.exp(m2 - m1)
    w1 = g_w / (1.0 + e)
    w2 = g_w * e / (1.0 + e)
    first_lo = i1 < i2
    a = jnp.minimum(i1, i2) - lo
    b = jnp.maximum(i1, i2) - lo
    cls = gidx * PAIRS + a * (PER_GROUP - 1) - a * (a - 1.0) * 0.5 + (b - a - 1.0)
    w_lo = jnp.where(first_lo, w1, w2)
    w_hi = jnp.where(first_lo, w2, w1)
    @pl.when(pl.program_id(0) == 0)
    def _():
        cnt_ref[...] = jnp.zeros_like(cnt_ref)

    onehot = jnp.where(lane == cls, 1.0, 0.0)
    r = lax.broadcasted_iota(jnp.int32, (TM, TM), 0)
    c = lax.broadcasted_iota(jnp.int32, (TM, TM), 1)
    before = jnp.where(c < r, 1.0, 0.0).astype(BF16)
    prefix = jnp.dot(before, onehot.astype(BF16), preferred_element_type=F32)
    base = cnt_ref[...]
    rank = jnp.sum(onehot * (prefix + base), axis=-1, keepdims=True)
    total = base + jnp.sum(onehot, axis=0, keepdims=True)
    cnt_ref[...] = total
    counts_ref[...] = total
    slab_ref[...] = jnp.where(lane == 0, cls, jnp.where(lane == 1, rank, jnp.where(
        lane == 2, w_lo, jnp.where(lane == 3, w_hi, 0.0))))


def _router(x, g, mods, w, b):
    return pl.pallas_call(
        _router_kernel,
        grid=(NT,),
        in_specs=[_row_spec(D), _full_spec((1, D)), _mod_spec(), _full_spec((D, LANES)),
                  _full_spec((1, LANES))],
        out_specs=[pl.BlockSpec((TM * SUB, LANES), lambda i: (i, 0)), _row_spec(LANES),
                   _full_spec((1, LANES))],
        out_shape=[jax.ShapeDtypeStruct((T * SUB, LANES), F32), jax.ShapeDtypeStruct((T, LANES), F32),
                   jax.ShapeDtypeStruct((1, LANES), F32)],
        scratch_shapes=[pltpu.VMEM((1, LANES), F32)],
        compiler_params=_params("arbitrary"),
        name="moe_router",
    )(x, g, mods, w, b)


def _ffn_kernel(lo_ref, hi_ref, nv_ref, tok_ref, hs_ref, w1l_ref, w3l_ref, w2l_ref,
                w1h_ref, w3h_ref, w2h_ref, ys_ref, xg, yb, gsem, ssem):
    i = pl.program_id(0)
    slot = i % 2

    def row_copy_in(blk, sl, r):
        tok = tok_ref[blk * BM + r]
        return pltpu.make_async_copy(hs_ref.at[pl.ds(pl.multiple_of(tok * SUB, SUB), SUB)],
                                     xg.at[sl, pl.ds(pl.multiple_of(r * SUB, SUB), SUB)], gsem.at[sl])

    def row_copy_out(blk, sl, r):
        tok = tok_ref[blk * BM + r]
        return pltpu.make_async_copy(yb.at[sl, pl.ds(pl.multiple_of(r * YROWS, YROWS), YROWS)],
                                     ys_ref.at[pl.ds(pl.multiple_of(tok * YROWS, YROWS), YROWS)],
                                     ssem.at[sl])

    def start_rows(copy, blk, sl):
        def body(r, carry):
            copy(blk, sl, r).start()
            return carry
        lax.fori_loop(0, nv_ref[blk], body, 0)

    def wait_gather(blk, sl):
        n = nv_ref[blk]

        @pl.when(n > 0)
        def _():
            pltpu.make_async_copy(hs_ref.at[pl.ds(0, n * SUB)], xg.at[sl, pl.ds(0, n * SUB)],
                                  gsem.at[sl]).wait()

    def wait_scatter(blk, sl):
        n = nv_ref[blk]

        @pl.when(n > 0)
        def _():
            pltpu.make_async_copy(yb.at[sl, pl.ds(0, n * YROWS)], ys_ref.at[pl.ds(0, n * YROWS)],
                                  ssem.at[sl]).wait()

    @pl.when(i == 0)
    def _():
        xg[...] = jnp.zeros_like(xg)
        start_rows(row_copy_in, 0, 0)

    @pl.when(i + 1 < NB)
    def _():
        start_rows(row_copy_in, i + 1, 1 - slot)

    wait_gather(i, slot)

    @pl.when(i >= 2)
    def _():
        wait_scatter(i - 2, slot)

    def expert(xb, w1_ref, w3_ref, w2_ref):
        a = jnp.dot(xb, w1_ref[...], preferred_element_type=F32)
        b = jnp.dot(xb, w3_ref[...], preferred_element_type=F32)
        hh = (a * _sigmoid(a) * b).astype(BF16)
        return jnp.dot(hh, w2_ref[...], preferred_element_type=F32)

    @pl.when(nv_ref[i] > 0)
    def _():
        xb = jnp.concatenate([xg[slot, pl.ds(s, BM, stride=SUB), :] for s in range(SUB)],
                             axis=-1).astype(BF16)
        y_lo = expert(xb, w1l_ref, w3l_ref, w2l_ref)
        for s in range(SUB):
            yb[slot, pl.ds(s, BM, stride=YROWS), :] = y_lo[:, s * LANES:(s + 1) * LANES]
        y_hi = expert(xb, w1h_ref, w3h_ref, w2h_ref)
        for s in range(SUB):
            yb[slot, pl.ds(SUB + s, BM, stride=YROWS), :] = y_hi[:, s * LANES:(s + 1) * LANES]
        start_rows(row_copy_out, i, slot)

    @pl.when(i == NB - 1)
    def _():
        wait_scatter(NB - 2, 1 - slot)
        wait_scatter(NB - 1, slot)


def _ffn(blk_lo, blk_hi, n_valid, tok_buf, hs, w1, w3, w2, layer):
    def w_spec(shape, which):
        return pl.BlockSpec((None, None) + shape,
                            lambda i, lo, hi, nv, tok: (layer, (lo, hi)[which][i], 0, 0))

    return pl.pallas_call(
        _ffn_kernel,
        grid_spec=pltpu.PrefetchScalarGridSpec(
            num_scalar_prefetch=4,
            grid=(NB,),
            in_specs=[pl.BlockSpec(memory_space=pl.ANY),
                      w_spec((D, FF), 0), w_spec((D, FF), 0), w_spec((FF, D), 0),
                      w_spec((D, FF), 1), w_spec((D, FF), 1), w_spec((FF, D), 1)],
            out_specs=pl.BlockSpec(memory_space=pl.ANY),
            scratch_shapes=[pltpu.VMEM((2, BM * SUB, LANES), F32), pltpu.VMEM((2, BM * YROWS, LANES), F32),
                            pltpu.SemaphoreType.DMA((2,)), pltpu.SemaphoreType.DMA((2,))]),
        out_shape=jax.ShapeDtypeStruct((T * YROWS, LANES), F32),
        compiler_params=_params("arbitrary"),
        name="moe_expert_ffn",
    )(blk_lo, blk_hi, n_valid, tok_buf, hs, w1, w3, w2, w1, w3, w2)


def _combine_kernel(x_ref, y_ref, slab_ref, mod_ref, fg_ref, o_ref, *, final):
    w_lo = slab_ref[:, 2:3]
    w_hi = slab_ref[:, 3:4]
    y = jnp.concatenate([w_lo * y_ref[pl.ds(s, TM, stride=YROWS), :]
                         + w_hi * y_ref[pl.ds(SUB + s, TM, stride=YROWS), :] for s in range(SUB)], axis=-1)
    x = x_ref[...] + mod_ref[5:6, :] * y
    if final:
        x = x * lax.rsqrt(jnp.mean(x * x, axis=-1, keepdims=True) + EPS) * fg_ref[...]
    o_ref[...] = x


def _combine(x, ys, slab, mods, final_g, final):
    first = CTX_TILES if final else 0

    def rows(height, width):
        return pl.BlockSpec((height, width), lambda i: (i + first, 0))

    return pl.pallas_call(
        functools.partial(_combine_kernel, final=final),
        grid=(NT - first,),
        in_specs=[rows(TM, D), rows(TM * YROWS, LANES), rows(TM, LANES),
                  pl.BlockSpec((None, 6, D), lambda i: (_seg(i + first), 0, 0)), _full_spec((1, D))],
        out_specs=_row_spec(D),
        out_shape=jax.ShapeDtypeStruct((T - first * TM, D), F32),
        compiler_params=_params("arbitrary"),
        name="moe_combine",
    )(x, ys, slab, mods, final_g)


def _class_experts():
    lo, hi = [], []
    for g in range(GROUPS):
        for a in range(PER_GROUP):
            for b in range(a + 1, PER_GROUP):
                lo.append(g * PER_GROUP + a)
                hi.append(g * PER_GROUP + b)
    return jnp.array(lo, jnp.int32), jnp.array(hi, jnp.int32)


def _dispatch_plan(slab, counts):
    cls = slab[:, 0].astype(jnp.int32)
    rank = slab[:, 1].astype(jnp.int32)
    counts = counts[0, :NCLS].astype(jnp.int32)
    padded = ((counts + BM - 1) // BM) * BM
    pend = jnp.cumsum(padded)
    pstart = pend - padded
    dest = pstart[cls] + rank
    tok_buf = jnp.zeros((P,), jnp.int32).at[dest].set(jnp.arange(T, dtype=jnp.int32))
    blk0 = jnp.arange(NB, dtype=jnp.int32) * BM
    blk_c = jnp.clip(jnp.searchsorted(pend, blk0, side='right'), 0, NCLS - 1).astype(jnp.int32)
    n_valid = jnp.clip(counts[blk_c] - (blk0 - pstart[blk_c]), 0, BM).astype(jnp.int32)
    cls_lo, cls_hi = _class_experts()
    return tok_buf, cls_lo[blk_c], cls_hi[blk_c], n_valid


def _moe(x, g, mods, wr_cat, br_cat, w1, w3, w2, layer, final_g, final):
    hs, slab, counts = _router(x, g, mods, wr_cat, br_cat)
    tok_buf, blk_lo, blk_hi, n_valid = _dispatch_plan(slab, counts)
    ys = _ffn(blk_lo, blk_hi, n_valid, tok_buf, hs, w1, w3, w2, layer)
    return _combine(x, ys, slab, mods, final_g, final)


def _cast_kernel(w_ref, o_ref):
    o_ref[...] = w_ref[...].astype(BF16)


def _to_bf16(w, rows):
    n, cols = w.shape
    return pl.pallas_call(
        _cast_kernel,
        grid=(n // rows,),
        in_specs=[pl.BlockSpec((rows, cols), lambda i: (i, 0))],
        out_specs=pl.BlockSpec((rows, cols), lambda i: (i, 0)),
        out_shape=jax.ShapeDtypeStruct((n, cols), BF16),
        compiler_params=_params("arbitrary"),
        name="weights_to_bf16",
    )(w)


def _sincos_2d(rows, width, dim):
    quarter = dim // 4
    omega = 1.0 / (POS_BASE ** (jnp.arange(quarter, dtype=F32) / quarter))
    ar = jnp.arange(rows, dtype=F32)[:, None] * omega
    ac = jnp.arange(width, dtype=F32)[:, None] * omega
    row_part = jnp.concatenate([jnp.sin(ar), jnp.cos(ar)], axis=-1)
    col_part = jnp.concatenate([jnp.sin(ac), jnp.cos(ac)], axis=-1)
    row_part = jnp.broadcast_to(row_part[:, None, :], (rows, width, dim // 2))
    col_part = jnp.broadcast_to(col_part[None, :, :], (rows, width, dim // 2))
    return jnp.concatenate([row_part, col_part], axis=-1).reshape(rows * width, dim)


def kernel(x, c, ctx, c_ctx, ada_w, ada_b, norm_g, final_g, conv_pw1_w, conv_pw1_b, conv_dw_w, conv_dw_b, conv_ln_g, conv_ln_b, conv_pw2_w, conv_pw2_b, gla_w_in, gla_gk_up, gla_gk_b, gla_head_g, gla_w_o, moe_wg, moe_bg, moe_wr, moe_br, moe_w1, moe_w3, moe_w2):
    assert x.shape == (BATCH, SEQ, D) and ctx.shape == (BATCH, CTX_LEN, D)
    pos = _sincos_2d(SEQ // GRID_W, GRID_W, D)
    xt = jnp.concatenate([ctx.reshape(T_CTX, D), (x + pos[None]).reshape(T_LAT, D)], axis=0)

    c8 = jnp.concatenate([c_ctx[None], c, jnp.zeros((8 - 1 - BATCH, D), F32)], axis=0)
    mods = _ada_mods(c8, ada_w, ada_b)
    fg = final_g.reshape(1, D)
    w1b = _to_bf16(moe_w1.reshape(-1, FF), 4 * D).reshape(moe_w1.shape)
    w3b = _to_bf16(moe_w3.reshape(-1, FF), 4 * D).reshape(moe_w3.shape)
    w2b = _to_bf16(moe_w2.reshape(-1, D), 4 * FF).reshape(moe_w2.shape)

    for i in range(DEPTH):
        j = i // 2
        g_mix = norm_g[i, 0].reshape(1, D)
        if i % 2 == 0:
            u = _pw1(xt, g_mix, mods[i], conv_pw1_w[j].astype(BF16), conv_pw1_b[j].reshape(1, 2 * D))
            dww = jnp.concatenate([conv_dw_w[j], jnp.zeros((1, D), F32)], axis=0)
            xt = _conv(u, xt, mods[i], dww, conv_dw_b[j].reshape(1, D), conv_ln_g[j].reshape(1, D),
                       conv_ln_b[j].reshape(1, D), conv_pw2_w[j].astype(BF16),
                       conv_pw2_b[j].reshape(1, D))
        else:
            w_in = jnp.pad(gla_w_in[j], ((0, 0), (0, W_IN_PAD - gla_w_in.shape[-1]))).astype(BF16)
            gk = jnp.zeros((LANES, 2 * KEY_DIM), F32)
            gk = gk.at[0:GATE_RANK, 0:KEY_DIM].set(gla_gk_up[j, 0])
            gk = gk.at[GATE_RANK:2 * GATE_RANK, KEY_DIM:].set(gla_gk_up[j, 1])
            gkb = gla_gk_b[j].reshape(1, 2 * KEY_DIM)
            q, k, v, gate, lf, lb = _gla_proj(xt, g_mix, mods[i], w_in, gk, gkb)
            o_f = _gla_scan(q, k, v, lf, None, reverse=False)
            og = _gla_scan(q, k, v, lb, (o_f, gate, gla_head_g[j].reshape(1, D)), reverse=True)
            xt = _matmul_res(og, gla_w_o[j].astype(BF16), xt, mods[i])
        wr_cat = jnp.concatenate([moe_wr[i], moe_wg[i],
                                  jnp.zeros((D, LANES - EXPERTS - GROUPS), F32)], axis=1)
        br_cat = jnp.concatenate([moe_br[i], moe_bg[i],
                                  jnp.zeros((LANES - EXPERTS - GROUPS,), F32)]).reshape(1, LANES)
        xt = _moe(xt, norm_g[i, 1].reshape(1, D), mods[i], wr_cat, br_cat,
                  w1b, w3b, w2b, i, fg, final=(i == DEPTH - 1))
    return xt.reshape(BATCH, SEQ, D)
```

```python
import functools

import jax
import jax.numpy as jnp
from jax import lax
from jax.experimental import pallas as pl
from jax.experimental.pallas import tpu as pltpu

F32 = jnp.float32
BF16 = jnp.bfloat16
HIGHEST = lax.Precision.HIGHEST

D = 1024
BATCH = 2
SEQ = 16384
CTX_LEN = 256
DEPTH = 4
GRID_W = 64
POS_BASE = 10000.0
EPS = 1e-6
CONV_WIDTH = 31
CONV_PAD = CONV_WIDTH // 2
HEADS = 4
KEY_DIM = D // 2
HEAD_K = KEY_DIM // HEADS
HEAD_V = D // HEADS
GATE_RANK = 16
GATE_NORM = 16.0
CHUNK = 64
GROUPS = 4
PER_GROUP = 8
EXPERTS = GROUPS * PER_GROUP
FF = D // 2

TM = 256
T_CTX = BATCH * CTX_LEN
T_LAT = BATCH * SEQ
T = T_CTX + T_LAT
NT = T // TM
CTX_TILES = T_CTX // TM
LAT_TILES = SEQ // TM
HALO = 16
CONV_ROWS = 128
CONV_WIN = CONV_ROWS + 2 * HALO
SUBLANES = 8
BM = 256
PAIRS = PER_GROUP * (PER_GROUP - 1) // 2
NCLS = GROUPS * PAIRS
NB = T // BM + NCLS
P = NB * BM
LANES = 128
SUB = D // LANES
YROWS = 2 * SUB
W_IN_PAD = 2 * KEY_DIM + 2 * D + LANES
VMEM_LIMIT = 48 * 1024 * 1024

_NT_DIMS = (((1,), (1,)), ((), ()))
_TN_DIMS = (((0,), (0,)), ((), ()))


def _params(*sem):
    return pltpu.CompilerParams(dimension_semantics=sem, vmem_limit_bytes=VMEM_LIMIT)


def _seg(i):
    return jnp.where(i < CTX_TILES, 0, jnp.where(i < CTX_TILES + LAT_TILES, 1, 2))


def _mod_spec():
    return pl.BlockSpec((None, 6, D), lambda i: (_seg(i), 0, 0))


def _row_spec(width):
    return pl.BlockSpec((TM, width), lambda i: (i, 0))


def _full_spec(shape):
    return pl.BlockSpec(shape, lambda i: (0,) * len(shape))


def _normmod(x, g, shift, scale):
    ms = jnp.mean(x * x, axis=-1, keepdims=True)
    return (x * lax.rsqrt(ms + EPS) * g) * (1.0 + scale) + shift


def _sigmoid(x):
    return 1.0 / (1.0 + jnp.exp(-x))


def _split_bf16(a):
    hi = a.astype(BF16)
    return hi, (a - hi.astype(F32)).astype(BF16)


def _dot3(a, w_hi, w_lo):
    a_hi, a_lo = _split_bf16(a)
    return (jnp.dot(a_hi, w_hi, preferred_element_type=F32)
            + jnp.dot(a_lo, w_hi, preferred_element_type=F32)
            + jnp.dot(a_hi, w_lo, preferred_element_type=F32))


def _ada_kernel(c_ref, w_ref, b_ref, o_ref):
    c = c_ref[...]
    s = c * _sigmoid(c)
    o_ref[...] = jnp.dot(s, w_ref[...], precision=HIGHEST, preferred_element_type=F32) + b_ref[...]


def _ada_mods(c8, ada_w, ada_b):
    out = pl.pallas_call(
        _ada_kernel,
        grid=(DEPTH, 6),
        in_specs=[pl.BlockSpec((8, D), lambda l, n: (0, 0)),
                  pl.BlockSpec((None, D, D), lambda l, n: (l, 0, n)),
                  pl.BlockSpec((None, 1, D), lambda l, n: (l, 0, n))],
        out_specs=pl.BlockSpec((None, 8, D), lambda l, n: (l, 0, n)),
        out_shape=jax.ShapeDtypeStruct((DEPTH, 8, 6 * D), F32),
        compiler_params=_params("arbitrary", "arbitrary"),
        name="ada_mods",
    )(c8, ada_w, ada_b.reshape(DEPTH, 1, 6 * D))
    return out[:, :3].reshape(DEPTH, 3, 6, D)


def _pw1_kernel(x_ref, g_ref, mod_ref, w_ref, b_ref, u_ref):
    h = _normmod(x_ref[...], g_ref[...], mod_ref[0:1, :], mod_ref[1:2, :]).astype(BF16)
    a = jnp.dot(h, w_ref[:, :D], preferred_element_type=F32) + b_ref[:, :D]
    g = jnp.dot(h, w_ref[:, D:], preferred_element_type=F32) + b_ref[:, D:]
    u_ref[...] = a * _sigmoid(g)


def _pw1(x, g, mods, w, b):
    return pl.pallas_call(
        _pw1_kernel,
        grid=(NT,),
        in_specs=[_row_spec(D), _full_spec((1, D)), _mod_spec(),
                  _full_spec((D, 2 * D)), _full_spec((1, 2 * D))],
        out_specs=_row_spec(D),
        out_shape=jax.ShapeDtypeStruct((T, D), F32),
        compiler_params=_params("arbitrary"),
        name="conv_pw1_glu",
    )(x, g, mods, w, b)


def _conv_kernel(prev_ref, cur_ref, next_ref, x_ref, mod_ref, dww_ref, dwb_ref, lng_ref, lnb_ref,
                 w_ref, b_ref, o_ref, ext_ref, conv_ref):
    i = pl.program_id(0)
    lat0 = CTX_TILES
    lat1 = CTX_TILES + LAT_TILES
    is_start = (i < CTX_TILES) | (i == lat0) | (i == lat1)
    is_end = (i < CTX_TILES) | (i == lat1 - 1) | (i == NT - 1)
    ext_ref[0:HALO, :] = jnp.where(is_start, 0.0, prev_ref[...])
    ext_ref[HALO:HALO + TM, :] = cur_ref[...]
    ext_ref[HALO + TM:, :] = jnp.where(is_end, 0.0, next_ref[...])

    def lane_block(j, carry):
        lo = pl.multiple_of(j * LANES, LANES)
        for rb in range(TM // CONV_ROWS):
            r0 = rb * CONV_ROWS
            win = ext_ref[r0:r0 + CONV_WIN, pl.ds(lo, LANES)]
            acc = jnp.broadcast_to(dwb_ref[:, pl.ds(lo, LANES)], (CONV_ROWS, LANES))
            for rho in range(SUBLANES):
                shifted = win if rho == 0 else pltpu.roll(win, shift=CONV_WIN - rho, axis=0)
                for k in range(CONV_WIDTH):
                    off = k + HALO - CONV_PAD - rho
                    if off % SUBLANES == 0:
                        acc = acc + dww_ref[k:k + 1, pl.ds(lo, LANES)] * shifted[off:off + CONV_ROWS]
            conv_ref[r0:r0 + CONV_ROWS, pl.ds(lo, LANES)] = acc
        return carry

    lax.fori_loop(0, D // LANES, lane_block, 0)
    v = conv_ref[...]
    mu = jnp.mean(v, axis=-1, keepdims=True)
    vc = v - mu
    var = jnp.mean(vc * vc, axis=-1, keepdims=True)
    y = vc * lax.rsqrt(var + EPS) * lng_ref[...] + lnb_ref[...]
    y = (y * _sigmoid(y)).astype(BF16)
    out = jnp.dot(y, w_ref[...], preferred_element_type=F32) + b_ref[...]
    o_ref[...] = x_ref[...] + mod_ref[2:3, :] * out


def _conv(u, x, mods, dww, dwb, lng, lnb, w, b):
    per = TM // HALO
    return pl.pallas_call(
        _conv_kernel,
        grid=(NT,),
        in_specs=[pl.BlockSpec((HALO, D), lambda i: (jnp.maximum(i * per - 1, 0), 0)),
                  _row_spec(D),
                  pl.BlockSpec((HALO, D), lambda i: (jnp.minimum((i + 1) * per, T // HALO - 1), 0)),
                  _row_spec(D), _mod_spec(),
                  _full_spec((CONV_WIDTH + 1, D)), _full_spec((1, D)), _full_spec((1, D)),
                  _full_spec((1, D)), _full_spec((D, D)), _full_spec((1, D))],
        out_specs=_row_spec(D),
        out_shape=jax.ShapeDtypeStruct((T, D), F32),
        scratch_shapes=[pltpu.VMEM((TM + 2 * HALO, D), F32), pltpu.VMEM((TM, D), F32)],
        compiler_params=_params("arbitrary"),
        name="conv_dw_ln_pw2",
    )(u, u, u, x, mods, dww, dwb, lng, lnb, w, b)


def _log_sigmoid(z):
    return jnp.minimum(z, 0.0) - jnp.log(1.0 + jnp.exp(-jnp.abs(z)))


def _gla_proj_kernel(x_ref, g_ref, mod_ref, w_ref, gkh_ref, gkl_ref, gkb_ref,
                     q_ref, k_ref, v_ref, gate_ref, lf_ref, lb_ref):
    h = _normmod(x_ref[...], g_ref[...], mod_ref[0:1, :], mod_ref[1:2, :]).astype(BF16)
    kd = KEY_DIM
    q_ref[...] = jnp.dot(h, w_ref[:, 0:kd], preferred_element_type=F32) * (HEAD_K ** -0.5)
    k_ref[...] = jnp.dot(h, w_ref[:, kd:2 * kd], preferred_element_type=F32)
    v_ref[...] = jnp.dot(h, w_ref[:, 2 * kd:2 * kd + D], preferred_element_type=F32).astype(BF16)
    gate_ref[...] = jnp.dot(h, w_ref[:, 2 * kd + D:2 * kd + 2 * D], preferred_element_type=F32)
    codes = jnp.dot(h, w_ref[:, 2 * kd + 2 * D:], preferred_element_type=F32)
    z = _dot3(codes, gkh_ref[...], gkl_ref[...]) + gkb_ref[...]
    ls = _log_sigmoid(z) * (1.0 / GATE_NORM)
    lf_ref[...] = ls[:, :kd]
    lb_ref[...] = ls[:, kd:]


def _gla_proj(x, g, mods, w, gk, gkb):
    gk_hi, gk_lo = _split_bf16(gk)
    shapes = [jax.ShapeDtypeStruct((T, KEY_DIM), F32), jax.ShapeDtypeStruct((T, KEY_DIM), F32),
              jax.ShapeDtypeStruct((T, D), BF16), jax.ShapeDtypeStruct((T, D), F32),
              jax.ShapeDtypeStruct((T, KEY_DIM), F32), jax.ShapeDtypeStruct((T, KEY_DIM), F32)]
    return pl.pallas_call(
        _gla_proj_kernel,
        grid=(NT,),
        in_specs=[_row_spec(D), _full_spec((1, D)), _mod_spec(), _full_spec((D, W_IN_PAD)),
                  _full_spec((LANES, 2 * KEY_DIM)), _full_spec((LANES, 2 * KEY_DIM)),
                  _full_spec((1, 2 * KEY_DIM))],
        out_specs=[_row_spec(KEY_DIM), _row_spec(KEY_DIM), _row_spec(D), _row_spec(D),
                   _row_spec(KEY_DIM), _row_spec(KEY_DIM)],
        out_shape=shapes,
        compiler_params=_params("arbitrary"),
        name="gla_proj",
    )(x, g, mods, w, gk_hi, gk_lo, gkb)


def _gla_scan_kernel(*refs, reverse):
    if reverse:
        q_ref, k_ref, v_ref, lg_ref, of_ref, gate_ref, hg_ref, o_ref, st_ref, cum_ref = refs
    else:
        q_ref, k_ref, v_ref, lg_ref, o_ref, st_ref, cum_ref = refs
    j = pl.program_id(1)

    @pl.when(j == 0)
    def _():
        st_ref[...] = jnp.zeros_like(st_ref)

    r = lax.broadcasted_iota(jnp.int32, (TM, TM), 0)
    c = lax.broadcasted_iota(jnp.int32, (TM, TM), 1)
    shift = CHUNK.bit_length() - 1
    same = (r >> shift) == (c >> shift)
    tri = jnp.where(same & ((c >= r) if reverse else (c <= r)), 1.0, 0.0).astype(BF16)
    lg = lg_ref[...]
    hi = lg.astype(BF16)
    r1 = lg - hi.astype(F32)
    mid = r1.astype(BF16)
    lo = (r1 - mid.astype(F32)).astype(BF16)
    cum_ref[...] = (jnp.dot(tri, hi, preferred_element_type=F32)
                    + jnp.dot(tri, mid, preferred_element_type=F32)
                    + jnp.dot(tri, lo, preferred_element_type=F32))

    rr = lax.broadcasted_iota(jnp.int32, (CHUNK, CHUNK), 0)
    cc = lax.broadcasted_iota(jnp.int32, (CHUNK, CHUNK), 1)
    mask = (cc >= rr) if reverse else (cc <= rr)
    i_last = 0 if reverse else CHUNK - 1
    i_ref = CHUNK // 2 - 1 if reverse else CHUNK // 2
    chunks = range(TM // CHUNK)
    for ci in (reversed(chunks) if reverse else chunks):
        r0 = ci * CHUNK
        for h in range(HEADS):
            k0, v0 = h * HEAD_K, h * HEAD_V
            cm = cum_ref[r0:r0 + CHUNK, k0:k0 + HEAD_K]
            last = cum_ref[r0 + i_last:r0 + i_last + 1, k0:k0 + HEAD_K]
            ref = cum_ref[r0 + i_ref:r0 + i_ref + 1, k0:k0 + HEAD_K]
            qc = q_ref[r0:r0 + CHUNK, k0:k0 + HEAD_K]
            kc = k_ref[r0:r0 + CHUNK, k0:k0 + HEAD_K]
            vc = v_ref[r0:r0 + CHUNK, v0:v0 + HEAD_V]
            qr = (qc * jnp.exp(cm - ref)).astype(BF16)
            kr = (kc * jnp.exp(ref - cm)).astype(BF16)
            att = lax.dot_general(qr, kr, _NT_DIMS, preferred_element_type=F32)
            att = jnp.where(mask, att, 0.0).astype(BF16)
            qd = (qc * jnp.exp(cm)).astype(BF16)
            st = st_ref[h]
            o = (jnp.dot(att, vc, preferred_element_type=F32)
                 + lax.dot_general(qd, st.astype(BF16), _NT_DIMS, preferred_element_type=F32))
            kdec = (kc * jnp.exp(last - cm)).astype(BF16)
            st_ref[h] = st * jnp.exp(last) + lax.dot_general(vc, kdec, _TN_DIMS,
                                                             preferred_element_type=F32)
            if reverse:
                o = o + of_ref[r0:r0 + CHUNK, v0:v0 + HEAD_V]
                on = o * lax.rsqrt(jnp.mean(o * o, axis=-1, keepdims=True) + EPS)
                on = on * hg_ref[:, v0:v0 + HEAD_V]
                gt = gate_ref[r0:r0 + CHUNK, v0:v0 + HEAD_V]
                o_ref[r0:r0 + CHUNK, v0:v0 + HEAD_V] = (on * (gt * _sigmoid(gt))).astype(BF16)
            else:
                o_ref[r0:r0 + CHUNK, v0:v0 + HEAD_V] = o


def _gla_scan(q, k, v, lg, extra, reverse):
    if reverse:
        def tile(b, j):
            return jnp.where(j == 0, b, CTX_TILES + LAT_TILES * (b + 1) - j)
    else:
        def tile(b, j):
            return jnp.where(j == 0, b, CTX_TILES + LAT_TILES * b + j - 1)

    def rows(width):
        return pl.BlockSpec((TM, width), lambda b, j: (tile(b, j), 0))

    in_specs = [rows(KEY_DIM), rows(KEY_DIM), rows(D), rows(KEY_DIM)]
    args = [q, k, v, lg]
    if reverse:
        in_specs += [rows(D), rows(D), pl.BlockSpec((1, D), lambda b, j: (0, 0))]
        args += list(extra)
    return pl.pallas_call(
        functools.partial(_gla_scan_kernel, reverse=reverse),
        grid=(BATCH, 1 + LAT_TILES),
        in_specs=in_specs,
        out_specs=rows(D),
        out_shape=jax.ShapeDtypeStruct((T, D), BF16 if reverse else F32),
        scratch_shapes=[pltpu.VMEM((HEADS, HEAD_V, HEAD_K), F32), pltpu.VMEM((TM, KEY_DIM), F32)],
        compiler_params=_params("arbitrary", "arbitrary"),
        name="gla_scan_bwd" if reverse else "gla_scan_fwd",
    )(*args)


def _matmul_res_kernel(a_ref, w_ref, x_ref, mod_ref, o_ref):
    y = jnp.dot(a_ref[...], w_ref[...], preferred_element_type=F32)
    o_ref[...] = x_ref[...] + mod_ref[2:3, :] * y


def _matmul_res(a, w, x, mods):
    return pl.pallas_call(
        _matmul_res_kernel,
        grid=(NT,),
        in_specs=[_row_spec(D), _full_spec((D, D)), _row_spec(D), _mod_spec()],
        out_specs=_row_spec(D),
        out_shape=jax.ShapeDtypeStruct((T, D), F32),
        compiler_params=_params("arbitrary"),
        name="gla_out_proj",
    )(a, w, x, mods)


def _router_kernel(x_ref, g_ref, mod_ref, wh_ref, wl_ref, b_ref, h_ref, slab_ref, counts_ref, cnt_ref):
    h = _normmod(x_ref[...], g_ref[...], mod_ref[3:4, :], mod_ref[4:5, :])
    for s in range(SUB):
        h_ref[pl.ds(s, TM, stride=SUB), :] = h[:, s * LANES:(s + 1) * LANES]
    lg = _dot3(h, wh_ref[...], wl_ref[...]) + b_ref[...]
    lane = lax.broadcasted_iota(jnp.int32, (TM, LANES), 1).astype(F32)
    neg = -jnp.inf
    far = float(LANES)
    is_group = (lane >= EXPERTS) & (lane < EXPERTS + GROUPS)
    glog = jnp.where(is_group, lg, neg)
    gmax = jnp.max(glog, axis=-1, keepdims=True)
    gidx = jnp.min(jnp.where(glog == gmax, lane, far), axis=-1, keepdims=True) - EXPERTS
    g_w = 1.0 / jnp.sum(jnp.exp(glog - gmax), axis=-1, keepdims=True)
    lo = gidx * PER_GROUP
    vals = jnp.where((lane >= lo) & (lane < lo + PER_GROUP), lg, neg)
    m1 = jnp.max(vals, axis=-1, keepdims=True)
    i1 = jnp.min(jnp.where(vals == m1, lane, far), axis=-1, keepdims=True)
    vals2 = jnp.where(lane == i1, neg, vals)
    m2 = jnp.max(vals2, axis=-1, keepdims=True)
    i2 = jnp.min(jnp.where(vals2 == m2, lane, far), axis=-1, keepdims=True)
    e = jnp.exp(m2 - m1)
    w1 = g_w / (1.0 + e)
    w2 = g_w * e / (1.0 + e)
    first_lo = i1 < i2
    a = jnp.minimum(i1, i2) - lo
    b = jnp.maximum(i1, i2) - lo
    cls = gidx * PAIRS + a * (PER_GROUP - 1) - a * (a - 1.0) * 0.5 + (b - a - 1.0)
    w_lo = jnp.where(first_lo, w1, w2)
    w_hi = jnp.where(first_lo, w2, w1)
    @pl.when(pl.program_id(0) == 0)
    def _():
        cnt_ref[...] = jnp.zeros_like(cnt_ref)

    onehot = jnp.where(lane == cls, 1.0, 0.0)
    r = lax.broadcasted_iota(jnp.int32, (TM, TM), 0)
    c = lax.broadcasted_iota(jnp.int32, (TM, TM), 1)
    before = jnp.where(c < r, 1.0, 0.0).astype(BF16)
    prefix = jnp.dot(before, onehot.astype(BF16), preferred_element_type=F32)
    base = cnt_ref[...]
    rank = jnp.sum(onehot * (prefix + base), axis=-1, keepdims=True)
    total = base + jnp.sum(onehot, axis=0, keepdims=True)
    cnt_ref[...] = total
    counts_ref[...] = total
    slab_ref[...] = jnp.where(lane == 0, cls, jnp.where(lane == 1, rank, jnp.where(
        lane == 2, w_lo, jnp.where(lane == 3, w_hi, 0.0))))


def _router(x, g, mods, w, b):
    w_hi, w_lo = _split_bf16(w)
    return pl.pallas_call(
        _router_kernel,
        grid=(NT,),
        in_specs=[_row_spec(D), _full_spec((1, D)), _mod_spec(), _full_spec((D, LANES)),
                  _full_spec((D, LANES)), _full_spec((1, LANES))],
        out_specs=[pl.BlockSpec((TM * SUB, LANES), lambda i: (i, 0)), _row_spec(LANES),
                   _full_spec((1, LANES))],
        out_shape=[jax.ShapeDtypeStruct((T * SUB, LANES), F32), jax.ShapeDtypeStruct((T, LANES), F32),
                   jax.ShapeDtypeStruct((1, LANES), F32)],
        scratch_shapes=[pltpu.VMEM((1, LANES), F32)],
        compiler_params=_params("arbitrary"),
        name="moe_router",
    )(x, g, mods, w_hi, w_lo, b)


def _ffn_kernel(lo_ref, hi_ref, nv_ref, tok_ref, hs_ref, w1l_ref, w3l_ref, w2l_ref,
                w1h_ref, w3h_ref, w2h_ref, ys_ref, xg, yb, gsem, ssem):
    i = pl.program_id(0)
    slot = i % 2

    def used(blk):
        return nv_ref[jnp.clip(blk, 0, NB - 1)] > 0

    def gather_copy(blk, sl, r):
        tok = tok_ref[blk * BM + r]
        return pltpu.make_async_copy(hs_ref.at[pl.ds(pl.multiple_of(tok * SUB, SUB), SUB)],
                                     xg.at[sl, pl.ds(r * SUB, SUB)], gsem.at[sl])

    def scatter_copy(blk, sl, r):
        row = jnp.where(r < nv_ref[blk], tok_ref[blk * BM + r], T + sl * BM + r)
        return pltpu.make_async_copy(yb.at[sl, pl.ds(r * YROWS, YROWS)],
                                     ys_ref.at[pl.ds(pl.multiple_of(row * YROWS, YROWS), YROWS)],
                                     ssem.at[sl])

    def start_block(copy, blk, sl, priority):
        for r in range(BM):
            copy(blk, sl, r).start(priority=priority)

    def wait_gather(sl):
        pltpu.make_async_copy(hs_ref.at[pl.ds(0, BM * SUB)], xg.at[sl], gsem.at[sl]).wait()

    def wait_scatter(sl):
        pltpu.make_async_copy(yb.at[sl], ys_ref.at[pl.ds(0, BM * YROWS)], ssem.at[sl]).wait()

    @pl.when((i == 0) & used(0))
    def _():
        start_block(gather_copy, 0, 0, 0)

    @pl.when(((i == 0) & used(0)) | ((i > 0) & used(i - 1)))
    def _():
        wait_gather(slot)

    @pl.when((i >= 2) & used(i - 2))
    def _():
        wait_scatter(slot)

    def expert(xb, w1_ref, w3_ref, w2_ref):
        a = jnp.dot(xb, w1_ref[...], preferred_element_type=F32)
        b = jnp.dot(xb, w3_ref[...], preferred_element_type=F32)
        hh = (a * _sigmoid(a) * b).astype(BF16)
        return jnp.dot(hh, w2_ref[...], preferred_element_type=F32)

    @pl.when(used(i))
    def _():
        start_block(gather_copy, jnp.minimum(i + 1, NB - 1), 1 - slot, 0)
        xb = jnp.concatenate([xg[slot, pl.ds(s, BM, stride=SUB), :] for s in range(SUB)],
                             axis=-1).astype(BF16)
        y_lo = expert(xb, w1l_ref, w3l_ref, w2l_ref)
        for s in range(SUB):
            yb[slot, pl.ds(s, BM, stride=YROWS), :] = y_lo[:, s * LANES:(s + 1) * LANES]
        y_hi = expert(xb, w1h_ref, w3h_ref, w2h_ref)
        for s in range(SUB):
            yb[slot, pl.ds(SUB + s, BM, stride=YROWS), :] = y_hi[:, s * LANES:(s + 1) * LANES]
        start_block(scatter_copy, i, slot, 1)

    @pl.when(i == NB - 1)
    def _():
        @pl.when(used(NB - 2))
        def _():
            wait_scatter(1 - slot)

        @pl.when(used(NB - 1))
        def _():
            wait_gather(1 - slot)
            wait_scatter(slot)


def _ffn(blk_lo, blk_hi, n_valid, tok_buf, hs, w1, w3, w2, layer):
    def w_spec(shape, which):
        return pl.BlockSpec((None, None) + shape,
                            lambda i, lo, hi, nv, tok: (layer, (lo, hi)[which][i], 0, 0))

    return pl.pallas_call(
        _ffn_kernel,
        grid_spec=pltpu.PrefetchScalarGridSpec(
            num_scalar_prefetch=4,
            grid=(NB,),
            in_specs=[pl.BlockSpec(memory_space=pl.ANY),
                      w_spec((D, FF), 0), w_spec((D, FF), 0), w_spec((FF, D), 0),
                      w_spec((D, FF), 1), w_spec((D, FF), 1), w_spec((FF, D), 1)],
            out_specs=pl.BlockSpec(memory_space=pl.ANY),
            scratch_shapes=[pltpu.VMEM((2, BM * SUB, LANES), F32), pltpu.VMEM((2, BM * YROWS, LANES), F32),
                            pltpu.SemaphoreType.DMA((2,)), pltpu.SemaphoreType.DMA((2,))]),
        out_shape=jax.ShapeDtypeStruct(((T + 2 * BM) * YROWS, LANES), F32),
        compiler_params=_params("arbitrary"),
        name="moe_expert_ffn",
    )(blk_lo, blk_hi, n_valid, tok_buf, hs, w1, w3, w2, w1, w3, w2)


def _combine_kernel(x_ref, y_ref, slab_ref, mod_ref, fg_ref, o_ref, *, final):
    w_lo = slab_ref[:, 2:3]
    w_hi = slab_ref[:, 3:4]
    y = jnp.concatenate([w_lo * y_ref[pl.ds(s, TM, stride=YROWS), :]
                         + w_hi * y_ref[pl.ds(SUB + s, TM, stride=YROWS), :] for s in range(SUB)], axis=-1)
    x = x_ref[...] + mod_ref[5:6, :] * y
    if final:
        x = x * lax.rsqrt(jnp.mean(x * x, axis=-1, keepdims=True) + EPS) * fg_ref[...]
    o_ref[...] = x


def _combine(x, ys, slab, mods, final_g, final):
    first = CTX_TILES if final else 0

    def rows(height, width):
        return pl.BlockSpec((height, width), lambda i: (i + first, 0))

    return pl.pallas_call(
        functools.partial(_combine_kernel, final=final),
        grid=(NT - first,),
        in_specs=[rows(TM, D), rows(TM * YROWS, LANES), rows(TM, LANES),
                  pl.BlockSpec((None, 6, D), lambda i: (_seg(i + first), 0, 0)), _full_spec((1, D))],
        out_specs=_row_spec(D),
        out_shape=jax.ShapeDtypeStruct((T - first * TM, D), F32),
        compiler_params=_params("arbitrary"),
        name="moe_combine",
    )(x, ys, slab, mods, final_g)


def _class_experts():
    lo, hi = [], []
    for g in range(GROUPS):
        for a in range(PER_GROUP):
            for b in range(a + 1, PER_GROUP):
                lo.append(g * PER_GROUP + a)
                hi.append(g * PER_GROUP + b)
    return jnp.array(lo, jnp.int32), jnp.array(hi, jnp.int32)


def _dispatch_plan(slab, counts):
    cls = slab[:, 0].astype(jnp.int32)
    rank = slab[:, 1].astype(jnp.int32)
    counts = counts[0, :NCLS].astype(jnp.int32)
    padded = ((counts + BM - 1) // BM) * BM
    pend = jnp.cumsum(padded)
    pstart = pend - padded
    ids = jnp.arange(NCLS, dtype=jnp.int32)

    def lookup(table, idx):
        return jnp.sum(jnp.where(idx[:, None] == ids[None, :], table[None, :], 0), axis=1)

    dest = lookup(pstart, cls) + rank
    tok_buf = jnp.zeros((P,), jnp.int32).at[dest].set(jnp.arange(T, dtype=jnp.int32))
    blk0 = jnp.arange(NB, dtype=jnp.int32) * BM
    blk_c = jnp.minimum(jnp.sum((pend[None, :] <= blk0[:, None]).astype(jnp.int32), axis=1), NCLS - 1)
    n_valid = jnp.clip(lookup(counts, blk_c) - (blk0 - lookup(pstart, blk_c)), 0, BM)
    cls_lo, cls_hi = _class_experts()
    return tok_buf, lookup(cls_lo, blk_c), lookup(cls_hi, blk_c), n_valid


def _moe(x, g, mods, wr_cat, br_cat, w1, w3, w2, layer, final_g, final):
    hs, slab, counts = _router(x, g, mods, wr_cat, br_cat)
    tok_buf, blk_lo, blk_hi, n_valid = _dispatch_plan(slab, counts)
    ys = _ffn(blk_lo, blk_hi, n_valid, tok_buf, hs, w1, w3, w2, layer)
    return _combine(x, ys, slab, mods, final_g, final)


def _cast_kernel(w_ref, o_ref):
    o_ref[...] = w_ref[...].astype(BF16)


def _to_bf16(w, rows):
    n, cols = w.shape
    return pl.pallas_call(
        _cast_kernel,
        grid=(n // rows,),
        in_specs=[pl.BlockSpec((rows, cols), lambda i: (i, 0))],
        out_specs=pl.BlockSpec((rows, cols), lambda i: (i, 0)),
        out_shape=jax.ShapeDtypeStruct((n, cols), BF16),
        compiler_params=_params("arbitrary"),
        name="weights_to_bf16",
    )(w)


def _sincos_2d(rows, width, dim):
    quarter = dim // 4
    omega = 1.0 / (POS_BASE ** (jnp.arange(quarter, dtype=F32) / quarter))
    ar = jnp.arange(rows, dtype=F32)[:, None] * omega
    ac = jnp.arange(width, dtype=F32)[:, None] * omega
    row_part = jnp.concatenate([jnp.sin(ar), jnp.cos(ar)], axis=-1)
    col_part = jnp.concatenate([jnp.sin(ac), jnp.cos(ac)], axis=-1)
    row_part = jnp.broadcast_to(row_part[:, None, :], (rows, width, dim // 2))
    col_part = jnp.broadcast_to(col_part[None, :, :], (rows, width, dim // 2))
    return jnp.concatenate([row_part, col_part], axis=-1).reshape(rows * width, dim)


def kernel(x, c, ctx, c_ctx, ada_w, ada_b, norm_g, final_g, conv_pw1_w, conv_pw1_b, conv_dw_w, conv_dw_b, conv_ln_g, conv_ln_b, conv_pw2_w, conv_pw2_b, gla_w_in, gla_gk_up, gla_gk_b, gla_head_g, gla_w_o, moe_wg, moe_bg, moe_wr, moe_br, moe_w1, moe_w3, moe_w2):
    assert x.shape == (BATCH, SEQ, D) and ctx.shape == (BATCH, CTX_LEN, D)
    pos = _sincos_2d(SEQ // GRID_W, GRID_W, D)
    xt = jnp.concatenate([ctx.reshape(T_CTX, D), (x + pos[None]).reshape(T_LAT, D)], axis=0)

    c8 = jnp.concatenate([c_ctx[None], c, jnp.zeros((8 - 1 - BATCH, D), F32)], axis=0)
    mods = _ada_mods(c8, ada_w, ada_b)
    fg = final_g.reshape(1, D)
    w1b = _to_bf16(moe_w1.reshape(-1, FF), 4 * D).reshape(moe_w1.shape)
    w3b = _to_bf16(moe_w3.reshape(-1, FF), 4 * D).reshape(moe_w3.shape)
    w2b = _to_bf16(moe_w2.reshape(-1, D), 4 * FF).reshape(moe_w2.shape)

    for i in range(DEPTH):
        j = i // 2
        g_mix = norm_g[i, 0].reshape(1, D)
        if i % 2 == 0:
            u = _pw1(xt, g_mix, mods[i], conv_pw1_w[j].astype(BF16), conv_pw1_b[j].reshape(1, 2 * D))
            dww = jnp.concatenate([conv_dw_w[j], jnp.zeros((1, D), F32)], axis=0)
            xt = _conv(u, xt, mods[i], dww, conv_dw_b[j].reshape(1, D), conv_ln_g[j].reshape(1, D),
                       conv_ln_b[j].reshape(1, D), conv_pw2_w[j].astype(BF16),
                       conv_pw2_b[j].reshape(1, D))
        else:
            w_in = jnp.pad(gla_w_in[j], ((0, 0), (0, W_IN_PAD - gla_w_in.shape[-1]))).astype(BF16)
            gk = jnp.zeros((LANES, 2 * KEY_DIM), F32)
            gk = gk.at[0:GATE_RANK, 0:KEY_DIM].set(gla_gk_up[j, 0])
            gk = gk.at[GATE_RANK:2 * GATE_RANK, KEY_DIM:].set(gla_gk_up[j, 1])
            gkb = gla_gk_b[j].reshape(1, 2 * KEY_DIM)
            q, k, v, gate, lf, lb = _gla_proj(xt, g_mix, mods[i], w_in, gk, gkb)
            o_f = _gla_scan(q, k, v, lf, None, reverse=False)
            og = _gla_scan(q, k, v, lb, (o_f, gate, gla_head_g[j].reshape(1, D)), reverse=True)
            xt = _matmul_res(og, gla_w_o[j].astype(BF16), xt, mods[i])
        wr_cat = jnp.concatenate([moe_wr[i], moe_wg[i],
                                  jnp.zeros((D, LANES - EXPERTS - GROUPS), F32)], axis=1)
        br_cat = jnp.concatenate([moe_br[i], moe_bg[i],
                                  jnp.zeros((LANES - EXPERTS - GROUPS,), F32)]).reshape(1, LANES)
        xt = _moe(xt, norm_g[i, 1].reshape(1, D), mods[i], wr_cat, br_cat,
                  w1b, w3b, w2b, i, fg, final=(i == DEPTH - 1))
    return xt.reshape(BATCH, SEQ, D)
```

```python
import functools

import jax
import jax.numpy as jnp
from jax import lax
from jax.experimental import pallas as pl
from jax.experimental.pallas import tpu as pltpu

F32 = jnp.float32
BF16 = jnp.bfloat16
HIGHEST = lax.Precision.HIGHEST

D = 1024
BATCH = 2
SEQ = 16384
CTX_LEN = 256
DEPTH = 4
GRID_W = 64
POS_BASE = 10000.0
EPS = 1e-6
CONV_WIDTH = 31
CONV_PAD = CONV_WIDTH // 2
HEADS = 4
KEY_DIM = D // 2
HEAD_K = KEY_DIM // HEADS
HEAD_V = D // HEADS
GATE_RANK = 16
GATE_NORM = 16.0
CHUNK = 64
GROUPS = 4
PER_GROUP = 8
EXPERTS = GROUPS * PER_GROUP
FF = D // 2

TM = 256
T_CTX = BATCH * CTX_LEN
T_LAT = BATCH * SEQ
T = T_CTX + T_LAT
NT = T // TM
TR = 512
NR = T // TR
CTX_TILES = T_CTX // TM
LAT_TILES = SEQ // TM
HALO = 16
CONV_ROWS = 128
CONV_WIN = CONV_ROWS + 2 * HALO
SUBLANES = 8
BM = 256
ROW_GROUP = 8
PAIRS = PER_GROUP * (PER_GROUP - 1) // 2
NCLS = GROUPS * PAIRS
NB = T // BM + NCLS
P = NB * BM
LANES = 128
SUB = D // LANES
YROWS = 2 * SUB
XPITCH = 12
YPITCH = 20
W_IN_PAD = 2 * KEY_DIM + 2 * D + LANES
VMEM_LIMIT = 48 * 1024 * 1024

_NT_DIMS = (((1,), (1,)), ((), ()))
_TN_DIMS = (((0,), (0,)), ((), ()))


def _params(*sem):
    return pltpu.CompilerParams(dimension_semantics=sem, vmem_limit_bytes=VMEM_LIMIT)


def _seg(i, rows):
    ctx_tiles = T_CTX // rows
    return jnp.where(i < ctx_tiles, 0, jnp.where(i < ctx_tiles + SEQ // rows, 1, 2))


def _mod_spec(rows=TR, first=0):
    return pl.BlockSpec((None, 6, D), lambda i: (_seg(i + first, rows), 0, 0))


def _row_spec(width, rows=TR):
    return pl.BlockSpec((rows, width), lambda i: (i, 0))


def _full_spec(shape):
    return pl.BlockSpec(shape, lambda i: (0,) * len(shape))


def _normmod(x, g, shift, scale):
    ms = jnp.mean(x * x, axis=-1, keepdims=True)
    return (x * lax.rsqrt(ms + EPS) * g) * (1.0 + scale) + shift


def _sigmoid(x):
    return 1.0 / (1.0 + jnp.exp(-x))


def _split_bf16(a):
    hi = a.astype(BF16)
    return hi, (a - hi.astype(F32)).astype(BF16)


def _dot3(a, w_hi, w_lo):
    a_hi, a_lo = _split_bf16(a)
    return (jnp.dot(a_hi, w_hi, preferred_element_type=F32)
            + jnp.dot(a_lo, w_hi, preferred_element_type=F32)
            + jnp.dot(a_hi, w_lo, preferred_element_type=F32))


def _ada_kernel(c_ref, w_ref, b_ref, o_ref):
    c = c_ref[...]
    s = c * _sigmoid(c)
    o_ref[...] = jnp.dot(s, w_ref[...], precision=HIGHEST, preferred_element_type=F32) + b_ref[...]


def _ada_mods(c8, ada_w, ada_b):
    out = pl.pallas_call(
        _ada_kernel,
        grid=(DEPTH, 6),
        in_specs=[pl.BlockSpec((8, D), lambda l, n: (0, 0)),
                  pl.BlockSpec((None, D, D), lambda l, n: (l, 0, n)),
                  pl.BlockSpec((None, 1, D), lambda l, n: (l, 0, n))],
        out_specs=pl.BlockSpec((None, 8, D), lambda l, n: (l, 0, n)),
        out_shape=jax.ShapeDtypeStruct((DEPTH, 8, 6 * D), F32),
        compiler_params=_params("arbitrary", "arbitrary"),
        name="ada_mods",
    )(c8, ada_w, ada_b.reshape(DEPTH, 1, 6 * D))
    return out[:, :3].reshape(DEPTH, 3, 6, D)


def _pw1_kernel(x_ref, g_ref, mod_ref, w_ref, b_ref, u_ref):
    h = _normmod(x_ref[...], g_ref[...], mod_ref[0:1, :], mod_ref[1:2, :]).astype(BF16)
    a = jnp.dot(h, w_ref[:, :D], preferred_element_type=F32) + b_ref[:, :D]
    g = jnp.dot(h, w_ref[:, D:], preferred_element_type=F32) + b_ref[:, D:]
    u_ref[...] = a * _sigmoid(g)


def _pw1(x, g, mods, w, b):
    return pl.pallas_call(
        _pw1_kernel,
        grid=(NR,),
        in_specs=[_row_spec(D), _full_spec((1, D)), _mod_spec(),
                  _full_spec((D, 2 * D)), _full_spec((1, 2 * D))],
        out_specs=_row_spec(D),
        out_shape=jax.ShapeDtypeStruct((T, D), F32),
        compiler_params=_params("arbitrary"),
        name="conv_pw1_glu",
    )(x, g, mods, w, b)


def _conv_kernel(prev_ref, cur_ref, next_ref, x_ref, mod_ref, dww_ref, dwb_ref, lng_ref, lnb_ref,
                 w_ref, b_ref, o_ref, ext_ref, conv_ref):
    i = pl.program_id(0)
    lat0 = CTX_TILES
    lat1 = CTX_TILES + LAT_TILES
    is_start = (i < CTX_TILES) | (i == lat0) | (i == lat1)
    is_end = (i < CTX_TILES) | (i == lat1 - 1) | (i == NT - 1)
    ext_ref[0:HALO, :] = jnp.where(is_start, 0.0, prev_ref[...])
    ext_ref[HALO:HALO + TM, :] = cur_ref[...]
    ext_ref[HALO + TM:, :] = jnp.where(is_end, 0.0, next_ref[...])

    def lane_block(j, carry):
        lo = pl.multiple_of(j * LANES, LANES)
        for rb in range(TM // CONV_ROWS):
            r0 = rb * CONV_ROWS
            win = ext_ref[r0:r0 + CONV_WIN, pl.ds(lo, LANES)]
            acc = jnp.broadcast_to(dwb_ref[:, pl.ds(lo, LANES)], (CONV_ROWS, LANES))
            for rho in range(SUBLANES):
                shifted = win if rho == 0 else pltpu.roll(win, shift=CONV_WIN - rho, axis=0)
                for k in range(CONV_WIDTH):
                    off = k + HALO - CONV_PAD - rho
                    if off % SUBLANES == 0:
                        acc = acc + dww_ref[k:k + 1, pl.ds(lo, LANES)] * shifted[off:off + CONV_ROWS]
            conv_ref[r0:r0 + CONV_ROWS, pl.ds(lo, LANES)] = acc
        return carry

    lax.fori_loop(0, D // LANES, lane_block, 0)
    v = conv_ref[...]
    mu = jnp.mean(v, axis=-1, keepdims=True)
    vc = v - mu
    var = jnp.mean(vc * vc, axis=-1, keepdims=True)
    y = vc * lax.rsqrt(var + EPS) * lng_ref[...] + lnb_ref[...]
    y = (y * _sigmoid(y)).astype(BF16)
    out = jnp.dot(y, w_ref[...], preferred_element_type=F32) + b_ref[...]
    o_ref[...] = x_ref[...] + mod_ref[2:3, :] * out


def _conv(u, x, mods, dww, dwb, lng, lnb, w, b):
    per = TM // HALO
    return pl.pallas_call(
        _conv_kernel,
        grid=(NT,),
        in_specs=[pl.BlockSpec((HALO, D), lambda i: (jnp.maximum(i * per - 1, 0), 0)),
                  _row_spec(D, TM),
                  pl.BlockSpec((HALO, D), lambda i: (jnp.minimum((i + 1) * per, T // HALO - 1), 0)),
                  _row_spec(D, TM), _mod_spec(TM),
                  _full_spec((CONV_WIDTH + 1, D)), _full_spec((1, D)), _full_spec((1, D)),
                  _full_spec((1, D)), _full_spec((D, D)), _full_spec((1, D))],
        out_specs=_row_spec(D, TM),
        out_shape=jax.ShapeDtypeStruct((T, D), F32),
        scratch_shapes=[pltpu.VMEM((TM + 2 * HALO, D), F32), pltpu.VMEM((TM, D), F32)],
        compiler_params=_params("arbitrary"),
        name="conv_dw_ln_pw2",
    )(u, u, u, x, mods, dww, dwb, lng, lnb, w, b)


def _log_sigmoid(z):
    return jnp.minimum(z, 0.0) - jnp.log(1.0 + jnp.exp(-jnp.abs(z)))


def _gla_proj_kernel(x_ref, g_ref, mod_ref, w_ref, gkh_ref, gkl_ref, gkb_ref,
                     q_ref, k_ref, v_ref, gate_ref, lf_ref, lb_ref):
    h = _normmod(x_ref[...], g_ref[...], mod_ref[0:1, :], mod_ref[1:2, :]).astype(BF16)
    kd = KEY_DIM
    q_ref[...] = jnp.dot(h, w_ref[:, 0:kd], preferred_element_type=F32) * (HEAD_K ** -0.5)
    k_ref[...] = jnp.dot(h, w_ref[:, kd:2 * kd], preferred_element_type=F32)
    v_ref[...] = jnp.dot(h, w_ref[:, 2 * kd:2 * kd + D], preferred_element_type=F32).astype(BF16)
    gate_ref[...] = jnp.dot(h, w_ref[:, 2 * kd + D:2 * kd + 2 * D], preferred_element_type=F32)
    codes = jnp.dot(h, w_ref[:, 2 * kd + 2 * D:], preferred_element_type=F32)
    z = _dot3(codes, gkh_ref[...], gkl_ref[...]) + gkb_ref[...]
    ls = _log_sigmoid(z) * (1.0 / GATE_NORM)
    lf_ref[...] = ls[:, :kd]
    lb_ref[...] = ls[:, kd:]


def _gla_proj(x, g, mods, w, gk, gkb):
    gk_hi, gk_lo = _split_bf16(gk)
    shapes = [jax.ShapeDtypeStruct((T, KEY_DIM), F32), jax.ShapeDtypeStruct((T, KEY_DIM), F32),
              jax.ShapeDtypeStruct((T, D), BF16), jax.ShapeDtypeStruct((T, D), F32),
              jax.ShapeDtypeStruct((T, KEY_DIM), F32), jax.ShapeDtypeStruct((T, KEY_DIM), F32)]
    return pl.pallas_call(
        _gla_proj_kernel,
        grid=(NR,),
        in_specs=[_row_spec(D), _full_spec((1, D)), _mod_spec(), _full_spec((D, W_IN_PAD)),
                  _full_spec((LANES, 2 * KEY_DIM)), _full_spec((LANES, 2 * KEY_DIM)),
                  _full_spec((1, 2 * KEY_DIM))],
        out_specs=[_row_spec(KEY_DIM), _row_spec(KEY_DIM), _row_spec(D), _row_spec(D),
                   _row_spec(KEY_DIM), _row_spec(KEY_DIM)],
        out_shape=shapes,
        compiler_params=_params("arbitrary"),
        name="gla_proj",
    )(x, g, mods, w, gk_hi, gk_lo, gkb)


def _gla_scan_kernel(*refs, reverse):
    if reverse:
        q_ref, k_ref, v_ref, lg_ref, of_ref, gate_ref, hg_ref, o_ref, st_ref, cum_ref = refs
    else:
        q_ref, k_ref, v_ref, lg_ref, o_ref, st_ref, cum_ref = refs
    j = pl.program_id(1)

    @pl.when(j == 0)
    def _():
        st_ref[...] = jnp.zeros_like(st_ref)

    r = lax.broadcasted_iota(jnp.int32, (TM, TM), 0)
    c = lax.broadcasted_iota(jnp.int32, (TM, TM), 1)
    shift = CHUNK.bit_length() - 1
    same = (r >> shift) == (c >> shift)
    tri = jnp.where(same & ((c >= r) if reverse else (c <= r)), 1.0, 0.0).astype(BF16)
    lg = lg_ref[...]
    hi = lg.astype(BF16)
    r1 = lg - hi.astype(F32)
    mid = r1.astype(BF16)
    lo = (r1 - mid.astype(F32)).astype(BF16)
    cum_ref[...] = (jnp.dot(tri, hi, preferred_element_type=F32)
                    + jnp.dot(tri, mid, preferred_element_type=F32)
                    + jnp.dot(tri, lo, preferred_element_type=F32))

    rr = lax.broadcasted_iota(jnp.int32, (CHUNK, CHUNK), 0)
    cc = lax.broadcasted_iota(jnp.int32, (CHUNK, CHUNK), 1)
    mask = (cc >= rr) if reverse else (cc <= rr)
    i_last = 0 if reverse else CHUNK - 1
    i_ref = CHUNK // 2 - 1 if reverse else CHUNK // 2
    chunks = range(TM // CHUNK)
    for ci in (reversed(chunks) if reverse else chunks):
        r0 = ci * CHUNK
        for h in range(HEADS):
            k0, v0 = h * HEAD_K, h * HEAD_V
            cm = cum_ref[r0:r0 + CHUNK, k0:k0 + HEAD_K]
            last = cum_ref[r0 + i_last:r0 + i_last + 1, k0:k0 + HEAD_K]
            ref = cum_ref[r0 + i_ref:r0 + i_ref + 1, k0:k0 + HEAD_K]
            qc = q_ref[r0:r0 + CHUNK, k0:k0 + HEAD_K]
            kc = k_ref[r0:r0 + CHUNK, k0:k0 + HEAD_K]
            vc = v_ref[r0:r0 + CHUNK, v0:v0 + HEAD_V]
            qr = (qc * jnp.exp(cm - ref)).astype(BF16)
            kr = (kc * jnp.exp(ref - cm)).astype(BF16)
            att = lax.dot_general(qr, kr, _NT_DIMS, preferred_element_type=F32)
            att = jnp.where(mask, att, 0.0).astype(BF16)
            qd = (qc * jnp.exp(cm)).astype(BF16)
            st = st_ref[h]
            o = (jnp.dot(att, vc, preferred_element_type=F32)
                 + lax.dot_general(qd, st.astype(BF16), _NT_DIMS, preferred_element_type=F32))
            kdec = (kc * jnp.exp(last - cm)).astype(BF16)
            st_ref[h] = st * jnp.exp(last) + lax.dot_general(vc, kdec, _TN_DIMS,
                                                             preferred_element_type=F32)
            if reverse:
                o = o + of_ref[r0:r0 + CHUNK, v0:v0 + HEAD_V]
                on = o * lax.rsqrt(jnp.mean(o * o, axis=-1, keepdims=True) + EPS)
                on = on * hg_ref[:, v0:v0 + HEAD_V]
                gt = gate_ref[r0:r0 + CHUNK, v0:v0 + HEAD_V]
                o_ref[r0:r0 + CHUNK, v0:v0 + HEAD_V] = (on * (gt * _sigmoid(gt))).astype(BF16)
            else:
                o_ref[r0:r0 + CHUNK, v0:v0 + HEAD_V] = o


def _gla_scan(q, k, v, lg, extra, reverse):
    if reverse:
        def tile(b, j):
            return jnp.where(j == 0, b, CTX_TILES + LAT_TILES * (b + 1) - j)
    else:
        def tile(b, j):
            return jnp.where(j == 0, b, CTX_TILES + LAT_TILES * b + j - 1)

    def rows(width):
        return pl.BlockSpec((TM, width), lambda b, j: (tile(b, j), 0))

    in_specs = [rows(KEY_DIM), rows(KEY_DIM), rows(D), rows(KEY_DIM)]
    args = [q, k, v, lg]
    if reverse:
        in_specs += [rows(D), rows(D), pl.BlockSpec((1, D), lambda b, j: (0, 0))]
        args += list(extra)
    return pl.pallas_call(
        functools.partial(_gla_scan_kernel, reverse=reverse),
        grid=(BATCH, 1 + LAT_TILES),
        in_specs=in_specs,
        out_specs=rows(D),
        out_shape=jax.ShapeDtypeStruct((T, D), BF16 if reverse else F32),
        scratch_shapes=[pltpu.VMEM((HEADS, HEAD_V, HEAD_K), F32), pltpu.VMEM((TM, KEY_DIM), F32)],
        compiler_params=_params("arbitrary", "arbitrary"),
        name="gla_scan_bwd" if reverse else "gla_scan_fwd",
    )(*args)


def _matmul_res_kernel(a_ref, w_ref, x_ref, mod_ref, o_ref):
    y = jnp.dot(a_ref[...], w_ref[...], preferred_element_type=F32)
    o_ref[...] = x_ref[...] + mod_ref[2:3, :] * y


def _matmul_res(a, w, x, mods):
    return pl.pallas_call(
        _matmul_res_kernel,
        grid=(NR,),
        in_specs=[_row_spec(D), _full_spec((D, D)), _row_spec(D), _mod_spec()],
        out_specs=_row_spec(D),
        out_shape=jax.ShapeDtypeStruct((T, D), F32),
        compiler_params=_params("arbitrary"),
        name="gla_out_proj",
    )(a, w, x, mods)


def _router_kernel(x_ref, g_ref, mod_ref, wh_ref, wl_ref, b_ref, h_ref, slab_ref, counts_ref, cnt_ref):
    h = _normmod(x_ref[...], g_ref[...], mod_ref[3:4, :], mod_ref[4:5, :])
    for s in range(SUB):
        h_ref[pl.ds(s, TR, stride=SUB), :] = h[:, s * LANES:(s + 1) * LANES]
    lg = _dot3(h, wh_ref[...], wl_ref[...]) + b_ref[...]
    lane = lax.broadcasted_iota(jnp.int32, (TR, LANES), 1).astype(F32)
    neg = -jnp.inf
    far = float(LANES)
    is_group = (lane >= EXPERTS) & (lane < EXPERTS + GROUPS)
    glog = jnp.where(is_group, lg, neg)
    gmax = jnp.max(glog, axis=-1, keepdims=True)
    gidx = jnp.min(jnp.where(glog == gmax, lane, far), axis=-1, keepdims=True) - EXPERTS
    g_w = 1.0 / jnp.sum(jnp.exp(glog - gmax), axis=-1, keepdims=True)
    lo = gidx * PER_GROUP
    vals = jnp.where((lane >= lo) & (lane < lo + PER_GROUP), lg, neg)
    m1 = jnp.max(vals, axis=-1, keepdims=True)
    i1 = jnp.min(jnp.where(vals == m1, lane, far), axis=-1, keepdims=True)
    vals2 = jnp.where(lane == i1, neg, vals)
    m2 = jnp.max(vals2, axis=-1, keepdims=True)
    i2 = jnp.min(jnp.where(vals2 == m2, lane, far), axis=-1, keepdims=True)
    e = jnp.exp(m2 - m1)
    w1 = g_w / (1.0 + e)
    w2 = g_w * e / (1.0 + e)
    first_lo = i1 < i2
    a = jnp.minimum(i1, i2) - lo
    b = jnp.maximum(i1, i2) - lo
    cls = gidx * PAIRS + a * (PER_GROUP - 1) - a * (a - 1.0) * 0.5 + (b - a - 1.0)
    w_lo = jnp.where(first_lo, w1, w2)
    w_hi = jnp.where(first_lo, w2, w1)
    @pl.when(pl.program_id(0) == 0)
    def _():
        cnt_ref[...] = jnp.zeros_like(cnt_ref)

    onehot = jnp.where(lane == cls, 1.0, 0.0)
    r = lax.broadcasted_iota(jnp.int32, (TR, TR), 0)
    c = lax.broadcasted_iota(jnp.int32, (TR, TR), 1)
    before = jnp.where(c < r, 1.0, 0.0).astype(BF16)
    prefix = jnp.dot(before, onehot.astype(BF16), preferred_element_type=F32)
    base = cnt_ref[...]
    rank = jnp.sum(onehot * (prefix + base), axis=-1, keepdims=True)
    total = base + jnp.sum(onehot, axis=0, keepdims=True)
    cnt_ref[...] = total
    counts_ref[...] = total
    slab_ref[...] = jnp.where(lane == 0, cls, jnp.where(lane == 1, rank, jnp.where(
        lane == 2, w_lo, jnp.where(lane == 3, w_hi, 0.0))))


def _router(x, g, mods, w, b):
    w_hi, w_lo = _split_bf16(w)
    return pl.pallas_call(
        _router_kernel,
        grid=(NR,),
        in_specs=[_row_spec(D), _full_spec((1, D)), _mod_spec(), _full_spec((D, LANES)),
                  _full_spec((D, LANES)), _full_spec((1, LANES))],
        out_specs=[pl.BlockSpec((TR * SUB, LANES), lambda i: (i, 0)), _row_spec(LANES),
                   _full_spec((1, LANES))],
        out_shape=[jax.ShapeDtypeStruct((T * SUB, LANES), F32), jax.ShapeDtypeStruct((T, LANES), F32),
                   jax.ShapeDtypeStruct((1, LANES), F32)],
        scratch_shapes=[pltpu.VMEM((1, LANES), F32)],
        compiler_params=_params("arbitrary"),
        name="moe_router",
    )(x, g, mods, w_hi, w_lo, b)


def _ffn_kernel(lo_ref, hi_ref, nv_ref, tok_ref, out_ref, hs_ref, w1l_ref, w3l_ref, w2l_ref,
                w1h_ref, w3h_ref, w2h_ref, ys_ref, xg, yb, gsem, ssem):
    i = pl.program_id(0)
    slot = i % 2

    def groups(blk):
        return (nv_ref[jnp.clip(blk, 0, NB - 1)] + ROW_GROUP - 1) // ROW_GROUP

    def gather_copy(blk, sl, r):
        tok = tok_ref[blk * BM + r]
        return pltpu.make_async_copy(hs_ref.at[pl.ds(pl.multiple_of(tok * SUB, SUB), SUB)],
                                     xg.at[sl, pl.ds(r * XPITCH, SUB)], gsem.at[sl])

    def scatter_copy(blk, sl, r):
        row = out_ref[blk * BM + r]
        return pltpu.make_async_copy(yb.at[sl, pl.ds(r * YPITCH, YROWS)],
                                     ys_ref.at[pl.ds(pl.multiple_of(row * YROWS, YROWS), YROWS)],
                                     ssem.at[sl])

    def start_block(copy, blk, sl, priority):
        def group(gi, carry):
            for u in range(ROW_GROUP):
                copy(blk, sl, gi * ROW_GROUP + u).start(priority=priority)
            return carry
        lax.fori_loop(0, groups(blk), group, 0)

    def wait_gather(blk, sl):
        n = groups(blk) * (ROW_GROUP * SUB)

        @pl.when(n > 0)
        def _():
            pltpu.make_async_copy(hs_ref.at[pl.ds(0, n)], xg.at[sl, pl.ds(0, n)], gsem.at[sl]).wait()

    def wait_scatter(blk, sl):
        n = groups(blk) * (ROW_GROUP * YROWS)

        @pl.when(n > 0)
        def _():
            pltpu.make_async_copy(yb.at[sl, pl.ds(0, n)], ys_ref.at[pl.ds(0, n)], ssem.at[sl]).wait()

    @pl.when(i == 0)
    def _():
        xg[...] = jnp.zeros_like(xg)
        yb[...] = jnp.zeros_like(yb)
        for sl in range(2):
            spare = pltpu.make_async_copy(yb.at[sl, pl.ds(0, BM * YROWS)],
                                          ys_ref.at[pl.ds((T + sl * BM) * YROWS, BM * YROWS)], ssem.at[sl])
            spare.start()
            spare.wait()
        start_block(gather_copy, 0, 0, 0)

    @pl.when(i + 1 < NB)
    def _():
        start_block(gather_copy, i + 1, 1 - slot, 0)

    wait_gather(i, slot)

    @pl.when(i >= 2)
    def _():
        wait_scatter(i - 2, slot)

    def expert(xb, w1_ref, w3_ref, w2_ref):
        a = jnp.dot(xb, w1_ref[...], preferred_element_type=F32)
        b = jnp.dot(xb, w3_ref[...], preferred_element_type=F32)
        hh = (a * _sigmoid(a) * b).astype(BF16)
        return jnp.dot(hh, w2_ref[...], preferred_element_type=F32)

    @pl.when(nv_ref[i] > 0)
    def _():
        xb = jnp.concatenate([xg[slot, pl.ds(s, BM, stride=XPITCH), :] for s in range(SUB)],
                             axis=-1).astype(BF16)
        y_lo = expert(xb, w1l_ref, w3l_ref, w2l_ref)
        for s in range(SUB):
            yb[slot, pl.ds(s, BM, stride=YPITCH), :] = y_lo[:, s * LANES:(s + 1) * LANES]
        y_hi = expert(xb, w1h_ref, w3h_ref, w2h_ref)
        for s in range(SUB):
            yb[slot, pl.ds(SUB + s, BM, stride=YPITCH), :] = y_hi[:, s * LANES:(s + 1) * LANES]
        start_block(scatter_copy, i, slot, 1)

    @pl.when(i == NB - 1)
    def _():
        wait_scatter(NB - 2, 1 - slot)
        wait_scatter(NB - 1, slot)


def _ffn(blk_lo, blk_hi, n_valid, tok_buf, out_buf, hs, w1, w3, w2, layer):
    def w_spec(shape, which):
        return pl.BlockSpec((None, None) + shape,
                            lambda i, lo, hi, nv, tok, out: (layer, (lo, hi)[which][i], 0, 0))

    return pl.pallas_call(
        _ffn_kernel,
        grid_spec=pltpu.PrefetchScalarGridSpec(
            num_scalar_prefetch=5,
            grid=(NB,),
            in_specs=[pl.BlockSpec(memory_space=pl.ANY),
                      w_spec((D, FF), 0), w_spec((D, FF), 0), w_spec((FF, D), 0),
                      w_spec((D, FF), 1), w_spec((D, FF), 1), w_spec((FF, D), 1)],
            out_specs=pl.BlockSpec(memory_space=pl.ANY),
            scratch_shapes=[pltpu.VMEM((2, BM * XPITCH, LANES), F32), pltpu.VMEM((2, BM * YPITCH, LANES), F32),
                            pltpu.SemaphoreType.DMA((2,)), pltpu.SemaphoreType.DMA((2,))]),
        out_shape=jax.ShapeDtypeStruct(((T + 2 * BM) * YROWS, LANES), F32),
        compiler_params=_params("arbitrary"),
        name="moe_expert_ffn",
    )(blk_lo, blk_hi, n_valid, tok_buf, out_buf, hs, w1, w3, w2, w1, w3, w2)


def _combine_kernel(x_ref, y_ref, slab_ref, mod_ref, fg_ref, o_ref, *, final):
    w_lo = slab_ref[:, 2:3]
    w_hi = slab_ref[:, 3:4]
    y = jnp.concatenate([w_lo * y_ref[pl.ds(s, TR, stride=YROWS), :]
                         + w_hi * y_ref[pl.ds(SUB + s, TR, stride=YROWS), :] for s in range(SUB)], axis=-1)
    x = x_ref[...] + mod_ref[5:6, :] * y
    if final:
        x = x * lax.rsqrt(jnp.mean(x * x, axis=-1, keepdims=True) + EPS) * fg_ref[...]
    o_ref[...] = x


def _combine(x, ys, slab, mods, final_g, final):
    first = T_CTX // TR if final else 0

    def rows(height, width):
        return pl.BlockSpec((height, width), lambda i: (i + first, 0))

    return pl.pallas_call(
        functools.partial(_combine_kernel, final=final),
        grid=(NR - first,),
        in_specs=[rows(TR, D), rows(TR * YROWS, LANES), rows(TR, LANES),
                  _mod_spec(TR, first), _full_spec((1, D))],
        out_specs=_row_spec(D),
        out_shape=jax.ShapeDtypeStruct((T - first * TR, D), F32),
        compiler_params=_params("arbitrary"),
        name="moe_combine",
    )(x, ys, slab, mods, final_g)


def _class_experts():
    lo, hi = [], []
    for g in range(GROUPS):
        for a in range(PER_GROUP):
            for b in range(a + 1, PER_GROUP):
                lo.append(g * PER_GROUP + a)
                hi.append(g * PER_GROUP + b)
    return jnp.array(lo, jnp.int32), jnp.array(hi, jnp.int32)


def _dispatch_plan(slab, counts):
    cls = slab[:, 0].astype(jnp.int32)
    rank = slab[:, 1].astype(jnp.int32)
    counts = counts[0, :NCLS].astype(jnp.int32)
    padded = ((counts + BM - 1) // BM) * BM
    pend = jnp.cumsum(padded)
    pstart = pend - padded
    ids = jnp.arange(NCLS, dtype=jnp.int32)

    def lookup(table, idx):
        return jnp.sum(jnp.where(idx[:, None] == ids[None, :], table[None, :], 0), axis=1)

    dest = lookup(pstart, cls) + rank
    tok_buf = (jnp.arange(P, dtype=jnp.int32) % T).at[dest].set(jnp.arange(T, dtype=jnp.int32))
    blk0 = jnp.arange(NB, dtype=jnp.int32) * BM
    blk_c = jnp.minimum(jnp.sum((pend[None, :] <= blk0[:, None]).astype(jnp.int32), axis=1), NCLS - 1)
    n_valid = jnp.clip(lookup(counts, blk_c) - (blk0 - lookup(pstart, blk_c)), 0, BM)
    cls_lo, cls_hi = _class_experts()
    slot = jnp.arange(P, dtype=jnp.int32)
    spare = T + ((slot // BM) % 2) * BM + slot % BM
    out_buf = jnp.where(slot % BM < jnp.repeat(n_valid, BM), tok_buf, spare)
    return tok_buf, out_buf, lookup(cls_lo, blk_c), lookup(cls_hi, blk_c), n_valid


def _moe(x, g, mods, wr_cat, br_cat, w1, w3, w2, layer, final_g, final):
    hs, slab, counts = _router(x, g, mods, wr_cat, br_cat)
    tok_buf, out_buf, blk_lo, blk_hi, n_valid = _dispatch_plan(slab, counts)
    ys = _ffn(blk_lo, blk_hi, n_valid, tok_buf, out_buf, hs, w1, w3, w2, layer)
    return _combine(x, ys, slab, mods, final_g, final)


def _cast_kernel(w_ref, o_ref):
    o_ref[...] = w_ref[...].astype(BF16)


def _to_bf16(w, rows):
    n, cols = w.shape
    return pl.pallas_call(
        _cast_kernel,
        grid=(n // rows,),
        in_specs=[pl.BlockSpec((rows, cols), lambda i: (i, 0))],
        out_specs=pl.BlockSpec((rows, cols), lambda i: (i, 0)),
        out_shape=jax.ShapeDtypeStruct((n, cols), BF16),
        compiler_params=_params("arbitrary"),
        name="weights_to_bf16",
    )(w)


def _sincos_2d(rows, width, dim):
    quarter = dim // 4
    omega = 1.0 / (POS_BASE ** (jnp.arange(quarter, dtype=F32) / quarter))
    ar = jnp.arange(rows, dtype=F32)[:, None] * omega
    ac = jnp.arange(width, dtype=F32)[:, None] * omega
    row_part = jnp.concatenate([jnp.sin(ar), jnp.cos(ar)], axis=-1)
    col_part = jnp.concatenate([jnp.sin(ac), jnp.cos(ac)], axis=-1)
    row_part = jnp.broadcast_to(row_part[:, None, :], (rows, width, dim // 2))
    col_part = jnp.broadcast_to(col_part[None, :, :], (rows, width, dim // 2))
    return jnp.concatenate([row_part, col_part], axis=-1).reshape(rows * width, dim)


def kernel(x, c, ctx, c_ctx, ada_w, ada_b, norm_g, final_g, conv_pw1_w, conv_pw1_b, conv_dw_w, conv_dw_b, conv_ln_g, conv_ln_b, conv_pw2_w, conv_pw2_b, gla_w_in, gla_gk_up, gla_gk_b, gla_head_g, gla_w_o, moe_wg, moe_bg, moe_wr, moe_br, moe_w1, moe_w3, moe_w2):
    assert x.shape == (BATCH, SEQ, D) and ctx.shape == (BATCH, CTX_LEN, D)
    pos = _sincos_2d(SEQ // GRID_W, GRID_W, D)
    xt = jnp.concatenate([ctx.reshape(T_CTX, D), (x + pos[None]).reshape(T_LAT, D)], axis=0)

    c8 = jnp.concatenate([c_ctx[None], c, jnp.zeros((8 - 1 - BATCH, D), F32)], axis=0)
    mods = _ada_mods(c8, ada_w, ada_b)
    fg = final_g.reshape(1, D)
    w1b = _to_bf16(moe_w1.reshape(-1, FF), 4 * D).reshape(moe_w1.shape)
    w3b = _to_bf16(moe_w3.reshape(-1, FF), 4 * D).reshape(moe_w3.shape)
    w2b = _to_bf16(moe_w2.reshape(-1, D), 4 * FF).reshape(moe_w2.shape)

    for i in range(DEPTH):
        j = i // 2
        g_mix = norm_g[i, 0].reshape(1, D)
        if i % 2 == 0:
            u = _pw1(xt, g_mix, mods[i], conv_pw1_w[j].astype(BF16), conv_pw1_b[j].reshape(1, 2 * D))
            dww = jnp.concatenate([conv_dw_w[j], jnp.zeros((1, D), F32)], axis=0)
            xt = _conv(u, xt, mods[i], dww, conv_dw_b[j].reshape(1, D), conv_ln_g[j].reshape(1, D),
                       conv_ln_b[j].reshape(1, D), conv_pw2_w[j].astype(BF16),
                       conv_pw2_b[j].reshape(1, D))
        else:
            w_in = jnp.pad(gla_w_in[j], ((0, 0), (0, W_IN_PAD - gla_w_in.shape[-1]))).astype(BF16)
            gk = jnp.zeros((LANES, 2 * KEY_DIM), F32)
            gk = gk.at[0:GATE_RANK, 0:KEY_DIM].set(gla_gk_up[j, 0])
            gk = gk.at[GATE_RANK:2 * GATE_RANK, KEY_DIM:].set(gla_gk_up[j, 1])
            gkb = gla_gk_b[j].reshape(1, 2 * KEY_DIM)
            q, k, v, gate, lf, lb = _gla_proj(xt, g_mix, mods[i], w_in, gk, gkb)
            o_f = _gla_scan(q, k, v, lf, None, reverse=False)
            og = _gla_scan(q, k, v, lb, (o_f, gate, gla_head_g[j].reshape(1, D)), reverse=True)
            xt = _matmul_res(og, gla_w_o[j].astype(BF16), xt, mods[i])
        wr_cat = jnp.concatenate([moe_wr[i], moe_wg[i],
                                  jnp.zeros((D, LANES - EXPERTS - GROUPS), F32)], axis=1)
        br_cat = jnp.concatenate([moe_br[i], moe_bg[i],
                                  jnp.zeros((LANES - EXPERTS - GROUPS,), F32)]).reshape(1, LANES)
        xt = _moe(xt, norm_g[i, 1].reshape(1, D), mods[i], wr_cat, br_cat,
                  w1b, w3b, w2b, i, fg, final=(i == DEPTH - 1))
    return xt.reshape(BATCH, SEQ, D)
```

```python
import functools

import jax
import jax.numpy as jnp
from jax import lax
from jax.experimental import pallas as pl
from jax.experimental.pallas import tpu as pltpu

F32 = jnp.float32
BF16 = jnp.bfloat16
HIGHEST = lax.Precision.HIGHEST

D = 1024
BATCH = 2
SEQ = 16384
CTX_LEN = 256
DEPTH = 4
GRID_W = 64
POS_BASE = 10000.0
EPS = 1e-6
CONV_WIDTH = 31
CONV_PAD = CONV_WIDTH // 2
HEADS = 4
KEY_DIM = D // 2
HEAD_K = KEY_DIM // HEADS
HEAD_V = D // HEADS
GATE_RANK = 16
GATE_NORM = 16.0
CHUNK = 64
GROUPS = 4
PER_GROUP = 8
EXPERTS = GROUPS * PER_GROUP
FF = D // 2

TM = 256
T_CTX = BATCH * CTX_LEN
T_LAT = BATCH * SEQ
T = T_CTX + T_LAT
NT = T // TM
TR = 512
NR = T // TR
CTX_TILES = T_CTX // TM
LAT_TILES = SEQ // TM
HALO = 16
CONV_ROWS = 128
CONV_WIN = CONV_ROWS + 2 * HALO
SUBLANES = 8
BM = 256
ROW_GROUP = 16
PAIRS = PER_GROUP * (PER_GROUP - 1) // 2
NCLS = GROUPS * PAIRS
NB = T // BM + NCLS
P = NB * BM
LANES = 128
SUB = D // LANES
YROWS = 2 * SUB
XPITCH = 12
YPITCH = 20
W_IN_PAD = 2 * KEY_DIM + 2 * D + LANES
VMEM_LIMIT = 48 * 1024 * 1024

_NT_DIMS = (((1,), (1,)), ((), ()))
_TN_DIMS = (((0,), (0,)), ((), ()))


def _params(*sem):
    return pltpu.CompilerParams(dimension_semantics=sem, vmem_limit_bytes=VMEM_LIMIT)


def _seg(i, rows):
    ctx_tiles = T_CTX // rows
    return jnp.where(i < ctx_tiles, 0, jnp.where(i < ctx_tiles + SEQ // rows, 1, 2))


def _mod_spec(rows=TR, first=0):
    return pl.BlockSpec((None, 6, D), lambda i: (_seg(i + first, rows), 0, 0))


def _row_spec(width, rows=TR):
    return pl.BlockSpec((rows, width), lambda i: (i, 0))


def _full_spec(shape):
    return pl.BlockSpec(shape, lambda i: (0,) * len(shape))


def _normmod(x, g, shift, scale):
    ms = jnp.mean(x * x, axis=-1, keepdims=True)
    return (x * lax.rsqrt(ms + EPS) * g) * (1.0 + scale) + shift


def _sigmoid(x):
    return 1.0 / (1.0 + jnp.exp(-x))


def _split_bf16(a):
    hi = a.astype(BF16)
    return hi, (a - hi.astype(F32)).astype(BF16)


def _dot3(a, w_hi, w_lo):
    a_hi, a_lo = _split_bf16(a)
    return (jnp.dot(a_hi, w_hi, preferred_element_type=F32)
            + jnp.dot(a_lo, w_hi, preferred_element_type=F32)
            + jnp.dot(a_hi, w_lo, preferred_element_type=F32))


def _ada_kernel(c_ref, w_ref, b_ref, o_ref):
    c = c_ref[...]
    s = c * _sigmoid(c)
    o_ref[...] = jnp.dot(s, w_ref[...], precision=HIGHEST, preferred_element_type=F32) + b_ref[...]


def _ada_mods(c8, ada_w, ada_b):
    out = pl.pallas_call(
        _ada_kernel,
        grid=(DEPTH, 6),
        in_specs=[pl.BlockSpec((8, D), lambda l, n: (0, 0)),
                  pl.BlockSpec((None, D, D), lambda l, n: (l, 0, n)),
                  pl.BlockSpec((None, 1, D), lambda l, n: (l, 0, n))],
        out_specs=pl.BlockSpec((None, 8, D), lambda l, n: (l, 0, n)),
        out_shape=jax.ShapeDtypeStruct((DEPTH, 8, 6 * D), F32),
        compiler_params=_params("arbitrary", "arbitrary"),
        name="ada_mods",
    )(c8, ada_w, ada_b.reshape(DEPTH, 1, 6 * D))
    return out[:, :3].reshape(DEPTH, 3, 6, D)


def _pw1_kernel(x_ref, g_ref, mod_ref, w_ref, b_ref, u_ref):
    h = _normmod(x_ref[...], g_ref[...], mod_ref[0:1, :], mod_ref[1:2, :]).astype(BF16)
    a = jnp.dot(h, w_ref[:, :D], preferred_element_type=F32) + b_ref[:, :D]
    g = jnp.dot(h, w_ref[:, D:], preferred_element_type=F32) + b_ref[:, D:]
    u_ref[...] = a * _sigmoid(g)


def _pw1(x, g, mods, w, b):
    return pl.pallas_call(
        _pw1_kernel,
        grid=(NR,),
        in_specs=[_row_spec(D), _full_spec((1, D)), _mod_spec(),
                  _full_spec((D, 2 * D)), _full_spec((1, 2 * D))],
        out_specs=_row_spec(D),
        out_shape=jax.ShapeDtypeStruct((T, D), F32),
        compiler_params=_params("arbitrary"),
        name="conv_pw1_glu",
    )(x, g, mods, w, b)


def _conv_kernel(prev_ref, cur_ref, next_ref, x_ref, mod_ref, dww_ref, dwb_ref, lng_ref, lnb_ref,
                 w_ref, b_ref, o_ref, ext_ref, conv_ref):
    i = pl.program_id(0)
    lat0 = CTX_TILES
    lat1 = CTX_TILES + LAT_TILES
    is_start = (i < CTX_TILES) | (i == lat0) | (i == lat1)
    is_end = (i < CTX_TILES) | (i == lat1 - 1) | (i == NT - 1)
    ext_ref[0:HALO, :] = jnp.where(is_start, 0.0, prev_ref[...])
    ext_ref[HALO:HALO + TM, :] = cur_ref[...]
    ext_ref[HALO + TM:, :] = jnp.where(is_end, 0.0, next_ref[...])

    def lane_block(j, carry):
        lo = pl.multiple_of(j * LANES, LANES)
        for rb in range(TM // CONV_ROWS):
            r0 = rb * CONV_ROWS
            win = ext_ref[r0:r0 + CONV_WIN, pl.ds(lo, LANES)]
            acc = jnp.broadcast_to(dwb_ref[:, pl.ds(lo, LANES)], (CONV_ROWS, LANES))
            for rho in range(SUBLANES):
                shifted = win if rho == 0 else pltpu.roll(win, shift=CONV_WIN - rho, axis=0)
                for k in range(CONV_WIDTH):
                    off = k + HALO - CONV_PAD - rho
                    if off % SUBLANES == 0:
                        acc = acc + dww_ref[k:k + 1, pl.ds(lo, LANES)] * shifted[off:off + CONV_ROWS]
            conv_ref[r0:r0 + CONV_ROWS, pl.ds(lo, LANES)] = acc
        return carry

    lax.fori_loop(0, D // LANES, lane_block, 0)
    v = conv_ref[...]
    mu = jnp.mean(v, axis=-1, keepdims=True)
    vc = v - mu
    var = jnp.mean(vc * vc, axis=-1, keepdims=True)
    y = vc * lax.rsqrt(var + EPS) * lng_ref[...] + lnb_ref[...]
    y = (y * _sigmoid(y)).astype(BF16)
    out = jnp.dot(y, w_ref[...], preferred_element_type=F32) + b_ref[...]
    o_ref[...] = x_ref[...] + mod_ref[2:3, :] * out


def _conv(u, x, mods, dww, dwb, lng, lnb, w, b):
    per = TM // HALO
    return pl.pallas_call(
        _conv_kernel,
        grid=(NT,),
        in_specs=[pl.BlockSpec((HALO, D), lambda i: (jnp.maximum(i * per - 1, 0), 0)),
                  _row_spec(D, TM),
                  pl.BlockSpec((HALO, D), lambda i: (jnp.minimum((i + 1) * per, T // HALO - 1), 0)),
                  _row_spec(D, TM), _mod_spec(TM),
                  _full_spec((CONV_WIDTH + 1, D)), _full_spec((1, D)), _full_spec((1, D)),
                  _full_spec((1, D)), _full_spec((D, D)), _full_spec((1, D))],
        out_specs=_row_spec(D, TM),
        out_shape=jax.ShapeDtypeStruct((T, D), F32),
        scratch_shapes=[pltpu.VMEM((TM + 2 * HALO, D), F32), pltpu.VMEM((TM, D), F32)],
        compiler_params=_params("arbitrary"),
        name="conv_dw_ln_pw2",
    )(u, u, u, x, mods, dww, dwb, lng, lnb, w, b)


def _log_sigmoid(z):
    return jnp.minimum(z, 0.0) - jnp.log(1.0 + jnp.exp(-jnp.abs(z)))


def _gla_proj_kernel(x_ref, g_ref, mod_ref, w_ref, gkh_ref, gkl_ref, gkb_ref,
                     q_ref, k_ref, v_ref, gate_ref, lf_ref, lb_ref):
    h = _normmod(x_ref[...], g_ref[...], mod_ref[0:1, :], mod_ref[1:2, :]).astype(BF16)
    kd = KEY_DIM
    q_ref[...] = jnp.dot(h, w_ref[:, 0:kd], preferred_element_type=F32) * (HEAD_K ** -0.5)
    k_ref[...] = jnp.dot(h, w_ref[:, kd:2 * kd], preferred_element_type=F32)
    v_ref[...] = jnp.dot(h, w_ref[:, 2 * kd:2 * kd + D], preferred_element_type=F32).astype(BF16)
    gate_ref[...] = jnp.dot(h, w_ref[:, 2 * kd + D:2 * kd + 2 * D], preferred_element_type=F32)
    codes = jnp.dot(h, w_ref[:, 2 * kd + 2 * D:], preferred_element_type=F32)
    z = _dot3(codes, gkh_ref[...], gkl_ref[...]) + gkb_ref[...]
    ls = _log_sigmoid(z) * (1.0 / GATE_NORM)
    lf_ref[...] = ls[:, :kd]
    lb_ref[...] = ls[:, kd:]


def _gla_proj(x, g, mods, w, gk, gkb):
    gk_hi, gk_lo = _split_bf16(gk)
    shapes = [jax.ShapeDtypeStruct((T, KEY_DIM), F32), jax.ShapeDtypeStruct((T, KEY_DIM), F32),
              jax.ShapeDtypeStruct((T, D), BF16), jax.ShapeDtypeStruct((T, D), F32),
              jax.ShapeDtypeStruct((T, KEY_DIM), F32), jax.ShapeDtypeStruct((T, KEY_DIM), F32)]
    return pl.pallas_call(
        _gla_proj_kernel,
        grid=(NT,),
        in_specs=[_row_spec(D, TM), _full_spec((1, D)), _mod_spec(TM), _full_spec((D, W_IN_PAD)),
                  _full_spec((LANES, 2 * KEY_DIM)), _full_spec((LANES, 2 * KEY_DIM)),
                  _full_spec((1, 2 * KEY_DIM))],
        out_specs=[_row_spec(KEY_DIM, TM), _row_spec(KEY_DIM, TM), _row_spec(D, TM), _row_spec(D, TM),
                   _row_spec(KEY_DIM, TM), _row_spec(KEY_DIM, TM)],
        out_shape=shapes,
        compiler_params=_params("arbitrary"),
        name="gla_proj",
    )(x, g, mods, w, gk_hi, gk_lo, gkb)


def _scan_tile(q_ref, k_ref, v_ref, lg_ref, o_ref, st_ref, cum_ref, reverse):
    r = lax.broadcasted_iota(jnp.int32, (TM, TM), 0)
    c = lax.broadcasted_iota(jnp.int32, (TM, TM), 1)
    shift = CHUNK.bit_length() - 1
    same = (r >> shift) == (c >> shift)
    tri = jnp.where(same & ((c >= r) if reverse else (c <= r)), 1.0, 0.0).astype(BF16)
    lg = lg_ref[...]
    hi = lg.astype(BF16)
    r1 = lg - hi.astype(F32)
    mid = r1.astype(BF16)
    lo = (r1 - mid.astype(F32)).astype(BF16)
    cum_ref[...] = (jnp.dot(tri, hi, preferred_element_type=F32)
                    + jnp.dot(tri, mid, preferred_element_type=F32)
                    + jnp.dot(tri, lo, preferred_element_type=F32))

    rr = lax.broadcasted_iota(jnp.int32, (CHUNK, CHUNK), 0)
    cc = lax.broadcasted_iota(jnp.int32, (CHUNK, CHUNK), 1)
    mask = (cc >= rr) if reverse else (cc <= rr)
    i_last = 0 if reverse else CHUNK - 1
    i_ref = CHUNK // 2 - 1 if reverse else CHUNK // 2
    chunks = range(TM // CHUNK)
    for ci in (reversed(chunks) if reverse else chunks):
        r0 = ci * CHUNK
        for h in range(HEADS):
            k0, v0 = h * HEAD_K, h * HEAD_V
            cm = cum_ref[r0:r0 + CHUNK, k0:k0 + HEAD_K]
            last = cum_ref[r0 + i_last:r0 + i_last + 1, k0:k0 + HEAD_K]
            ref = cum_ref[r0 + i_ref:r0 + i_ref + 1, k0:k0 + HEAD_K]
            qc = q_ref[r0:r0 + CHUNK, k0:k0 + HEAD_K]
            kc = k_ref[r0:r0 + CHUNK, k0:k0 + HEAD_K]
            vc = v_ref[r0:r0 + CHUNK, v0:v0 + HEAD_V]
            qr = (qc * jnp.exp(cm - ref)).astype(BF16)
            kr = (kc * jnp.exp(ref - cm)).astype(BF16)
            att = lax.dot_general(qr, kr, _NT_DIMS, preferred_element_type=F32)
            att = jnp.where(mask, att, 0.0).astype(BF16)
            qd = (qc * jnp.exp(cm)).astype(BF16)
            st = st_ref[h]
            o = (jnp.dot(att, vc, preferred_element_type=F32)
                 + lax.dot_general(qd, st.astype(BF16), _NT_DIMS, preferred_element_type=F32))
            kdec = (kc * jnp.exp(last - cm)).astype(BF16)
            st_ref[h] = st * jnp.exp(last) + lax.dot_general(vc, kdec, _TN_DIMS,
                                                             preferred_element_type=F32)
            o_ref[r0:r0 + CHUNK, v0:v0 + HEAD_V] = o


def _gla_scan_kernel(qf_ref, kf_ref, vf_ref, lf_ref, qb_ref, kb_ref, vb_ref, lb_ref,
                     of_ref, ob_ref, st_ref, cum_ref):
    @pl.when(pl.program_id(1) == 0)
    def _():
        st_ref[...] = jnp.zeros_like(st_ref)

    _scan_tile(qf_ref, kf_ref, vf_ref, lf_ref, of_ref, st_ref.at[0], cum_ref.at[0], False)
    _scan_tile(qb_ref, kb_ref, vb_ref, lb_ref, ob_ref, st_ref.at[1], cum_ref.at[1], True)


def _gla_scan(q, k, v, lf, lb):
    def tile_f(b, j):
        return jnp.where(j == 0, b, CTX_TILES + LAT_TILES * b + j - 1)

    def tile_b(b, j):
        return jnp.where(j == 0, b, CTX_TILES + LAT_TILES * (b + 1) - j)

    def rows(width, tile):
        return pl.BlockSpec((TM, width), lambda b, j: (tile(b, j), 0))

    def scan_specs(tile):
        return [rows(KEY_DIM, tile), rows(KEY_DIM, tile), rows(D, tile), rows(KEY_DIM, tile)]

    return pl.pallas_call(
        _gla_scan_kernel,
        grid=(BATCH, 1 + LAT_TILES),
        in_specs=scan_specs(tile_f) + scan_specs(tile_b),
        out_specs=[rows(D, tile_f), rows(D, tile_b)],
        out_shape=[jax.ShapeDtypeStruct((T, D), F32), jax.ShapeDtypeStruct((T, D), F32)],
        scratch_shapes=[pltpu.VMEM((2, HEADS, HEAD_V, HEAD_K), F32), pltpu.VMEM((2, TM, KEY_DIM), F32)],
        compiler_params=_params("arbitrary", "arbitrary"),
        name="gla_scan",
    )(q, k, v, lf, q, k, v, lb)


def _gla_out_kernel(of_ref, ob_ref, gate_ref, hg_ref, w_ref, x_ref, mod_ref, o_ref):
    heads = []
    for h in range(HEADS):
        v0 = h * HEAD_V
        o = ob_ref[:, v0:v0 + HEAD_V] + of_ref[:, v0:v0 + HEAD_V]
        on = o * lax.rsqrt(jnp.mean(o * o, axis=-1, keepdims=True) + EPS)
        on = on * hg_ref[:, v0:v0 + HEAD_V]
        gt = gate_ref[:, v0:v0 + HEAD_V]
        heads.append((on * (gt * _sigmoid(gt))).astype(BF16))
    y = jnp.dot(jnp.concatenate(heads, axis=-1), w_ref[...], preferred_element_type=F32)
    o_ref[...] = x_ref[...] + mod_ref[2:3, :] * y


def _gla_out(o_f, o_b, gate, head_g, w, x, mods):
    return pl.pallas_call(
        _gla_out_kernel,
        grid=(NR,),
        in_specs=[_row_spec(D), _row_spec(D), _row_spec(D), _full_spec((1, D)), _full_spec((D, D)),
                  _row_spec(D), _mod_spec()],
        out_specs=_row_spec(D),
        out_shape=jax.ShapeDtypeStruct((T, D), F32),
        compiler_params=_params("arbitrary"),
        name="gla_out_proj",
    )(o_f, o_b, gate, head_g, w, x, mods)


def _router_kernel(x_ref, g_ref, mod_ref, wh_ref, wl_ref, b_ref, h_ref, slab_ref, counts_ref, cnt_ref):
    h = _normmod(x_ref[...], g_ref[...], mod_ref[3:4, :], mod_ref[4:5, :])
    for s in range(SUB):
        h_ref[pl.ds(s, TR, stride=SUB), :] = h[:, s * LANES:(s + 1) * LANES]
    lg = _dot3(h, wh_ref[...], wl_ref[...]) + b_ref[...]
    lane = lax.broadcasted_iota(jnp.int32, (TR, LANES), 1).astype(F32)
    neg = -jnp.inf
    far = float(LANES)
    is_group = (lane >= EXPERTS) & (lane < EXPERTS + GROUPS)
    glog = jnp.where(is_group, lg, neg)
    gmax = jnp.max(glog, axis=-1, keepdims=True)
    gidx = jnp.min(jnp.where(glog == gmax, lane, far), axis=-1, keepdims=True) - EXPERTS
    g_w = 1.0 / jnp.sum(jnp.exp(glog - gmax), axis=-1, keepdims=True)
    lo = gidx * PER_GROUP
    vals = jnp.where((lane >= lo) & (lane < lo + PER_GROUP), lg, neg)
    m1 = jnp.max(vals, axis=-1, keepdims=True)
    i1 = jnp.min(jnp.where(vals == m1, lane, far), axis=-1, keepdims=True)
    vals2 = jnp.where(lane == i1, neg, vals)
    m2 = jnp.max(vals2, axis=-1, keepdims=True)
    i2 = jnp.min(jnp.where(vals2 == m2, lane, far), axis=-1, keepdims=True)
    e = jnp.exp(m2 - m1)
    w1 = g_w / (1.0 + e)
    w2 = g_w * e / (1.0 + e)
    first_lo = i1 < i2
    a = jnp.minimum(i1, i2) - lo
    b = jnp.maximum(i1, i2) - lo
    cls = gidx * PAIRS + a * (PER_GROUP - 1) - a * (a - 1.0) * 0.5 + (b - a - 1.0)
    w_lo = jnp.where(first_lo, w1, w2)
    w_hi = jnp.where(first_lo, w2, w1)
    @pl.when(pl.program_id(0) == 0)
    def _():
        cnt_ref[...] = jnp.zeros_like(cnt_ref)

    onehot = jnp.where(lane == cls, 1.0, 0.0)
    r = lax.broadcasted_iota(jnp.int32, (TR, TR), 0)
    c = lax.broadcasted_iota(jnp.int32, (TR, TR), 1)
    before = jnp.where(c < r, 1.0, 0.0).astype(BF16)
    prefix = jnp.dot(before, onehot.astype(BF16), preferred_element_type=F32)
    base = cnt_ref[...]
    rank = jnp.sum(onehot * (prefix + base), axis=-1, keepdims=True)
    total = base + jnp.sum(onehot, axis=0, keepdims=True)
    cnt_ref[...] = total
    counts_ref[...] = total
    slab_ref[...] = jnp.where(lane == 0, cls, jnp.where(lane == 1, rank, jnp.where(
        lane == 2, w_lo, jnp.where(lane == 3, w_hi, 0.0))))


def _router(x, g, mods, w, b):
    w_hi, w_lo = _split_bf16(w)
    return pl.pallas_call(
        _router_kernel,
        grid=(NR,),
        in_specs=[_row_spec(D), _full_spec((1, D)), _mod_spec(), _full_spec((D, LANES)),
                  _full_spec((D, LANES)), _full_spec((1, LANES))],
        out_specs=[pl.BlockSpec((TR * SUB, LANES), lambda i: (i, 0)), _row_spec(LANES),
                   _full_spec((1, LANES))],
        out_shape=[jax.ShapeDtypeStruct((T * SUB, LANES), F32), jax.ShapeDtypeStruct((T, LANES), F32),
                   jax.ShapeDtypeStruct((1, LANES), F32)],
        scratch_shapes=[pltpu.VMEM((1, LANES), F32)],
        compiler_params=_params("arbitrary"),
        name="moe_router",
    )(x, g, mods, w_hi, w_lo, b)


def _ffn_kernel(lo_ref, hi_ref, nv_ref, tok_ref, out_ref, hs_ref, w1l_ref, w3l_ref, w2l_ref,
                w1h_ref, w3h_ref, w2h_ref, ys_ref, xg, yb, gsem, ssem):
    i = pl.program_id(0)
    slot = i % 2

    def groups(blk):
        return (nv_ref[jnp.clip(blk, 0, NB - 1)] + ROW_GROUP - 1) // ROW_GROUP

    def gather_copy(blk, sl, r):
        tok = tok_ref[blk * BM + r]
        return pltpu.make_async_copy(hs_ref.at[pl.ds(pl.multiple_of(tok * SUB, SUB), SUB)],
                                     xg.at[sl, pl.ds(r * XPITCH, SUB)], gsem.at[sl])

    def scatter_copy(blk, sl, r):
        row = out_ref[blk * BM + r]
        return pltpu.make_async_copy(yb.at[sl, pl.ds(r * YPITCH, YROWS)],
                                     ys_ref.at[pl.ds(pl.multiple_of(row * YROWS, YROWS), YROWS)],
                                     ssem.at[sl])

    def start_block(copy, blk, sl, priority):
        def group(gi, carry):
            for u in range(ROW_GROUP):
                copy(blk, sl, gi * ROW_GROUP + u).start(priority=priority)
            return carry
        lax.fori_loop(0, groups(blk), group, 0)

    def wait_gather(blk, sl):
        n = groups(blk) * (ROW_GROUP * SUB)

        @pl.when(n > 0)
        def _():
            pltpu.make_async_copy(hs_ref.at[pl.ds(0, n)], xg.at[sl, pl.ds(0, n)], gsem.at[sl]).wait()

    def wait_scatter(blk, sl):
        n = groups(blk) * (ROW_GROUP * YROWS)

        @pl.when(n > 0)
        def _():
            pltpu.make_async_copy(yb.at[sl, pl.ds(0, n)], ys_ref.at[pl.ds(0, n)], ssem.at[sl]).wait()

    @pl.when(i == 0)
    def _():
        xg[...] = jnp.zeros_like(xg)
        yb[...] = jnp.zeros_like(yb)
        for sl in range(2):
            spare = pltpu.make_async_copy(yb.at[sl, pl.ds(0, BM * YROWS)],
                                          ys_ref.at[pl.ds((T + sl * BM) * YROWS, BM * YROWS)], ssem.at[sl])
            spare.start()
            spare.wait()
        start_block(gather_copy, 0, 0, 0)

    @pl.when(i + 1 < NB)
    def _():
        start_block(gather_copy, i + 1, 1 - slot, 0)

    wait_gather(i, slot)

    @pl.when(i >= 2)
    def _():
        wait_scatter(i - 2, slot)

    def expert(xb, w1_ref, w3_ref, w2_ref):
        a = jnp.dot(xb, w1_ref[...], preferred_element_type=F32)
        b = jnp.dot(xb, w3_ref[...], preferred_element_type=F32)
        hh = (a * _sigmoid(a) * b).astype(BF16)
        return jnp.dot(hh, w2_ref[...], preferred_element_type=F32)

    @pl.when(nv_ref[i] > 0)
    def _():
        xb = jnp.concatenate([xg[slot, pl.ds(s, BM, stride=XPITCH), :] for s in range(SUB)],
                             axis=-1).astype(BF16)
        y_lo = expert(xb, w1l_ref, w3l_ref, w2l_ref)
        for s in range(SUB):
            yb[slot, pl.ds(s, BM, stride=YPITCH), :] = y_lo[:, s * LANES:(s + 1) * LANES]
        y_hi = expert(xb, w1h_ref, w3h_ref, w2h_ref)
        for s in range(SUB):
            yb[slot, pl.ds(SUB + s, BM, stride=YPITCH), :] = y_hi[:, s * LANES:(s + 1) * LANES]
        start_block(scatter_copy, i, slot, 1)

    @pl.when(i == NB - 1)
    def _():
        wait_scatter(NB - 2, 1 - slot)
        wait_scatter(NB - 1, slot)


def _ffn(blk_lo, blk_hi, n_valid, tok_buf, out_buf, hs, w1, w3, w2, layer):
    def w_spec(shape, which):
        return pl.BlockSpec((None, None) + shape,
                            lambda i, lo, hi, nv, tok, out: (layer, (lo, hi)[which][i], 0, 0))

    return pl.pallas_call(
        _ffn_kernel,
        grid_spec=pltpu.PrefetchScalarGridSpec(
            num_scalar_prefetch=5,
            grid=(NB,),
            in_specs=[pl.BlockSpec(memory_space=pl.ANY),
                      w_spec((D, FF), 0), w_spec((D, FF), 0), w_spec((FF, D), 0),
                      w_spec((D, FF), 1), w_spec((D, FF), 1), w_spec((FF, D), 1)],
            out_specs=pl.BlockSpec(memory_space=pl.ANY),
            scratch_shapes=[pltpu.VMEM((2, BM * XPITCH, LANES), F32), pltpu.VMEM((2, BM * YPITCH, LANES), F32),
                            pltpu.SemaphoreType.DMA((2,)), pltpu.SemaphoreType.DMA((2,))]),
        out_shape=jax.ShapeDtypeStruct(((T + 2 * BM) * YROWS, LANES), F32),
        compiler_params=_params("arbitrary"),
        name="moe_expert_ffn",
    )(blk_lo, blk_hi, n_valid, tok_buf, out_buf, hs, w1, w3, w2, w1, w3, w2)


def _combine_kernel(x_ref, y_ref, slab_ref, mod_ref, fg_ref, o_ref, *, final):
    w_lo = slab_ref[:, 2:3]
    w_hi = slab_ref[:, 3:4]
    y = jnp.concatenate([w_lo * y_ref[pl.ds(s, TR, stride=YROWS), :]
                         + w_hi * y_ref[pl.ds(SUB + s, TR, stride=YROWS), :] for s in range(SUB)], axis=-1)
    x = x_ref[...] + mod_ref[5:6, :] * y
    if final:
        x = x * lax.rsqrt(jnp.mean(x * x, axis=-1, keepdims=True) + EPS) * fg_ref[...]
    o_ref[...] = x


def _combine(x, ys, slab, mods, final_g, final):
    first = T_CTX // TR if final else 0

    def rows(height, width):
        return pl.BlockSpec((height, width), lambda i: (i + first, 0))

    return pl.pallas_call(
        functools.partial(_combine_kernel, final=final),
        grid=(NR - first,),
        in_specs=[rows(TR, D), rows(TR * YROWS, LANES), rows(TR, LANES),
                  _mod_spec(TR, first), _full_spec((1, D))],
        out_specs=_row_spec(D),
        out_shape=jax.ShapeDtypeStruct((T - first * TR, D), F32),
        compiler_params=_params("arbitrary"),
        name="moe_combine",
    )(x, ys, slab, mods, final_g)


def _class_experts():
    lo, hi = [], []
    for g in range(GROUPS):
        for a in range(PER_GROUP):
            for b in range(a + 1, PER_GROUP):
                lo.append(g * PER_GROUP + a)
                hi.append(g * PER_GROUP + b)
    return jnp.array(lo, jnp.int32), jnp.array(hi, jnp.int32)


def _dispatch_plan(slab, counts):
    cls = slab[:, 0].astype(jnp.int32)
    rank = slab[:, 1].astype(jnp.int32)
    counts = counts[0, :NCLS].astype(jnp.int32)
    padded = ((counts + BM - 1) // BM) * BM
    pend = jnp.cumsum(padded)
    pstart = pend - padded
    ids = jnp.arange(NCLS, dtype=jnp.int32)

    def lookup(table, idx):
        return jnp.sum(jnp.where(idx[:, None] == ids[None, :], table[None, :], 0), axis=1)

    dest = lookup(pstart, cls) + rank
    tok_buf = (jnp.arange(P, dtype=jnp.int32) % T).at[dest].set(jnp.arange(T, dtype=jnp.int32))
    blk0 = jnp.arange(NB, dtype=jnp.int32) * BM
    blk_c = jnp.minimum(jnp.sum((pend[None, :] <= blk0[:, None]).astype(jnp.int32), axis=1), NCLS - 1)
    n_valid = jnp.clip(lookup(counts, blk_c) - (blk0 - lookup(pstart, blk_c)), 0, BM)
    cls_lo, cls_hi = _class_experts()
    slot = jnp.arange(P, dtype=jnp.int32)
    spare = T + ((slot // BM) % 2) * BM + slot % BM
    out_buf = jnp.where(slot % BM < jnp.repeat(n_valid, BM), tok_buf, spare)
    return tok_buf, out_buf, lookup(cls_lo, blk_c), lookup(cls_hi, blk_c), n_valid


def _moe(x, g, mods, wr_cat, br_cat, w1, w3, w2, layer, final_g, final):
    hs, slab, counts = _router(x, g, mods, wr_cat, br_cat)
    tok_buf, out_buf, blk_lo, blk_hi, n_valid = _dispatch_plan(slab, counts)
    ys = _ffn(blk_lo, blk_hi, n_valid, tok_buf, out_buf, hs, w1, w3, w2, layer)
    return _combine(x, ys, slab, mods, final_g, final)


def _cast_kernel(w_ref, o_ref):
    o_ref[...] = w_ref[...].astype(BF16)


def _to_bf16(w, rows):
    n, cols = w.shape
    return pl.pallas_call(
        _cast_kernel,
        grid=(n // rows,),
        in_specs=[pl.BlockSpec((rows, cols), lambda i: (i, 0))],
        out_specs=pl.BlockSpec((rows, cols), lambda i: (i, 0)),
        out_shape=jax.ShapeDtypeStruct((n, cols), BF16),
        compiler_params=_params("arbitrary"),
        name="weights_to_bf16",
    )(w)


def _sincos_2d(rows, width, dim):
    quarter = dim // 4
    omega = 1.0 / (POS_BASE ** (jnp.arange(quarter, dtype=F32) / quarter))
    ar = jnp.arange(rows, dtype=F32)[:, None] * omega
    ac = jnp.arange(width, dtype=F32)[:, None] * omega
    row_part = jnp.concatenate([jnp.sin(ar), jnp.cos(ar)], axis=-1)
    col_part = jnp.concatenate([jnp.sin(ac), jnp.cos(ac)], axis=-1)
    row_part = jnp.broadcast_to(row_part[:, None, :], (rows, width, dim // 2))
    col_part = jnp.broadcast_to(col_part[None, :, :], (rows, width, dim // 2))
    return jnp.concatenate([row_part, col_part], axis=-1).reshape(rows * width, dim)


def kernel(x, c, ctx, c_ctx, ada_w, ada_b, norm_g, final_g, conv_pw1_w, conv_pw1_b, conv_dw_w, conv_dw_b, conv_ln_g, conv_ln_b, conv_pw2_w, conv_pw2_b, gla_w_in, gla_gk_up, gla_gk_b, gla_head_g, gla_w_o, moe_wg, moe_bg, moe_wr, moe_br, moe_w1, moe_w3, moe_w2):
    assert x.shape == (BATCH, SEQ, D) and ctx.shape == (BATCH, CTX_LEN, D)
    pos = _sincos_2d(SEQ // GRID_W, GRID_W, D)
    xt = jnp.concatenate([ctx.reshape(T_CTX, D), (x + pos[None]).reshape(T_LAT, D)], axis=0)

    c8 = jnp.concatenate([c_ctx[None], c, jnp.zeros((8 - 1 - BATCH, D), F32)], axis=0)
    mods = _ada_mods(c8, ada_w, ada_b)
    fg = final_g.reshape(1, D)
    w1b = _to_bf16(moe_w1.reshape(-1, FF), 4 * D).reshape(moe_w1.shape)
    w3b = _to_bf16(moe_w3.reshape(-1, FF), 4 * D).reshape(moe_w3.shape)
    w2b = _to_bf16(moe_w2.reshape(-1, D), 4 * FF).reshape(moe_w2.shape)

    for i in range(DEPTH):
        j = i // 2
        g_mix = norm_g[i, 0].reshape(1, D)
        if i % 2 == 0:
            u = _pw1(xt, g_mix, mods[i], conv_pw1_w[j].astype(BF16), conv_pw1_b[j].reshape(1, 2 * D))
            dww = jnp.concatenate([conv_dw_w[j], jnp.zeros((1, D), F32)], axis=0)
            xt = _conv(u, xt, mods[i], dww, conv_dw_b[j].reshape(1, D), conv_ln_g[j].reshape(1, D),
                       conv_ln_b[j].reshape(1, D), conv_pw2_w[j].astype(BF16),
                       conv_pw2_b[j].reshape(1, D))
        else:
            w_in = jnp.pad(gla_w_in[j], ((0, 0), (0, W_IN_PAD - gla_w_in.shape[-1]))).astype(BF16)
            gk = jnp.zeros((LANES, 2 * KEY_DIM), F32)
            gk = gk.at[0:GATE_RANK, 0:KEY_DIM].set(gla_gk_up[j, 0])
            gk = gk.at[GATE_RANK:2 * GATE_RANK, KEY_DIM:].set(gla_gk_up[j, 1])
            gkb = gla_gk_b[j].reshape(1, 2 * KEY_DIM)
            q, k, v, gate, lf, lb = _gla_proj(xt, g_mix, mods[i], w_in, gk, gkb)
            o_f, o_b = _gla_scan(q, k, v, lf, lb)
            xt = _gla_out(o_f, o_b, gate, gla_head_g[j].reshape(1, D), gla_w_o[j].astype(BF16), xt, mods[i])
        wr_cat = jnp.concatenate([moe_wr[i], moe_wg[i],
                                  jnp.zeros((D, LANES - EXPERTS - GROUPS), F32)], axis=1)
        br_cat = jnp.concatenate([moe_br[i], moe_bg[i],
                                  jnp.zeros((LANES - EXPERTS - GROUPS,), F32)]).reshape(1, LANES)
        xt = _moe(xt, norm_g[i, 1].reshape(1, D), mods[i], wr_cat, br_cat,
                  w1b, w3b, w2b, i, fg, final=(i == DEPTH - 1))
    return xt.reshape(BATCH, SEQ, D)
```

```python
import functools

import jax
import jax.numpy as jnp
from jax import lax
from jax.experimental import pallas as pl
from jax.experimental.pallas import tpu as pltpu

F32 = jnp.float32
BF16 = jnp.bfloat16
HIGHEST = lax.Precision.HIGHEST

D = 1024
BATCH = 2
SEQ = 16384
CTX_LEN = 256
DEPTH = 4
GRID_W = 64
POS_BASE = 10000.0
EPS = 1e-6
CONV_WIDTH = 31
CONV_PAD = CONV_WIDTH // 2
HEADS = 4
KEY_DIM = D // 2
HEAD_K = KEY_DIM // HEADS
HEAD_V = D // HEADS
GATE_RANK = 16
GATE_NORM = 16.0
CHUNK = 64
GROUPS = 4
PER_GROUP = 8
EXPERTS = GROUPS * PER_GROUP
FF = D // 2

TM = 256
T_CTX = BATCH * CTX_LEN
T_LAT = BATCH * SEQ
T = T_CTX + T_LAT
NT = T // TM
TR = 512
NR = T // TR
CTX_TILES = T_CTX // TM
LAT_TILES = SEQ // TM
HALO = 16
CONV_ROWS = 128
CONV_WIN = CONV_ROWS + 2 * HALO
SUBLANES = 8
BM = 256
ROW_GROUP = 16
PAIRS = PER_GROUP * (PER_GROUP - 1) // 2
NCLS = GROUPS * PAIRS
NB = T // BM + NCLS
P = NB * BM
LANES = 128
SUB = D // LANES
YROWS = 2 * SUB
XPITCH = 12
YPITCH = 20
W_IN_PAD = 2 * KEY_DIM + 2 * D + LANES
VMEM_LIMIT = 48 * 1024 * 1024

_NT_DIMS = (((1,), (1,)), ((), ()))
_TN_DIMS = (((0,), (0,)), ((), ()))


def _params(*sem):
    return pltpu.CompilerParams(dimension_semantics=sem, vmem_limit_bytes=VMEM_LIMIT)


def _seg(i, rows):
    ctx_tiles = T_CTX // rows
    return jnp.where(i < ctx_tiles, 0, jnp.where(i < ctx_tiles + SEQ // rows, 1, 2))


def _mod_spec(rows=TR, first=0):
    return pl.BlockSpec((None, 6, D), lambda i: (_seg(i + first, rows), 0, 0))


def _row_spec(width, rows=TR):
    return pl.BlockSpec((rows, width), lambda i: (i, 0))


def _full_spec(shape):
    return pl.BlockSpec(shape, lambda i: (0,) * len(shape))


def _normmod(x, g, shift, scale):
    ms = jnp.mean(x * x, axis=-1, keepdims=True)
    return (x * lax.rsqrt(ms + EPS) * g) * (1.0 + scale) + shift


def _sigmoid(x):
    return 1.0 / (1.0 + jnp.exp(-x))


def _split_bf16(a):
    hi = a.astype(BF16)
    return hi, (a - hi.astype(F32)).astype(BF16)


def _dot3(a, w_hi, w_lo):
    a_hi, a_lo = _split_bf16(a)
    return (jnp.dot(a_hi, w_hi, preferred_element_type=F32)
            + jnp.dot(a_lo, w_hi, preferred_element_type=F32)
            + jnp.dot(a_hi, w_lo, preferred_element_type=F32))


def _moe_mix(x_ref, y_ref, slab_ref, mod_ref, rows):
    w_lo = slab_ref[:, 2:3]
    w_hi = slab_ref[:, 3:4]
    y = jnp.concatenate([w_lo * y_ref[pl.ds(s, rows, stride=YROWS), :]
                         + w_hi * y_ref[pl.ds(SUB + s, rows, stride=YROWS), :] for s in range(SUB)], axis=-1)
    return x_ref[...] + mod_ref[5:6, :] * y


def _pending_specs(rows):
    return [pl.BlockSpec((rows * YROWS, LANES), lambda i: (i, 0)), _row_spec(LANES, rows), _mod_spec(rows)]


def _ada_kernel(c_ref, w_ref, b_ref, o_ref):
    c = c_ref[...]
    s = c * _sigmoid(c)
    o_ref[...] = jnp.dot(s, w_ref[...], precision=HIGHEST, preferred_element_type=F32) + b_ref[...]


def _ada_mods(c8, ada_w, ada_b):
    out = pl.pallas_call(
        _ada_kernel,
        grid=(DEPTH, 6),
        in_specs=[pl.BlockSpec((8, D), lambda l, n: (0, 0)),
                  pl.BlockSpec((None, D, D), lambda l, n: (l, 0, n)),
                  pl.BlockSpec((None, 1, D), lambda l, n: (l, 0, n))],
        out_specs=pl.BlockSpec((None, 8, D), lambda l, n: (l, 0, n)),
        out_shape=jax.ShapeDtypeStruct((DEPTH, 8, 6 * D), F32),
        compiler_params=_params("arbitrary", "arbitrary"),
        name="ada_mods",
    )(c8, ada_w, ada_b.reshape(DEPTH, 1, 6 * D))
    return out[:, :3].reshape(DEPTH, 3, 6, D)


def _pw1_kernel(*refs, pending):
    if pending:
        x_ref, y_ref, slab_ref, modp_ref, g_ref, mod_ref, w_ref, b_ref, xo_ref, u_ref = refs
        x = _moe_mix(x_ref, y_ref, slab_ref, modp_ref, TR)
        xo_ref[...] = x
    else:
        x_ref, g_ref, mod_ref, w_ref, b_ref, u_ref = refs
        x = x_ref[...]
    h = _normmod(x, g_ref[...], mod_ref[0:1, :], mod_ref[1:2, :]).astype(BF16)
    a = jnp.dot(h, w_ref[:, :D], preferred_element_type=F32) + b_ref[:, :D]
    g = jnp.dot(h, w_ref[:, D:], preferred_element_type=F32) + b_ref[:, D:]
    u_ref[...] = a * _sigmoid(g)


def _pw1(x, pending, g, mods, w, b):
    row = jax.ShapeDtypeStruct((T, D), F32)
    out = pl.pallas_call(
        functools.partial(_pw1_kernel, pending=pending is not None),
        grid=(NR,),
        in_specs=[_row_spec(D)] + (_pending_specs(TR) if pending else [])
        + [_full_spec((1, D)), _mod_spec(), _full_spec((D, 2 * D)), _full_spec((1, 2 * D))],
        out_specs=[_row_spec(D)] * 2 if pending else _row_spec(D),
        out_shape=[row, row] if pending else row,
        compiler_params=_params("arbitrary"),
        name="conv_pw1_glu",
    )(x, *(pending or ()), g, mods, w, b)
    return out if pending else (x, out)


def _conv_kernel(prev_ref, cur_ref, next_ref, x_ref, mod_ref, dww_ref, dwb_ref, lng_ref, lnb_ref,
                 w_ref, b_ref, o_ref, ext_ref, conv_ref):
    i = pl.program_id(0)
    lat0 = CTX_TILES
    lat1 = CTX_TILES + LAT_TILES
    is_start = (i < CTX_TILES) | (i == lat0) | (i == lat1)
    is_end = (i < CTX_TILES) | (i == lat1 - 1) | (i == NT - 1)
    ext_ref[0:HALO, :] = jnp.where(is_start, 0.0, prev_ref[...])
    ext_ref[HALO:HALO + TM, :] = cur_ref[...]
    ext_ref[HALO + TM:, :] = jnp.where(is_end, 0.0, next_ref[...])

    def lane_block(j, carry):
        lo = pl.multiple_of(j * LANES, LANES)
        for rb in range(TM // CONV_ROWS):
            r0 = rb * CONV_ROWS
            win = ext_ref[r0:r0 + CONV_WIN, pl.ds(lo, LANES)]
            acc = jnp.broadcast_to(dwb_ref[:, pl.ds(lo, LANES)], (CONV_ROWS, LANES))
            for rho in range(SUBLANES):
                shifted = win if rho == 0 else pltpu.roll(win, shift=CONV_WIN - rho, axis=0)
                for k in range(CONV_WIDTH):
                    off = k + HALO - CONV_PAD - rho
                    if off % SUBLANES == 0:
                        acc = acc + dww_ref[k:k + 1, pl.ds(lo, LANES)] * shifted[off:off + CONV_ROWS]
            conv_ref[r0:r0 + CONV_ROWS, pl.ds(lo, LANES)] = acc
        return carry

    lax.fori_loop(0, D // LANES, lane_block, 0)
    v = conv_ref[...]
    mu = jnp.mean(v, axis=-1, keepdims=True)
    vc = v - mu
    var = jnp.mean(vc * vc, axis=-1, keepdims=True)
    y = vc * lax.rsqrt(var + EPS) * lng_ref[...] + lnb_ref[...]
    y = (y * _sigmoid(y)).astype(BF16)
    out = jnp.dot(y, w_ref[...], preferred_element_type=F32) + b_ref[...]
    o_ref[...] = x_ref[...] + mod_ref[2:3, :] * out


def _conv(u, x, mods, dww, dwb, lng, lnb, w, b):
    per = TM // HALO
    return pl.pallas_call(
        _conv_kernel,
        grid=(NT,),
        in_specs=[pl.BlockSpec((HALO, D), lambda i: (jnp.maximum(i * per - 1, 0), 0)),
                  _row_spec(D, TM),
                  pl.BlockSpec((HALO, D), lambda i: (jnp.minimum((i + 1) * per, T // HALO - 1), 0)),
                  _row_spec(D, TM), _mod_spec(TM),
                  _full_spec((CONV_WIDTH + 1, D)), _full_spec((1, D)), _full_spec((1, D)),
                  _full_spec((1, D)), _full_spec((D, D)), _full_spec((1, D))],
        out_specs=_row_spec(D, TM),
        out_shape=jax.ShapeDtypeStruct((T, D), F32),
        scratch_shapes=[pltpu.VMEM((TM + 2 * HALO, D), F32), pltpu.VMEM((TM, D), F32)],
        compiler_params=_params("arbitrary"),
        name="conv_dw_ln_pw2",
    )(u, u, u, x, mods, dww, dwb, lng, lnb, w, b)


def _log_sigmoid(z):
    return jnp.minimum(z, 0.0) - jnp.log(1.0 + jnp.exp(-jnp.abs(z)))


def _gla_proj_kernel(*refs, pending):
    if pending:
        x_ref, y_ref, slab_ref, modp_ref = refs[:4]
        refs = refs[4:]
        xo_ref = refs[6]
        refs = refs[:6] + refs[7:]
        x = _moe_mix(x_ref, y_ref, slab_ref, modp_ref, TM)
        xo_ref[...] = x
    else:
        x = refs[0][...]
        refs = refs[1:]
    g_ref, mod_ref, w_ref, gkh_ref, gkl_ref, gkb_ref, q_ref, k_ref, v_ref, gate_ref, lf_ref, lb_ref = refs
    h = _normmod(x, g_ref[...], mod_ref[0:1, :], mod_ref[1:2, :]).astype(BF16)
    kd = KEY_DIM
    q_ref[...] = jnp.dot(h, w_ref[:, 0:kd], preferred_element_type=F32) * (HEAD_K ** -0.5)
    k_ref[...] = jnp.dot(h, w_ref[:, kd:2 * kd], preferred_element_type=F32)
    v_ref[...] = jnp.dot(h, w_ref[:, 2 * kd:2 * kd + D], preferred_element_type=F32).astype(BF16)
    gate_ref[...] = jnp.dot(h, w_ref[:, 2 * kd + D:2 * kd + 2 * D], preferred_element_type=F32)
    codes = jnp.dot(h, w_ref[:, 2 * kd + 2 * D:], preferred_element_type=F32)
    z = _dot3(codes, gkh_ref[...], gkl_ref[...]) + gkb_ref[...]
    ls = _log_sigmoid(z) * (1.0 / GATE_NORM)
    lf_ref[...] = ls[:, :kd]
    lb_ref[...] = ls[:, kd:]


def _gla_proj(x, pending, g, mods, w, gk, gkb):
    gk_hi, gk_lo = _split_bf16(gk)
    shapes = [jax.ShapeDtypeStruct((T, KEY_DIM), F32), jax.ShapeDtypeStruct((T, KEY_DIM), F32),
              jax.ShapeDtypeStruct((T, D), BF16), jax.ShapeDtypeStruct((T, D), F32),
              jax.ShapeDtypeStruct((T, KEY_DIM), F32), jax.ShapeDtypeStruct((T, KEY_DIM), F32)]
    out_specs = [_row_spec(KEY_DIM, TM), _row_spec(KEY_DIM, TM), _row_spec(D, TM), _row_spec(D, TM),
                 _row_spec(KEY_DIM, TM), _row_spec(KEY_DIM, TM)]
    if pending:
        shapes = [jax.ShapeDtypeStruct((T, D), F32)] + shapes
        out_specs = [_row_spec(D, TM)] + out_specs
    out = pl.pallas_call(
        functools.partial(_gla_proj_kernel, pending=pending is not None),
        grid=(NT,),
        in_specs=[_row_spec(D, TM)] + (_pending_specs(TM) if pending else [])
        + [_full_spec((1, D)), _mod_spec(TM), _full_spec((D, W_IN_PAD)),
           _full_spec((LANES, 2 * KEY_DIM)), _full_spec((LANES, 2 * KEY_DIM)),
           _full_spec((1, 2 * KEY_DIM))],
        out_specs=out_specs,
        out_shape=shapes,
        compiler_params=_params("arbitrary"),
        name="gla_proj",
    )(x, *(pending or ()), g, mods, w, gk_hi, gk_lo, gkb)
    return (out[0], out[1:]) if pending else (x, out)


def _scan_tile(q_ref, k_ref, v_ref, lg_ref, o_ref, st_ref, cum_ref, reverse):
    r = lax.broadcasted_iota(jnp.int32, (TM, TM), 0)
    c = lax.broadcasted_iota(jnp.int32, (TM, TM), 1)
    shift = CHUNK.bit_length() - 1
    same = (r >> shift) == (c >> shift)
    tri = jnp.where(same & ((c >= r) if reverse else (c <= r)), 1.0, 0.0).astype(BF16)
    lg = lg_ref[...]
    hi = lg.astype(BF16)
    r1 = lg - hi.astype(F32)
    mid = r1.astype(BF16)
    lo = (r1 - mid.astype(F32)).astype(BF16)
    cum_ref[...] = (jnp.dot(tri, hi, preferred_element_type=F32)
                    + jnp.dot(tri, mid, preferred_element_type=F32)
                    + jnp.dot(tri, lo, preferred_element_type=F32))

    rr = lax.broadcasted_iota(jnp.int32, (CHUNK, CHUNK), 0)
    cc = lax.broadcasted_iota(jnp.int32, (CHUNK, CHUNK), 1)
    mask = (cc >= rr) if reverse else (cc <= rr)
    i_last = 0 if reverse else CHUNK - 1
    i_ref = CHUNK // 2 - 1 if reverse else CHUNK // 2
    chunks = range(TM // CHUNK)
    for ci in (reversed(chunks) if reverse else chunks):
        r0 = ci * CHUNK
        for h in range(HEADS):
            k0, v0 = h * HEAD_K, h * HEAD_V
            cm = cum_ref[r0:r0 + CHUNK, k0:k0 + HEAD_K]
            last = cum_ref[r0 + i_last:r0 + i_last + 1, k0:k0 + HEAD_K]
            ref = cum_ref[r0 + i_ref:r0 + i_ref + 1, k0:k0 + HEAD_K]
            qc = q_ref[r0:r0 + CHUNK, k0:k0 + HEAD_K]
            kc = k_ref[r0:r0 + CHUNK, k0:k0 + HEAD_K]
            vc = v_ref[r0:r0 + CHUNK, v0:v0 + HEAD_V]
            qr = (qc * jnp.exp(cm - ref)).astype(BF16)
            kr = (kc * jnp.exp(ref - cm)).astype(BF16)
            att = lax.dot_general(qr, kr, _NT_DIMS, preferred_element_type=F32)
            att = jnp.where(mask, att, 0.0).astype(BF16)
            qd = (qc * jnp.exp(cm)).astype(BF16)
            st = st_ref[h]
            o = (jnp.dot(att, vc, preferred_element_type=F32)
                 + lax.dot_general(qd, st.astype(BF16), _NT_DIMS, preferred_element_type=F32))
            kdec = (kc * jnp.exp(last - cm)).astype(BF16)
            st_ref[h] = st * jnp.exp(last) + lax.dot_general(vc, kdec, _TN_DIMS,
                                                             preferred_element_type=F32)
            o_ref[r0:r0 + CHUNK, v0:v0 + HEAD_V] = o


def _gla_scan_kernel(qf_ref, kf_ref, vf_ref, lf_ref, qb_ref, kb_ref, vb_ref, lb_ref,
                     of_ref, ob_ref, st_ref, cum_ref):
    @pl.when(pl.program_id(1) == 0)
    def _():
        st_ref[...] = jnp.zeros_like(st_ref)

    _scan_tile(qf_ref, kf_ref, vf_ref, lf_ref, of_ref, st_ref.at[0], cum_ref.at[0], False)
    _scan_tile(qb_ref, kb_ref, vb_ref, lb_ref, ob_ref, st_ref.at[1], cum_ref.at[1], True)


def _gla_scan(q, k, v, lf, lb):
    def tile_f(b, j):
        return jnp.where(j == 0, b, CTX_TILES + LAT_TILES * b + j - 1)

    def tile_b(b, j):
        return jnp.where(j == 0, b, CTX_TILES + LAT_TILES * (b + 1) - j)

    def rows(width, tile):
        return pl.BlockSpec((TM, width), lambda b, j: (tile(b, j), 0))

    def scan_specs(tile):
        return [rows(KEY_DIM, tile), rows(KEY_DIM, tile), rows(D, tile), rows(KEY_DIM, tile)]

    return pl.pallas_call(
        _gla_scan_kernel,
        grid=(BATCH, 1 + LAT_TILES),
        in_specs=scan_specs(tile_f) + scan_specs(tile_b),
        out_specs=[rows(D, tile_f), rows(D, tile_b)],
        out_shape=[jax.ShapeDtypeStruct((T, D), F32), jax.ShapeDtypeStruct((T, D), F32)],
        scratch_shapes=[pltpu.VMEM((2, HEADS, HEAD_V, HEAD_K), F32), pltpu.VMEM((2, TM, KEY_DIM), F32)],
        compiler_params=_params("arbitrary", "arbitrary"),
        name="gla_scan",
    )(q, k, v, lf, q, k, v, lb)


def _gla_out_kernel(of_ref, ob_ref, gate_ref, hg_ref, w_ref, x_ref, mod_ref, o_ref):
    heads = []
    for h in range(HEADS):
        v0 = h * HEAD_V
        o = ob_ref[:, v0:v0 + HEAD_V] + of_ref[:, v0:v0 + HEAD_V]
        on = o * lax.rsqrt(jnp.mean(o * o, axis=-1, keepdims=True) + EPS)
        on = on * hg_ref[:, v0:v0 + HEAD_V]
        gt = gate_ref[:, v0:v0 + HEAD_V]
        heads.append((on * (gt * _sigmoid(gt))).astype(BF16))
    y = jnp.dot(jnp.concatenate(heads, axis=-1), w_ref[...], preferred_element_type=F32)
    o_ref[...] = x_ref[...] + mod_ref[2:3, :] * y


def _gla_out(o_f, o_b, gate, head_g, w, x, mods):
    return pl.pallas_call(
        _gla_out_kernel,
        grid=(NR,),
        in_specs=[_row_spec(D), _row_spec(D), _row_spec(D), _full_spec((1, D)), _full_spec((D, D)),
                  _row_spec(D), _mod_spec()],
        out_specs=_row_spec(D),
        out_shape=jax.ShapeDtypeStruct((T, D), F32),
        compiler_params=_params("arbitrary"),
        name="gla_out_proj",
    )(o_f, o_b, gate, head_g, w, x, mods)


def _router_kernel(x_ref, g_ref, mod_ref, wh_ref, wl_ref, b_ref, h_ref, slab_ref, counts_ref, cnt_ref):
    h = _normmod(x_ref[...], g_ref[...], mod_ref[3:4, :], mod_ref[4:5, :])
    for s in range(SUB):
        h_ref[pl.ds(s, TR, stride=SUB), :] = h[:, s * LANES:(s + 1) * LANES]
    lg = _dot3(h, wh_ref[...], wl_ref[...]) + b_ref[...]
    lane = lax.broadcasted_iota(jnp.int32, (TR, LANES), 1).astype(F32)
    neg = -jnp.inf
    far = float(LANES)
    is_group = (lane >= EXPERTS) & (lane < EXPERTS + GROUPS)
    glog = jnp.where(is_group, lg, neg)
    gmax = jnp.max(glog, axis=-1, keepdims=True)
    gidx = jnp.min(jnp.where(glog == gmax, lane, far), axis=-1, keepdims=True) - EXPERTS
    g_w = 1.0 / jnp.sum(jnp.exp(glog - gmax), axis=-1, keepdims=True)
    lo = gidx * PER_GROUP
    vals = jnp.where((lane >= lo) & (lane < lo + PER_GROUP), lg, neg)
    m1 = jnp.max(vals, axis=-1, keepdims=True)
    i1 = jnp.min(jnp.where(vals == m1, lane, far), axis=-1, keepdims=True)
    vals2 = jnp.where(lane == i1, neg, vals)
    m2 = jnp.max(vals2, axis=-1, keepdims=True)
    i2 = jnp.min(jnp.where(vals2 == m2, lane, far), axis=-1, keepdims=True)
    e = jnp.exp(m2 - m1)
    w1 = g_w / (1.0 + e)
    w2 = g_w * e / (1.0 + e)
    first_lo = i1 < i2
    a = jnp.minimum(i1, i2) - lo
    b = jnp.maximum(i1, i2) - lo
    cls = gidx * PAIRS + a * (PER_GROUP - 1) - a * (a - 1.0) * 0.5 + (b - a - 1.0)
    w_lo = jnp.where(first_lo, w1, w2)
    w_hi = jnp.where(first_lo, w2, w1)
    @pl.when(pl.program_id(0) == 0)
    def _():
        cnt_ref[...] = jnp.zeros_like(cnt_ref)

    onehot = jnp.where(lane == cls, 1.0, 0.0)
    r = lax.broadcasted_iota(jnp.int32, (TR, TR), 0)
    c = lax.broadcasted_iota(jnp.int32, (TR, TR), 1)
    before = jnp.where(c < r, 1.0, 0.0).astype(BF16)
    prefix = jnp.dot(before, onehot.astype(BF16), preferred_element_type=F32)
    base = cnt_ref[...]
    rank = jnp.sum(onehot * (prefix + base), axis=-1, keepdims=True)
    total = base + jnp.sum(onehot, axis=0, keepdims=True)
    cnt_ref[...] = total
    counts_ref[...] = total
    slab_ref[...] = jnp.where(lane == 0, cls, jnp.where(lane == 1, rank, jnp.where(
        lane == 2, w_lo, jnp.where(lane == 3, w_hi, 0.0))))


def _router(x, g, mods, w, b):
    w_hi, w_lo = _split_bf16(w)
    return pl.pallas_call(
        _router_kernel,
        grid=(NR,),
        in_specs=[_row_spec(D), _full_spec((1, D)), _mod_spec(), _full_spec((D, LANES)),
                  _full_spec((D, LANES)), _full_spec((1, LANES))],
        out_specs=[pl.BlockSpec((TR * SUB, LANES), lambda i: (i, 0)), _row_spec(LANES),
                   _full_spec((1, LANES))],
        out_shape=[jax.ShapeDtypeStruct((T * SUB, LANES), F32), jax.ShapeDtypeStruct((T, LANES), F32),
                   jax.ShapeDtypeStruct((1, LANES), F32)],
        scratch_shapes=[pltpu.VMEM((1, LANES), F32)],
        compiler_params=_params("arbitrary"),
        name="moe_router",
    )(x, g, mods, w_hi, w_lo, b)


def _ffn_kernel(lo_ref, hi_ref, nv_ref, tok_ref, out_ref, hs_ref, w1l_ref, w3l_ref, w2l_ref,
                w1h_ref, w3h_ref, w2h_ref, ys_ref, xg, yb, gsem, ssem):
    i = pl.program_id(0)
    slot = i % 2

    def groups(blk):
        return (nv_ref[jnp.clip(blk, 0, NB - 1)] + ROW_GROUP - 1) // ROW_GROUP

    def gather_copy(blk, sl, r):
        tok = tok_ref[blk * BM + r]
        return pltpu.make_async_copy(hs_ref.at[pl.ds(pl.multiple_of(tok * SUB, SUB), SUB)],
                                     xg.at[sl, pl.ds(r * XPITCH, SUB)], gsem.at[sl])

    def scatter_copy(blk, sl, r):
        row = out_ref[blk * BM + r]
        return pltpu.make_async_copy(yb.at[sl, pl.ds(r * YPITCH, YROWS)],
                                     ys_ref.at[pl.ds(pl.multiple_of(row * YROWS, YROWS), YROWS)],
                                     ssem.at[sl])

    def start_block(copy, blk, sl, priority):
        def group(gi, carry):
            for u in range(ROW_GROUP):
                copy(blk, sl, gi * ROW_GROUP + u).start(priority=priority)
            return carry
        lax.fori_loop(0, groups(blk), group, 0)

    def wait_gather(blk, sl):
        n = groups(blk) * (ROW_GROUP * SUB)

        @pl.when(n > 0)
        def _():
            pltpu.make_async_copy(hs_ref.at[pl.ds(0, n)], xg.at[sl, pl.ds(0, n)], gsem.at[sl]).wait()

    def wait_scatter(blk, sl):
        n = groups(blk) * (ROW_GROUP * YROWS)

        @pl.when(n > 0)
        def _():
            pltpu.make_async_copy(yb.at[sl, pl.ds(0, n)], ys_ref.at[pl.ds(0, n)], ssem.at[sl]).wait()

    @pl.when(i == 0)
    def _():
        xg[...] = jnp.zeros_like(xg)
        yb[...] = jnp.zeros_like(yb)
        for sl in range(2):
            spare = pltpu.make_async_copy(yb.at[sl, pl.ds(0, BM * YROWS)],
                                          ys_ref.at[pl.ds((T + sl * BM) * YROWS, BM * YROWS)], ssem.at[sl])
            spare.start()
            spare.wait()
        start_block(gather_copy, 0, 0, 0)

    @pl.when(i + 1 < NB)
    def _():
        start_block(gather_copy, i + 1, 1 - slot, 0)

    wait_gather(i, slot)

    @pl.when(i >= 2)
    def _():
        wait_scatter(i - 2, slot)

    def expert(xb, w1_ref, w3_ref, w2_ref):
        a = jnp.dot(xb, w1_ref[...], preferred_element_type=F32)
        b = jnp.dot(xb, w3_ref[...], preferred_element_type=F32)
        hh = (a * _sigmoid(a) * b).astype(BF16)
        return jnp.dot(hh, w2_ref[...], preferred_element_type=F32)

    @pl.when(nv_ref[i] > 0)
    def _():
        xb = jnp.concatenate([xg[slot, pl.ds(s, BM, stride=XPITCH), :] for s in range(SUB)],
                             axis=-1).astype(BF16)
        y_lo = expert(xb, w1l_ref, w3l_ref, w2l_ref)
        for s in range(SUB):
            yb[slot, pl.ds(s, BM, stride=YPITCH), :] = y_lo[:, s * LANES:(s + 1) * LANES]
        y_hi = expert(xb, w1h_ref, w3h_ref, w2h_ref)
        for s in range(SUB):
            yb[slot, pl.ds(SUB + s, BM, stride=YPITCH), :] = y_hi[:, s * LANES:(s + 1) * LANES]
        start_block(scatter_copy, i, slot, 1)

    @pl.when(i == NB - 1)
    def _():
        wait_scatter(NB - 2, 1 - slot)
        wait_scatter(NB - 1, slot)


def _ffn(blk_lo, blk_hi, n_valid, tok_buf, out_buf, hs, w1, w3, w2, layer):
    def w_spec(shape, which):
        return pl.BlockSpec((None, None) + shape,
                            lambda i, lo, hi, nv, tok, out: (layer, (lo, hi)[which][i], 0, 0))

    return pl.pallas_call(
        _ffn_kernel,
        grid_spec=pltpu.PrefetchScalarGridSpec(
            num_scalar_prefetch=5,
            grid=(NB,),
            in_specs=[pl.BlockSpec(memory_space=pl.ANY),
                      w_spec((D, FF), 0), w_spec((D, FF), 0), w_spec((FF, D), 0),
                      w_spec((D, FF), 1), w_spec((D, FF), 1), w_spec((FF, D), 1)],
            out_specs=pl.BlockSpec(memory_space=pl.ANY),
            scratch_shapes=[pltpu.VMEM((2, BM * XPITCH, LANES), F32), pltpu.VMEM((2, BM * YPITCH, LANES), F32),
                            pltpu.SemaphoreType.DMA((2,)), pltpu.SemaphoreType.DMA((2,))]),
        out_shape=jax.ShapeDtypeStruct(((T + 2 * BM) * YROWS, LANES), F32),
        compiler_params=_params("arbitrary"),
        name="moe_expert_ffn",
    )(blk_lo, blk_hi, n_valid, tok_buf, out_buf, hs, w1, w3, w2, w1, w3, w2)


def _combine_kernel(x_ref, y_ref, slab_ref, mod_ref, fg_ref, o_ref):
    x = _moe_mix(x_ref, y_ref, slab_ref, mod_ref, TR)
    o_ref[...] = x * lax.rsqrt(jnp.mean(x * x, axis=-1, keepdims=True) + EPS) * fg_ref[...]


def _final_combine(x, ys, slab, mods, final_g):
    first = T_CTX // TR

    def rows(height, width):
        return pl.BlockSpec((height, width), lambda i: (i + first, 0))

    return pl.pallas_call(
        _combine_kernel,
        grid=(NR - first,),
        in_specs=[rows(TR, D), rows(TR * YROWS, LANES), rows(TR, LANES),
                  _mod_spec(TR, first), _full_spec((1, D))],
        out_specs=_row_spec(D),
        out_shape=jax.ShapeDtypeStruct((T - first * TR, D), F32),
        compiler_params=_params("arbitrary"),
        name="moe_combine",
    )(x, ys, slab, mods, final_g)


def _class_experts():
    lo, hi = [], []
    for g in range(GROUPS):
        for a in range(PER_GROUP):
            for b in range(a + 1, PER_GROUP):
                lo.append(g * PER_GROUP + a)
                hi.append(g * PER_GROUP + b)
    return jnp.array(lo, jnp.int32), jnp.array(hi, jnp.int32)


def _dispatch_plan(slab, counts):
    cls = slab[:, 0].astype(jnp.int32)
    rank = slab[:, 1].astype(jnp.int32)
    counts = counts[0, :NCLS].astype(jnp.int32)
    padded = ((counts + BM - 1) // BM) * BM
    pend = jnp.cumsum(padded)
    pstart = pend - padded
    ids = jnp.arange(NCLS, dtype=jnp.int32)

    def lookup(table, idx):
        return jnp.sum(jnp.where(idx[:, None] == ids[None, :], table[None, :], 0), axis=1)

    dest = lookup(pstart, cls) + rank
    tok_buf = (jnp.arange(P, dtype=jnp.int32) % T).at[dest].set(jnp.arange(T, dtype=jnp.int32))
    blk0 = jnp.arange(NB, dtype=jnp.int32) * BM
    blk_c = jnp.minimum(jnp.sum((pend[None, :] <= blk0[:, None]).astype(jnp.int32), axis=1), NCLS - 1)
    n_valid = jnp.clip(lookup(counts, blk_c) - (blk0 - lookup(pstart, blk_c)), 0, BM)
    cls_lo, cls_hi = _class_experts()
    slot = jnp.arange(P, dtype=jnp.int32)
    spare = T + ((slot // BM) % 2) * BM + slot % BM
    out_buf = jnp.where(slot % BM < jnp.repeat(n_valid, BM), tok_buf, spare)
    return tok_buf, out_buf, lookup(cls_lo, blk_c), lookup(cls_hi, blk_c), n_valid


def _moe(x, g, mods, wr_cat, br_cat, w1, w3, w2, layer):
    hs, slab, counts = _router(x, g, mods, wr_cat, br_cat)
    tok_buf, out_buf, blk_lo, blk_hi, n_valid = _dispatch_plan(slab, counts)
    ys = _ffn(blk_lo, blk_hi, n_valid, tok_buf, out_buf, hs, w1, w3, w2, layer)
    return ys, slab


def _cast_kernel(w_ref, o_ref):
    o_ref[...] = w_ref[...].astype(BF16)


def _to_bf16(w, rows):
    n, cols = w.shape
    return pl.pallas_call(
        _cast_kernel,
        grid=(n // rows,),
        in_specs=[pl.BlockSpec((rows, cols), lambda i: (i, 0))],
        out_specs=pl.BlockSpec((rows, cols), lambda i: (i, 0)),
        out_shape=jax.ShapeDtypeStruct((n, cols), BF16),
        compiler_params=_params("arbitrary"),
        name="weights_to_bf16",
    )(w)


def _sincos_2d(rows, width, dim):
    quarter = dim // 4
    omega = 1.0 / (POS_BASE ** (jnp.arange(quarter, dtype=F32) / quarter))
    ar = jnp.arange(rows, dtype=F32)[:, None] * omega
    ac = jnp.arange(width, dtype=F32)[:, None] * omega
    row_part = jnp.concatenate([jnp.sin(ar), jnp.cos(ar)], axis=-1)
    col_part = jnp.concatenate([jnp.sin(ac), jnp.cos(ac)], axis=-1)
    row_part = jnp.broadcast_to(row_part[:, None, :], (rows, width, dim // 2))
    col_part = jnp.broadcast_to(col_part[None, :, :], (rows, width, dim // 2))
    return jnp.concatenate([row_part, col_part], axis=-1).reshape(rows * width, dim)


def kernel(x, c, ctx, c_ctx, ada_w, ada_b, norm_g, final_g, conv_pw1_w, conv_pw1_b, conv_dw_w, conv_dw_b, conv_ln_g, conv_ln_b, conv_pw2_w, conv_pw2_b, gla_w_in, gla_gk_up, gla_gk_b, gla_head_g, gla_w_o, moe_wg, moe_bg, moe_wr, moe_br, moe_w1, moe_w3, moe_w2):
    assert x.shape == (BATCH, SEQ, D) and ctx.shape == (BATCH, CTX_LEN, D)
    pos = _sincos_2d(SEQ // GRID_W, GRID_W, D)
    xt = jnp.concatenate([ctx.reshape(T_CTX, D), (x + pos[None]).reshape(T_LAT, D)], axis=0)

    c8 = jnp.concatenate([c_ctx[None], c, jnp.zeros((8 - 1 - BATCH, D), F32)], axis=0)
    mods = _ada_mods(c8, ada_w, ada_b)
    fg = final_g.reshape(1, D)
    w1b = _to_bf16(moe_w1.reshape(-1, FF), 4 * D).reshape(moe_w1.shape)
    w3b = _to_bf16(moe_w3.reshape(-1, FF), 4 * D).reshape(moe_w3.shape)
    w2b = _to_bf16(moe_w2.reshape(-1, D), 4 * FF).reshape(moe_w2.shape)

    pending = None
    for i in range(DEPTH):
        j = i // 2
        g_mix = norm_g[i, 0].reshape(1, D)
        if i % 2 == 0:
            xt, u = _pw1(xt, pending, g_mix, mods[i], conv_pw1_w[j].astype(BF16),
                         conv_pw1_b[j].reshape(1, 2 * D))
            dww = jnp.concatenate([conv_dw_w[j], jnp.zeros((1, D), F32)], axis=0)
            xt = _conv(u, xt, mods[i], dww, conv_dw_b[j].reshape(1, D), conv_ln_g[j].reshape(1, D),
                       conv_ln_b[j].reshape(1, D), conv_pw2_w[j].astype(BF16),
                       conv_pw2_b[j].reshape(1, D))
        else:
            w_in = jnp.pad(gla_w_in[j], ((0, 0), (0, W_IN_PAD - gla_w_in.shape[-1]))).astype(BF16)
            gk = jnp.zeros((LANES, 2 * KEY_DIM), F32)
            gk = gk.at[0:GATE_RANK, 0:KEY_DIM].set(gla_gk_up[j, 0])
            gk = gk.at[GATE_RANK:2 * GATE_RANK, KEY_DIM:].set(gla_gk_up[j, 1])
            gkb = gla_gk_b[j].reshape(1, 2 * KEY_DIM)
            xt, (q, k, v, gate, lf, lb) = _gla_proj(xt, pending, g_mix, mods[i], w_in, gk, gkb)
            o_f, o_b = _gla_scan(q, k, v, lf, lb)
            xt = _gla_out(o_f, o_b, gate, gla_head_g[j].reshape(1, D), gla_w_o[j].astype(BF16), xt, mods[i])
        wr_cat = jnp.concatenate([moe_wr[i], moe_wg[i],
                                  jnp.zeros((D, LANES - EXPERTS - GROUPS), F32)], axis=1)
        br_cat = jnp.concatenate([moe_br[i], moe_bg[i],
                                  jnp.zeros((LANES - EXPERTS - GROUPS,), F32)]).reshape(1, LANES)
        ys, slab = _moe(xt, norm_g[i, 1].reshape(1, D), mods[i], wr_cat, br_cat, w1b, w3b, w2b, i)
        pending = (ys, slab, mods[i])
    return _final_combine(xt, *pending, fg).reshape(BATCH, SEQ, D)
```

```python
import functools

import jax
import jax.numpy as jnp
from jax import lax
from jax.experimental import pallas as pl
from jax.experimental.pallas import tpu as pltpu

F32 = jnp.float32
BF16 = jnp.bfloat16
HIGHEST = lax.Precision.HIGHEST

D = 1024
BATCH = 2
SEQ = 16384
CTX_LEN = 256
DEPTH = 4
GRID_W = 64
POS_BASE = 10000.0
EPS = 1e-6
CONV_WIDTH = 31
CONV_PAD = CONV_WIDTH // 2
HEADS = 4
KEY_DIM = D // 2
HEAD_K = KEY_DIM // HEADS
HEAD_V = D // HEADS
GATE_RANK = 16
GATE_NORM = 16.0
CHUNK = 64
GROUPS = 4
PER_GROUP = 8
EXPERTS = GROUPS * PER_GROUP
FF = D // 2

TM = 256
T_CTX = BATCH * CTX_LEN
T_LAT = BATCH * SEQ
T = T_CTX + T_LAT
NT = T // TM
TR = 512
NR = T // TR
CTX_TILES = T_CTX // TM
LAT_TILES = SEQ // TM
HALO = 16
CONV_ROWS = 128
CONV_WIN = CONV_ROWS + 2 * HALO
SUBLANES = 8
BM = 256
ROW_GROUP = 16
PAIRS = PER_GROUP * (PER_GROUP - 1) // 2
NCLS = GROUPS * PAIRS
NB = T // BM + NCLS
P = NB * BM
LANES = 128
SUB = D // LANES
YROWS = 2 * SUB
XPITCH = 12
YPITCH = 20
W_IN_PAD = 2 * KEY_DIM + 2 * D + LANES
VMEM_LIMIT = 48 * 1024 * 1024

_NT_DIMS = (((1,), (1,)), ((), ()))
_TN_DIMS = (((0,), (0,)), ((), ()))


def _params(*sem):
    return pltpu.CompilerParams(dimension_semantics=sem, vmem_limit_bytes=VMEM_LIMIT)


def _seg(i, rows):
    ctx_tiles = T_CTX // rows
    return jnp.where(i < ctx_tiles, 0, jnp.where(i < ctx_tiles + SEQ // rows, 1, 2))


def _mod_spec(rows=TR, first=0):
    return pl.BlockSpec((None, 6, D), lambda i: (_seg(i + first, rows), 0, 0))


def _row_spec(width, rows=TR):
    return pl.BlockSpec((rows, width), lambda i: (i, 0))


def _full_spec(shape):
    return pl.BlockSpec(shape, lambda i: (0,) * len(shape))


def _normmod(x, g, shift, scale):
    ms = jnp.mean(x * x, axis=-1, keepdims=True)
    return (x * lax.rsqrt(ms + EPS) * g) * (1.0 + scale) + shift


def _sigmoid(x):
    return 1.0 / (1.0 + jnp.exp(-x))


def _split_bf16(a):
    hi = a.astype(BF16)
    return hi, (a - hi.astype(F32)).astype(BF16)


def _dot3(a, w_hi, w_lo):
    a_hi, a_lo = _split_bf16(a)
    return (jnp.dot(a_hi, w_hi, preferred_element_type=F32)
            + jnp.dot(a_lo, w_hi, preferred_element_type=F32)
            + jnp.dot(a_hi, w_lo, preferred_element_type=F32))


def _moe_mix(x_ref, y_ref, slab_ref, mod_ref, rows):
    w_lo = slab_ref[:, 2:3]
    w_hi = slab_ref[:, 3:4]
    y = jnp.concatenate([w_lo * y_ref[pl.ds(s, rows, stride=YROWS), :]
                         + w_hi * y_ref[pl.ds(SUB + s, rows, stride=YROWS), :] for s in range(SUB)], axis=-1)
    return x_ref[...] + mod_ref[5:6, :] * y


def _pending_specs(rows):
    return [pl.BlockSpec((rows * YROWS, LANES), lambda i: (i, 0)), _row_spec(LANES, rows), _mod_spec(rows)]


def _ada_kernel(c_ref, w_ref, b_ref, o_ref):
    c = c_ref[...]
    s = c * _sigmoid(c)
    o_ref[...] = jnp.dot(s, w_ref[...], precision=HIGHEST, preferred_element_type=F32) + b_ref[...]


def _ada_mods(c8, ada_w, ada_b):
    out = pl.pallas_call(
        _ada_kernel,
        grid=(DEPTH, 6),
        in_specs=[pl.BlockSpec((8, D), lambda l, n: (0, 0)),
                  pl.BlockSpec((None, D, D), lambda l, n: (l, 0, n)),
                  pl.BlockSpec((None, 1, D), lambda l, n: (l, 0, n))],
        out_specs=pl.BlockSpec((None, 8, D), lambda l, n: (l, 0, n)),
        out_shape=jax.ShapeDtypeStruct((DEPTH, 8, 6 * D), F32),
        compiler_params=_params("arbitrary", "arbitrary"),
        name="ada_mods",
    )(c8, ada_w, ada_b.reshape(DEPTH, 1, 6 * D))
    return out[:, :3].reshape(DEPTH, 3, 6, D)


def _pw1_kernel(*refs, pending):
    if pending:
        x_ref, y_ref, slab_ref, modp_ref, g_ref, mod_ref, w_ref, b_ref, xo_ref, u_ref = refs
        x = _moe_mix(x_ref, y_ref, slab_ref, modp_ref, TR)
        xo_ref[...] = x
    else:
        x_ref, g_ref, mod_ref, w_ref, b_ref, u_ref = refs
        x = x_ref[...]
    h = _normmod(x, g_ref[...], mod_ref[0:1, :], mod_ref[1:2, :]).astype(BF16)
    a = jnp.dot(h, w_ref[:, :D], preferred_element_type=F32) + b_ref[:, :D]
    g = jnp.dot(h, w_ref[:, D:], preferred_element_type=F32) + b_ref[:, D:]
    u_ref[...] = a * _sigmoid(g)


def _pw1(x, pending, g, mods, w, b):
    row = jax.ShapeDtypeStruct((T, D), F32)
    out = pl.pallas_call(
        functools.partial(_pw1_kernel, pending=pending is not None),
        grid=(NR,),
        in_specs=[_row_spec(D)] + (_pending_specs(TR) if pending else [])
        + [_full_spec((1, D)), _mod_spec(), _full_spec((D, 2 * D)), _full_spec((1, 2 * D))],
        out_specs=[_row_spec(D)] * 2 if pending else _row_spec(D),
        out_shape=[row, row] if pending else row,
        compiler_params=_params("arbitrary"),
        name="conv_pw1_glu",
    )(x, *(pending or ()), g, mods, w, b)
    return out if pending else (x, out)


def _conv_kernel(prev_ref, cur_ref, next_ref, x_ref, mod_ref, dww_ref, dwb_ref, lng_ref, lnb_ref,
                 w_ref, b_ref, g2_ref, wh_ref, wl_ref, br_ref, o_ref, h_ref, slab_ref, counts_ref,
                 ext_ref, conv_ref, cnt_ref):
    i = pl.program_id(0)
    lat0 = CTX_TILES
    lat1 = CTX_TILES + LAT_TILES
    is_start = (i < CTX_TILES) | (i == lat0) | (i == lat1)
    is_end = (i < CTX_TILES) | (i == lat1 - 1) | (i == NT - 1)
    ext_ref[0:HALO, :] = jnp.where(is_start, 0.0, prev_ref[...])
    ext_ref[HALO:HALO + TM, :] = cur_ref[...]
    ext_ref[HALO + TM:, :] = jnp.where(is_end, 0.0, next_ref[...])

    def lane_block(j, carry):
        lo = pl.multiple_of(j * LANES, LANES)
        for rb in range(TM // CONV_ROWS):
            r0 = rb * CONV_ROWS
            win = ext_ref[r0:r0 + CONV_WIN, pl.ds(lo, LANES)]
            acc = jnp.broadcast_to(dwb_ref[:, pl.ds(lo, LANES)], (CONV_ROWS, LANES))
            for rho in range(SUBLANES):
                shifted = win if rho == 0 else pltpu.roll(win, shift=CONV_WIN - rho, axis=0)
                for k in range(CONV_WIDTH):
                    off = k + HALO - CONV_PAD - rho
                    if off % SUBLANES == 0:
                        acc = acc + dww_ref[k:k + 1, pl.ds(lo, LANES)] * shifted[off:off + CONV_ROWS]
            conv_ref[r0:r0 + CONV_ROWS, pl.ds(lo, LANES)] = acc
        return carry

    lax.fori_loop(0, D // LANES, lane_block, 0)
    v = conv_ref[...]
    mu = jnp.mean(v, axis=-1, keepdims=True)
    vc = v - mu
    var = jnp.mean(vc * vc, axis=-1, keepdims=True)
    y = vc * lax.rsqrt(var + EPS) * lng_ref[...] + lnb_ref[...]
    y = (y * _sigmoid(y)).astype(BF16)
    out = jnp.dot(y, w_ref[...], preferred_element_type=F32) + b_ref[...]
    x = x_ref[...] + mod_ref[2:3, :] * out
    o_ref[...] = x
    _route(x, TM, mod_ref, g2_ref, wh_ref, wl_ref, br_ref, h_ref, slab_ref, counts_ref, cnt_ref)


def _conv(u, x, mods, dww, dwb, lng, lnb, w, b, route):
    per = TM // HALO
    return pl.pallas_call(
        _conv_kernel,
        grid=(NT,),
        in_specs=[pl.BlockSpec((HALO, D), lambda i: (jnp.maximum(i * per - 1, 0), 0)),
                  _row_spec(D, TM),
                  pl.BlockSpec((HALO, D), lambda i: (jnp.minimum((i + 1) * per, T // HALO - 1), 0)),
                  _row_spec(D, TM), _mod_spec(TM),
                  _full_spec((CONV_WIDTH + 1, D)), _full_spec((1, D)), _full_spec((1, D)),
                  _full_spec((1, D)), _full_spec((D, D)), _full_spec((1, D))] + _route_in_specs(),
        out_specs=[_row_spec(D, TM)] + _route_out_specs(TM),
        out_shape=[jax.ShapeDtypeStruct((T, D), F32)] + _ROUTE_SHAPES,
        scratch_shapes=[pltpu.VMEM((TM + 2 * HALO, D), F32), pltpu.VMEM((TM, D), F32)] + _ROUTE_SCRATCH,
        compiler_params=_params("arbitrary"),
        name="conv_dw_ln_pw2",
    )(u, u, u, x, mods, dww, dwb, lng, lnb, w, b, *route)


def _log_sigmoid(z):
    return jnp.minimum(z, 0.0) - jnp.log(1.0 + jnp.exp(-jnp.abs(z)))


def _gla_proj_kernel(*refs, pending):
    if pending:
        x_ref, y_ref, slab_ref, modp_ref = refs[:4]
        refs = refs[4:]
        xo_ref = refs[6]
        refs = refs[:6] + refs[7:]
        x = _moe_mix(x_ref, y_ref, slab_ref, modp_ref, TM)
        xo_ref[...] = x
    else:
        x = refs[0][...]
        refs = refs[1:]
    g_ref, mod_ref, w_ref, gkh_ref, gkl_ref, gkb_ref, q_ref, k_ref, v_ref, gate_ref, lf_ref, lb_ref = refs
    h = _normmod(x, g_ref[...], mod_ref[0:1, :], mod_ref[1:2, :]).astype(BF16)
    kd = KEY_DIM
    q_ref[...] = jnp.dot(h, w_ref[:, 0:kd], preferred_element_type=F32) * (HEAD_K ** -0.5)
    k_ref[...] = jnp.dot(h, w_ref[:, kd:2 * kd], preferred_element_type=F32)
    v_ref[...] = jnp.dot(h, w_ref[:, 2 * kd:2 * kd + D], preferred_element_type=F32).astype(BF16)
    gate_ref[...] = jnp.dot(h, w_ref[:, 2 * kd + D:2 * kd + 2 * D], preferred_element_type=F32)
    codes = jnp.dot(h, w_ref[:, 2 * kd + 2 * D:], preferred_element_type=F32)
    z = _dot3(codes, gkh_ref[...], gkl_ref[...]) + gkb_ref[...]
    ls = _log_sigmoid(z) * (1.0 / GATE_NORM)
    lf_ref[...] = ls[:, :kd]
    lb_ref[...] = ls[:, kd:]


def _gla_proj(x, pending, g, mods, w, gk, gkb):
    gk_hi, gk_lo = _split_bf16(gk)
    shapes = [jax.ShapeDtypeStruct((T, KEY_DIM), F32), jax.ShapeDtypeStruct((T, KEY_DIM), F32),
              jax.ShapeDtypeStruct((T, D), BF16), jax.ShapeDtypeStruct((T, D), F32),
              jax.ShapeDtypeStruct((T, KEY_DIM), F32), jax.ShapeDtypeStruct((T, KEY_DIM), F32)]
    out_specs = [_row_spec(KEY_DIM, TM), _row_spec(KEY_DIM, TM), _row_spec(D, TM), _row_spec(D, TM),
                 _row_spec(KEY_DIM, TM), _row_spec(KEY_DIM, TM)]
    if pending:
        shapes = [jax.ShapeDtypeStruct((T, D), F32)] + shapes
        out_specs = [_row_spec(D, TM)] + out_specs
    out = pl.pallas_call(
        functools.partial(_gla_proj_kernel, pending=pending is not None),
        grid=(NT,),
        in_specs=[_row_spec(D, TM)] + (_pending_specs(TM) if pending else [])
        + [_full_spec((1, D)), _mod_spec(TM), _full_spec((D, W_IN_PAD)),
           _full_spec((LANES, 2 * KEY_DIM)), _full_spec((LANES, 2 * KEY_DIM)),
           _full_spec((1, 2 * KEY_DIM))],
        out_specs=out_specs,
        out_shape=shapes,
        compiler_params=_params("arbitrary"),
        name="gla_proj",
    )(x, *(pending or ()), g, mods, w, gk_hi, gk_lo, gkb)
    return (out[0], out[1:]) if pending else (x, out)


def _scan_tile(q_ref, k_ref, v_ref, lg_ref, o_ref, st_ref, cum_ref, reverse):
    r = lax.broadcasted_iota(jnp.int32, (TM, TM), 0)
    c = lax.broadcasted_iota(jnp.int32, (TM, TM), 1)
    shift = CHUNK.bit_length() - 1
    same = (r >> shift) == (c >> shift)
    tri = jnp.where(same & ((c >= r) if reverse else (c <= r)), 1.0, 0.0).astype(BF16)
    lg = lg_ref[...]
    hi = lg.astype(BF16)
    r1 = lg - hi.astype(F32)
    mid = r1.astype(BF16)
    lo = (r1 - mid.astype(F32)).astype(BF16)
    cum_ref[...] = (jnp.dot(tri, hi, preferred_element_type=F32)
                    + jnp.dot(tri, mid, preferred_element_type=F32)
                    + jnp.dot(tri, lo, preferred_element_type=F32))

    rr = lax.broadcasted_iota(jnp.int32, (CHUNK, CHUNK), 0)
    cc = lax.broadcasted_iota(jnp.int32, (CHUNK, CHUNK), 1)
    mask = (cc >= rr) if reverse else (cc <= rr)
    i_last = 0 if reverse else CHUNK - 1
    i_ref = CHUNK // 2 - 1 if reverse else CHUNK // 2
    chunks = range(TM // CHUNK)
    for ci in (reversed(chunks) if reverse else chunks):
        r0 = ci * CHUNK
        for h in range(HEADS):
            k0, v0 = h * HEAD_K, h * HEAD_V
            cm = cum_ref[r0:r0 + CHUNK, k0:k0 + HEAD_K]
            last = cum_ref[r0 + i_last:r0 + i_last + 1, k0:k0 + HEAD_K]
            ref = cum_ref[r0 + i_ref:r0 + i_ref + 1, k0:k0 + HEAD_K]
            qc = q_ref[r0:r0 + CHUNK, k0:k0 + HEAD_K]
            kc = k_ref[r0:r0 + CHUNK, k0:k0 + HEAD_K]
            vc = v_ref[r0:r0 + CHUNK, v0:v0 + HEAD_V]
            qr = (qc * jnp.exp(cm - ref)).astype(BF16)
            kr = (kc * jnp.exp(ref - cm)).astype(BF16)
            att = lax.dot_general(qr, kr, _NT_DIMS, preferred_element_type=F32)
            att = jnp.where(mask, att, 0.0).astype(BF16)
            qd = (qc * jnp.exp(cm)).astype(BF16)
            st = st_ref[h]
            o = (jnp.dot(att, vc, preferred_element_type=F32)
                 + lax.dot_general(qd, st.astype(BF16), _NT_DIMS, preferred_element_type=F32))
            kdec = (kc * jnp.exp(last - cm)).astype(BF16)
            st_ref[h] = st * jnp.exp(last) + lax.dot_general(vc, kdec, _TN_DIMS,
                                                             preferred_element_type=F32)
            o_ref[r0:r0 + CHUNK, v0:v0 + HEAD_V] = o


def _gla_scan_kernel(qf_ref, kf_ref, vf_ref, lf_ref, qb_ref, kb_ref, vb_ref, lb_ref,
                     of_ref, ob_ref, st_ref, cum_ref):
    @pl.when(pl.program_id(1) == 0)
    def _():
        st_ref[...] = jnp.zeros_like(st_ref)

    _scan_tile(qf_ref, kf_ref, vf_ref, lf_ref, of_ref, st_ref.at[0], cum_ref.at[0], False)
    _scan_tile(qb_ref, kb_ref, vb_ref, lb_ref, ob_ref, st_ref.at[1], cum_ref.at[1], True)


def _gla_scan(q, k, v, lf, lb):
    def tile_f(b, j):
        return jnp.where(j == 0, b, CTX_TILES + LAT_TILES * b + j - 1)

    def tile_b(b, j):
        return jnp.where(j == 0, b, CTX_TILES + LAT_TILES * (b + 1) - j)

    def rows(width, tile):
        return pl.BlockSpec((TM, width), lambda b, j: (tile(b, j), 0))

    def scan_specs(tile):
        return [rows(KEY_DIM, tile), rows(KEY_DIM, tile), rows(D, tile), rows(KEY_DIM, tile)]

    return pl.pallas_call(
        _gla_scan_kernel,
        grid=(BATCH, 1 + LAT_TILES),
        in_specs=scan_specs(tile_f) + scan_specs(tile_b),
        out_specs=[rows(D, tile_f), rows(D, tile_b)],
        out_shape=[jax.ShapeDtypeStruct((T, D), F32), jax.ShapeDtypeStruct((T, D), F32)],
        scratch_shapes=[pltpu.VMEM((2, HEADS, HEAD_V, HEAD_K), F32), pltpu.VMEM((2, TM, KEY_DIM), F32)],
        compiler_params=_params("arbitrary", "arbitrary"),
        name="gla_scan",
    )(q, k, v, lf, q, k, v, lb)


def _gla_out_kernel(of_ref, ob_ref, gate_ref, hg_ref, w_ref, x_ref, mod_ref, g2_ref, wh_ref, wl_ref, br_ref,
                    o_ref, h_ref, slab_ref, counts_ref, cnt_ref):
    heads = []
    for h in range(HEADS):
        v0 = h * HEAD_V
        o = ob_ref[:, v0:v0 + HEAD_V] + of_ref[:, v0:v0 + HEAD_V]
        on = o * lax.rsqrt(jnp.mean(o * o, axis=-1, keepdims=True) + EPS)
        on = on * hg_ref[:, v0:v0 + HEAD_V]
        gt = gate_ref[:, v0:v0 + HEAD_V]
        heads.append((on * (gt * _sigmoid(gt))).astype(BF16))
    y = jnp.dot(jnp.concatenate(heads, axis=-1), w_ref[...], preferred_element_type=F32)
    x = x_ref[...] + mod_ref[2:3, :] * y
    o_ref[...] = x
    _route(x, TR, mod_ref, g2_ref, wh_ref, wl_ref, br_ref, h_ref, slab_ref, counts_ref, cnt_ref)


def _gla_out(o_f, o_b, gate, head_g, w, x, mods, route):
    return pl.pallas_call(
        _gla_out_kernel,
        grid=(NR,),
        in_specs=[_row_spec(D), _row_spec(D), _row_spec(D), _full_spec((1, D)), _full_spec((D, D)),
                  _row_spec(D), _mod_spec()] + _route_in_specs(),
        out_specs=[_row_spec(D)] + _route_out_specs(TR),
        out_shape=[jax.ShapeDtypeStruct((T, D), F32)] + _ROUTE_SHAPES,
        scratch_shapes=_ROUTE_SCRATCH,
        compiler_params=_params("arbitrary"),
        name="gla_out_proj",
    )(o_f, o_b, gate, head_g, w, x, mods, *route)


def _route(x, rows, mod_ref, g_ref, wh_ref, wl_ref, b_ref, h_ref, slab_ref, counts_ref, cnt_ref):
    h = _normmod(x, g_ref[...], mod_ref[3:4, :], mod_ref[4:5, :])
    for s in range(SUB):
        h_ref[pl.ds(s, rows, stride=SUB), :] = h[:, s * LANES:(s + 1) * LANES]
    lg = _dot3(h, wh_ref[...], wl_ref[...]) + b_ref[...]
    lane = lax.broadcasted_iota(jnp.int32, (rows, LANES), 1).astype(F32)
    neg = -jnp.inf
    far = float(LANES)
    is_group = (lane >= EXPERTS) & (lane < EXPERTS + GROUPS)
    glog = jnp.where(is_group, lg, neg)
    gmax = jnp.max(glog, axis=-1, keepdims=True)
    gidx = jnp.min(jnp.where(glog == gmax, lane, far), axis=-1, keepdims=True) - EXPERTS
    g_w = 1.0 / jnp.sum(jnp.exp(glog - gmax), axis=-1, keepdims=True)
    lo = gidx * PER_GROUP
    vals = jnp.where((lane >= lo) & (lane < lo + PER_GROUP), lg, neg)
    m1 = jnp.max(vals, axis=-1, keepdims=True)
    i1 = jnp.min(jnp.where(vals == m1, lane, far), axis=-1, keepdims=True)
    vals2 = jnp.where(lane == i1, neg, vals)
    m2 = jnp.max(vals2, axis=-1, keepdims=True)
    i2 = jnp.min(jnp.where(vals2 == m2, lane, far), axis=-1, keepdims=True)
    e = jnp.exp(m2 - m1)
    w1 = g_w / (1.0 + e)
    w2 = g_w * e / (1.0 + e)
    first_lo = i1 < i2
    a = jnp.minimum(i1, i2) - lo
    b = jnp.maximum(i1, i2) - lo
    cls = gidx * PAIRS + a * (PER_GROUP - 1) - a * (a - 1.0) * 0.5 + (b - a - 1.0)
    w_lo = jnp.where(first_lo, w1, w2)
    w_hi = jnp.where(first_lo, w2, w1)
    @pl.when(pl.program_id(0) == 0)
    def _():
        cnt_ref[...] = jnp.zeros_like(cnt_ref)

    onehot = jnp.where(lane == cls, 1.0, 0.0)
    r = lax.broadcasted_iota(jnp.int32, (rows, rows), 0)
    c = lax.broadcasted_iota(jnp.int32, (rows, rows), 1)
    before = jnp.where(c < r, 1.0, 0.0).astype(BF16)
    prefix = jnp.dot(before, onehot.astype(BF16), preferred_element_type=F32)
    base = cnt_ref[...]
    rank = jnp.sum(onehot * (prefix + base), axis=-1, keepdims=True)
    total = base + jnp.sum(onehot, axis=0, keepdims=True)
    cnt_ref[...] = total
    counts_ref[...] = total
    slab_ref[...] = jnp.where(lane == 0, cls, jnp.where(lane == 1, rank, jnp.where(
        lane == 2, w_lo, jnp.where(lane == 3, w_hi, 0.0))))


def _route_in_specs():
    return [_full_spec((1, D)), _full_spec((D, LANES)), _full_spec((D, LANES)), _full_spec((1, LANES))]


def _route_out_specs(rows):
    return [pl.BlockSpec((rows * SUB, LANES), lambda i: (i, 0)), _row_spec(LANES, rows),
            _full_spec((1, LANES))]


_ROUTE_SHAPES = [jax.ShapeDtypeStruct((T * SUB, LANES), F32), jax.ShapeDtypeStruct((T, LANES), F32),
                 jax.ShapeDtypeStruct((1, LANES), F32)]
_ROUTE_SCRATCH = [pltpu.VMEM((1, LANES), F32)]


def _ffn_kernel(lo_ref, hi_ref, nv_ref, tok_ref, out_ref, hs_ref, w1l_ref, w3l_ref, w2l_ref,
                w1h_ref, w3h_ref, w2h_ref, ys_ref, xg, yb, gsem, ssem):
    i = pl.program_id(0)
    slot = i % 2

    def groups(blk):
        return (nv_ref[jnp.clip(blk, 0, NB - 1)] + ROW_GROUP - 1) // ROW_GROUP

    def gather_copy(blk, sl, r):
        tok = tok_ref[blk * BM + r]
        return pltpu.make_async_copy(hs_ref.at[pl.ds(pl.multiple_of(tok * SUB, SUB), SUB)],
                                     xg.at[sl, pl.ds(r * XPITCH, SUB)], gsem.at[sl])

    def scatter_copy(blk, sl, r):
        row = out_ref[blk * BM + r]
        return pltpu.make_async_copy(yb.at[sl, pl.ds(r * YPITCH, YROWS)],
                                     ys_ref.at[pl.ds(pl.multiple_of(row * YROWS, YROWS), YROWS)],
                                     ssem.at[sl])

    def start_block(copy, blk, sl, priority):
        def group(gi, carry):
            for u in range(ROW_GROUP):
                copy(blk, sl, gi * ROW_GROUP + u).start(priority=priority)
            return carry
        lax.fori_loop(0, groups(blk), group, 0)

    def wait_gather(blk, sl):
        n = groups(blk) * (ROW_GROUP * SUB)

        @pl.when(n > 0)
        def _():
            pltpu.make_async_copy(hs_ref.at[pl.ds(0, n)], xg.at[sl, pl.ds(0, n)], gsem.at[sl]).wait()

    def wait_scatter(blk, sl):
        n = groups(blk) * (ROW_GROUP * YROWS)

        @pl.when(n > 0)
        def _():
            pltpu.make_async_copy(yb.at[sl, pl.ds(0, n)], ys_ref.at[pl.ds(0, n)], ssem.at[sl]).wait()

    @pl.when(i == 0)
    def _():
        xg[...] = jnp.zeros_like(xg)
        yb[...] = jnp.zeros_like(yb)
        for sl in range(2):
            spare = pltpu.make_async_copy(yb.at[sl, pl.ds(0, BM * YROWS)],
                                          ys_ref.at[pl.ds((T + sl * BM) * YROWS, BM * YROWS)], ssem.at[sl])
            spare.start()
            spare.wait()
        start_block(gather_copy, 0, 0, 0)

    @pl.when(i + 1 < NB)
    def _():
        start_block(gather_copy, i + 1, 1 - slot, 0)

    wait_gather(i, slot)

    @pl.when(i >= 2)
    def _():
        wait_scatter(i - 2, slot)

    def expert(xb, w1_ref, w3_ref, w2_ref):
        a = jnp.dot(xb, w1_ref[...], preferred_element_type=F32)
        b = jnp.dot(xb, w3_ref[...], preferred_element_type=F32)
        hh = (a * _sigmoid(a) * b).astype(BF16)
        return jnp.dot(hh, w2_ref[...], preferred_element_type=F32)

    @pl.when(nv_ref[i] > 0)
    def _():
        xb = jnp.concatenate([xg[slot, pl.ds(s, BM, stride=XPITCH), :] for s in range(SUB)],
                             axis=-1).astype(BF16)
        y_lo = expert(xb, w1l_ref, w3l_ref, w2l_ref)
        for s in range(SUB):
            yb[slot, pl.ds(s, BM, stride=YPITCH), :] = y_lo[:, s * LANES:(s + 1) * LANES]
        y_hi = expert(xb, w1h_ref, w3h_ref, w2h_ref)
        for s in range(SUB):
            yb[slot, pl.ds(SUB + s, BM, stride=YPITCH), :] = y_hi[:, s * LANES:(s + 1) * LANES]
        start_block(scatter_copy, i, slot, 1)

    @pl.when(i == NB - 1)
    def _():
        wait_scatter(NB - 2, 1 - slot)
        wait_scatter(NB - 1, slot)


def _ffn(blk_lo, blk_hi, n_valid, tok_buf, out_buf, hs, w1, w3, w2, layer):
    def w_spec(shape, which):
        return pl.BlockSpec((None, None) + shape,
                            lambda i, lo, hi, nv, tok, out: (layer, (lo, hi)[which][i], 0, 0))

    return pl.pallas_call(
        _ffn_kernel,
        grid_spec=pltpu.PrefetchScalarGridSpec(
            num_scalar_prefetch=5,
            grid=(NB,),
            in_specs=[pl.BlockSpec(memory_space=pl.ANY),
                      w_spec((D, FF), 0), w_spec((D, FF), 0), w_spec((FF, D), 0),
                      w_spec((D, FF), 1), w_spec((D, FF), 1), w_spec((FF, D), 1)],
            out_specs=pl.BlockSpec(memory_space=pl.ANY),
            scratch_shapes=[pltpu.VMEM((2, BM * XPITCH, LANES), F32), pltpu.VMEM((2, BM * YPITCH, LANES), F32),
                            pltpu.SemaphoreType.DMA((2,)), pltpu.SemaphoreType.DMA((2,))]),
        out_shape=jax.ShapeDtypeStruct(((T + 2 * BM) * YROWS, LANES), F32),
        compiler_params=_params("arbitrary"),
        name="moe_expert_ffn",
    )(blk_lo, blk_hi, n_valid, tok_buf, out_buf, hs, w1, w3, w2, w1, w3, w2)


def _combine_kernel(x_ref, y_ref, slab_ref, mod_ref, fg_ref, o_ref):
    x = _moe_mix(x_ref, y_ref, slab_ref, mod_ref, TR)
    o_ref[...] = x * lax.rsqrt(jnp.mean(x * x, axis=-1, keepdims=True) + EPS) * fg_ref[...]


def _final_combine(x, ys, slab, mods, final_g):
    first = T_CTX // TR

    def rows(height, width):
        return pl.BlockSpec((height, width), lambda i: (i + first, 0))

    return pl.pallas_call(
        _combine_kernel,
        grid=(NR - first,),
        in_specs=[rows(TR, D), rows(TR * YROWS, LANES), rows(TR, LANES),
                  _mod_spec(TR, first), _full_spec((1, D))],
        out_specs=_row_spec(D),
        out_shape=jax.ShapeDtypeStruct((T - first * TR, D), F32),
        compiler_params=_params("arbitrary"),
        name="moe_combine",
    )(x, ys, slab, mods, final_g)


def _class_experts():
    lo, hi = [], []
    for g in range(GROUPS):
        for a in range(PER_GROUP):
            for b in range(a + 1, PER_GROUP):
                lo.append(g * PER_GROUP + a)
                hi.append(g * PER_GROUP + b)
    return jnp.array(lo, jnp.int32), jnp.array(hi, jnp.int32)


def _dispatch_plan(slab, counts):
    cls = slab[:, 0].astype(jnp.int32)
    rank = slab[:, 1].astype(jnp.int32)
    counts = counts[0, :NCLS].astype(jnp.int32)
    padded = ((counts + BM - 1) // BM) * BM
    pend = jnp.cumsum(padded)
    pstart = pend - padded
    ids = jnp.arange(NCLS, dtype=jnp.int32)

    def lookup(table, idx):
        return jnp.sum(jnp.where(idx[:, None] == ids[None, :], table[None, :], 0), axis=1)

    dest = lookup(pstart, cls) + rank
    tok_buf = (jnp.arange(P, dtype=jnp.int32) % T).at[dest].set(jnp.arange(T, dtype=jnp.int32))
    blk0 = jnp.arange(NB, dtype=jnp.int32) * BM
    blk_c = jnp.minimum(jnp.sum((pend[None, :] <= blk0[:, None]).astype(jnp.int32), axis=1), NCLS - 1)
    n_valid = jnp.clip(lookup(counts, blk_c) - (blk0 - lookup(pstart, blk_c)), 0, BM)
    cls_lo, cls_hi = _class_experts()
    slot = jnp.arange(P, dtype=jnp.int32)
    spare = T + ((slot // BM) % 2) * BM + slot % BM
    out_buf = jnp.where(slot % BM < jnp.repeat(n_valid, BM), tok_buf, spare)
    return tok_buf, out_buf, lookup(cls_lo, blk_c), lookup(cls_hi, blk_c), n_valid


def _moe(hs, slab, counts, w1, w3, w2, layer):
    tok_buf, out_buf, blk_lo, blk_hi, n_valid = _dispatch_plan(slab, counts)
    return _ffn(blk_lo, blk_hi, n_valid, tok_buf, out_buf, hs, w1, w3, w2, layer)


def _cast_kernel(w_ref, o_ref):
    o_ref[...] = w_ref[...].astype(BF16)


def _to_bf16(w, rows):
    n, cols = w.shape
    return pl.pallas_call(
        _cast_kernel,
        grid=(n // rows,),
        in_specs=[pl.BlockSpec((rows, cols), lambda i: (i, 0))],
        out_specs=pl.BlockSpec((rows, cols), lambda i: (i, 0)),
        out_shape=jax.ShapeDtypeStruct((n, cols), BF16),
        compiler_params=_params("arbitrary"),
        name="weights_to_bf16",
    )(w)


def _sincos_2d(rows, width, dim):
    quarter = dim // 4
    omega = 1.0 / (POS_BASE ** (jnp.arange(quarter, dtype=F32) / quarter))
    ar = jnp.arange(rows, dtype=F32)[:, None] * omega
    ac = jnp.arange(width, dtype=F32)[:, None] * omega
    row_part = jnp.concatenate([jnp.sin(ar), jnp.cos(ar)], axis=-1)
    col_part = jnp.concatenate([jnp.sin(ac), jnp.cos(ac)], axis=-1)
    row_part = jnp.broadcast_to(row_part[:, None, :], (rows, width, dim // 2))
    col_part = jnp.broadcast_to(col_part[None, :, :], (rows, width, dim // 2))
    return jnp.concatenate([row_part, col_part], axis=-1).reshape(rows * width, dim)


def kernel(x, c, ctx, c_ctx, ada_w, ada_b, norm_g, final_g, conv_pw1_w, conv_pw1_b, conv_dw_w, conv_dw_b, conv_ln_g, conv_ln_b, conv_pw2_w, conv_pw2_b, gla_w_in, gla_gk_up, gla_gk_b, gla_head_g, gla_w_o, moe_wg, moe_bg, moe_wr, moe_br, moe_w1, moe_w3, moe_w2):
    assert x.shape == (BATCH, SEQ, D) and ctx.shape == (BATCH, CTX_LEN, D)
    pos = _sincos_2d(SEQ // GRID_W, GRID_W, D)
    xt = jnp.concatenate([ctx.reshape(T_CTX, D), (x + pos[None]).reshape(T_LAT, D)], axis=0)

    c8 = jnp.concatenate([c_ctx[None], c, jnp.zeros((8 - 1 - BATCH, D), F32)], axis=0)
    mods = _ada_mods(c8, ada_w, ada_b)
    fg = final_g.reshape(1, D)
    w1b = _to_bf16(moe_w1.reshape(-1, FF), 4 * D).reshape(moe_w1.shape)
    w3b = _to_bf16(moe_w3.reshape(-1, FF), 4 * D).reshape(moe_w3.shape)
    w2b = _to_bf16(moe_w2.reshape(-1, D), 4 * FF).reshape(moe_w2.shape)

    pending = None
    for i in range(DEPTH):
        j = i // 2
        g_mix = norm_g[i, 0].reshape(1, D)
        wr_cat = jnp.concatenate([moe_wr[i], moe_wg[i],
                                  jnp.zeros((D, LANES - EXPERTS - GROUPS), F32)], axis=1)
        br_cat = jnp.concatenate([moe_br[i], moe_bg[i],
                                  jnp.zeros((LANES - EXPERTS - GROUPS,), F32)]).reshape(1, LANES)
        route = (norm_g[i, 1].reshape(1, D),) + _split_bf16(wr_cat) + (br_cat,)
        if i % 2 == 0:
            xt, u = _pw1(xt, pending, g_mix, mods[i], conv_pw1_w[j].astype(BF16),
                         conv_pw1_b[j].reshape(1, 2 * D))
            dww = jnp.concatenate([conv_dw_w[j], jnp.zeros((1, D), F32)], axis=0)
            xt, hs, slab, counts = _conv(u, xt, mods[i], dww, conv_dw_b[j].reshape(1, D),
                                         conv_ln_g[j].reshape(1, D), conv_ln_b[j].reshape(1, D),
                                         conv_pw2_w[j].astype(BF16), conv_pw2_b[j].reshape(1, D), route)
        else:
            w_in = jnp.pad(gla_w_in[j], ((0, 0), (0, W_IN_PAD - gla_w_in.shape[-1]))).astype(BF16)
            gk = jnp.zeros((LANES, 2 * KEY_DIM), F32)
            gk = gk.at[0:GATE_RANK, 0:KEY_DIM].set(gla_gk_up[j, 0])
            gk = gk.at[GATE_RANK:2 * GATE_RANK, KEY_DIM:].set(gla_gk_up[j, 1])
            gkb = gla_gk_b[j].reshape(1, 2 * KEY_DIM)
            xt, (q, k, v, gate, lf, lb) = _gla_proj(xt, pending, g_mix, mods[i], w_in, gk, gkb)
            o_f, o_b = _gla_scan(q, k, v, lf, lb)
            xt, hs, slab, counts = _gla_out(o_f, o_b, gate, gla_head_g[j].reshape(1, D),
                                            gla_w_o[j].astype(BF16), xt, mods[i], route)
        ys = _moe(hs, slab, counts, w1b, w3b, w2b, i)
        pending = (ys, slab, mods[i])
    return _final_combine(xt, *pending, fg).reshape(BATCH, SEQ, D)
```

```python
import functools

import jax
import jax.numpy as jnp
from jax import lax
from jax.experimental import pallas as pl
from jax.experimental.pallas import tpu as pltpu

F32 = jnp.float32
BF16 = jnp.bfloat16
HIGHEST = lax.Precision.HIGHEST

D = 1024
BATCH = 2
SEQ = 16384
CTX_LEN = 256
DEPTH = 4
GRID_W = 64
POS_BASE = 10000.0
EPS = 1e-6
CONV_WIDTH = 31
CONV_PAD = CONV_WIDTH // 2
HEADS = 4
KEY_DIM = D // 2
HEAD_K = KEY_DIM // HEADS
HEAD_V = D // HEADS
GATE_RANK = 16
GATE_NORM = 16.0
CHUNK = 64
GROUPS = 4
PER_GROUP = 8
EXPERTS = GROUPS * PER_GROUP
FF = D // 2

TM = 256
T_CTX = BATCH * CTX_LEN
T_LAT = BATCH * SEQ
T = T_CTX + T_LAT
NT = T // TM
TR = 512
NR = T // TR
CTX_TILES = T_CTX // TM
LAT_TILES = SEQ // TM
HALO = 16
CONV_ROWS = 128
CONV_WIN = CONV_ROWS + 2 * HALO
SUBLANES = 8
BM = 256
ROW_GROUP = 16
PAIRS = PER_GROUP * (PER_GROUP - 1) // 2
NCLS = GROUPS * PAIRS
NB = T // BM + NCLS
P = NB * BM
LANES = 128
SUB = D // LANES
YROWS = 2 * SUB
XPITCH = 12
YPITCH = 20
GLA_IN = 2 * KEY_DIM + 2 * D + 2 * GATE_RANK
VMEM_LIMIT = 48 * 1024 * 1024

_NT_DIMS = (((1,), (1,)), ((), ()))
_TN_DIMS = (((0,), (0,)), ((), ()))


def _params(*sem):
    return pltpu.CompilerParams(dimension_semantics=sem, vmem_limit_bytes=VMEM_LIMIT)


def _seg(i, rows):
    ctx_tiles = T_CTX // rows
    return jnp.where(i < ctx_tiles, 0, jnp.where(i < ctx_tiles + SEQ // rows, 1, 2))


def _mod_spec(rows=TR, first=0):
    return pl.BlockSpec((None, 6, D), lambda i: (_seg(i + first, rows), 0, 0))


def _row_spec(width, rows=TR):
    return pl.BlockSpec((rows, width), lambda i: (i, 0))


def _full_spec(shape):
    return pl.BlockSpec(shape, lambda i: (0,) * len(shape))


def _normmod(x, g, shift, scale):
    ms = jnp.mean(x * x, axis=-1, keepdims=True)
    return (x * lax.rsqrt(ms + EPS) * g) * (1.0 + scale) + shift


def _sigmoid(x):
    return 1.0 / (1.0 + jnp.exp(-x))


def _split_bf16(a):
    hi = a.astype(BF16)
    return hi, (a - hi.astype(F32)).astype(BF16)


def _dot3(a, w_hi, w_lo):
    a_hi, a_lo = _split_bf16(a)
    return (jnp.dot(a_hi, w_hi, preferred_element_type=F32)
            + jnp.dot(a_lo, w_hi, preferred_element_type=F32)
            + jnp.dot(a_hi, w_lo, preferred_element_type=F32))


def _moe_mix(x_ref, y_ref, slab_ref, mod_ref, rows):
    w_lo = slab_ref[:, 2:3]
    w_hi = slab_ref[:, 3:4]
    y = jnp.concatenate([w_lo * y_ref[pl.ds(s, rows, stride=YROWS), :]
                         + w_hi * y_ref[pl.ds(SUB + s, rows, stride=YROWS), :] for s in range(SUB)], axis=-1)
    return x_ref[...] + mod_ref[5:6, :] * y


def _pending_specs(rows):
    return [pl.BlockSpec((rows * YROWS, LANES), lambda i: (i, 0)), _row_spec(LANES, rows), _mod_spec(rows)]


def _ada_kernel(c_ref, w_ref, b_ref, o_ref):
    c = c_ref[...]
    s = c * _sigmoid(c)
    o_ref[...] = jnp.dot(s, w_ref[...], precision=HIGHEST, preferred_element_type=F32) + b_ref[...]


def _ada_mods(c8, ada_w, ada_b):
    out = pl.pallas_call(
        _ada_kernel,
        grid=(DEPTH, 6),
        in_specs=[pl.BlockSpec((8, D), lambda l, n: (0, 0)),
                  pl.BlockSpec((None, D, D), lambda l, n: (l, 0, n)),
                  pl.BlockSpec((None, 1, D), lambda l, n: (l, 0, n))],
        out_specs=pl.BlockSpec((None, 8, D), lambda l, n: (l, 0, n)),
        out_shape=jax.ShapeDtypeStruct((DEPTH, 8, 6 * D), F32),
        compiler_params=_params("arbitrary", "arbitrary"),
        name="ada_mods",
    )(c8, ada_w, ada_b.reshape(DEPTH, 1, 6 * D))
    return out[:, :3].reshape(DEPTH, 3, 6, D)


def _pw1_kernel(*refs, pending):
    if pending:
        x_ref, y_ref, slab_ref, modp_ref, g_ref, mod_ref, w_ref, b_ref, xo_ref, u_ref = refs
        x = _moe_mix(x_ref, y_ref, slab_ref, modp_ref, TR)
        xo_ref[...] = x
    else:
        x_ref, g_ref, mod_ref, w_ref, b_ref, u_ref = refs
        x = x_ref[...]
    h = _normmod(x, g_ref[...], mod_ref[0:1, :], mod_ref[1:2, :]).astype(BF16)
    a = jnp.dot(h, w_ref[:, :D], preferred_element_type=F32) + b_ref[:, :D]
    g = jnp.dot(h, w_ref[:, D:], preferred_element_type=F32) + b_ref[:, D:]
    u_ref[...] = a * _sigmoid(g)


def _pw1(x, pending, g, mods, w, b):
    row = jax.ShapeDtypeStruct((T, D), F32)
    out = pl.pallas_call(
        functools.partial(_pw1_kernel, pending=pending is not None),
        grid=(NR,),
        in_specs=[_row_spec(D)] + (_pending_specs(TR) if pending else [])
        + [_full_spec((1, D)), _mod_spec(), _full_spec((D, 2 * D)), _full_spec((1, 2 * D))],
        out_specs=[_row_spec(D)] * 2 if pending else _row_spec(D),
        out_shape=[row, row] if pending else row,
        compiler_params=_params("arbitrary"),
        name="conv_pw1_glu",
    )(x, *(pending or ()), g, mods, w, b)
    return out if pending else (x, out)


def _conv_kernel(prev_ref, cur_ref, next_ref, x_ref, mod_ref, dww_ref, dwb_ref, lng_ref, lnb_ref,
                 w_ref, b_ref, g2_ref, wh_ref, wl_ref, br_ref, o_ref, h_ref, slab_ref, counts_ref,
                 ext_ref, conv_ref, cnt_ref):
    i = pl.program_id(0)
    lat0 = CTX_TILES
    lat1 = CTX_TILES + LAT_TILES
    is_start = (i < CTX_TILES) | (i == lat0) | (i == lat1)
    is_end = (i < CTX_TILES) | (i == lat1 - 1) | (i == NT - 1)
    ext_ref[0:HALO, :] = jnp.where(is_start, 0.0, prev_ref[...])
    ext_ref[HALO:HALO + TM, :] = cur_ref[...]
    ext_ref[HALO + TM:, :] = jnp.where(is_end, 0.0, next_ref[...])

    def lane_block(j, carry):
        lo = pl.multiple_of(j * LANES, LANES)
        for rb in range(TM // CONV_ROWS):
            r0 = rb * CONV_ROWS
            win = ext_ref[r0:r0 + CONV_WIN, pl.ds(lo, LANES)]
            acc = jnp.broadcast_to(dwb_ref[:, pl.ds(lo, LANES)], (CONV_ROWS, LANES))
            for rho in range(SUBLANES):
                shifted = win if rho == 0 else pltpu.roll(win, shift=CONV_WIN - rho, axis=0)
                for k in range(CONV_WIDTH):
                    off = k + HALO - CONV_PAD - rho
                    if off % SUBLANES == 0:
                        acc = acc + dww_ref[k:k + 1, pl.ds(lo, LANES)] * shifted[off:off + CONV_ROWS]
            conv_ref[r0:r0 + CONV_ROWS, pl.ds(lo, LANES)] = acc
        return carry

    lax.fori_loop(0, D // LANES, lane_block, 0)
    v = conv_ref[...]
    mu = jnp.mean(v, axis=-1, keepdims=True)
    vc = v - mu
    var = jnp.mean(vc * vc, axis=-1, keepdims=True)
    y = vc * lax.rsqrt(var + EPS) * lng_ref[...] + lnb_ref[...]
    y = (y * _sigmoid(y)).astype(BF16)
    out = jnp.dot(y, w_ref[...], preferred_element_type=F32) + b_ref[...]
    x = x_ref[...] + mod_ref[2:3, :] * out
    o_ref[...] = x
    _route(x, TM, mod_ref, g2_ref, wh_ref, wl_ref, br_ref, h_ref, slab_ref, counts_ref, cnt_ref)


def _conv(u, x, mods, dww, dwb, lng, lnb, w, b, route):
    per = TM // HALO
    return pl.pallas_call(
        _conv_kernel,
        grid=(NT,),
        in_specs=[pl.BlockSpec((HALO, D), lambda i: (jnp.maximum(i * per - 1, 0), 0)),
                  _row_spec(D, TM),
                  pl.BlockSpec((HALO, D), lambda i: (jnp.minimum((i + 1) * per, T // HALO - 1), 0)),
                  _row_spec(D, TM), _mod_spec(TM),
                  _full_spec((CONV_WIDTH + 1, D)), _full_spec((1, D)), _full_spec((1, D)),
                  _full_spec((1, D)), _full_spec((D, D)), _full_spec((1, D))] + _route_in_specs(),
        out_specs=[_row_spec(D, TM)] + _route_out_specs(TM),
        out_shape=[jax.ShapeDtypeStruct((T, D), F32)] + _ROUTE_SHAPES,
        scratch_shapes=[pltpu.VMEM((TM + 2 * HALO, D), F32), pltpu.VMEM((TM, D), F32)] + _ROUTE_SCRATCH,
        compiler_params=_params("arbitrary"),
        name="conv_dw_ln_pw2",
    )(u, u, u, x, mods, dww, dwb, lng, lnb, w, b, *route)


def _log_sigmoid(z):
    return jnp.minimum(z, 0.0) - jnp.log(1.0 + jnp.exp(-jnp.abs(z)))


def _gla_proj_kernel(*refs, pending):
    if pending:
        x_ref, y_ref, slab_ref, modp_ref = refs[:4]
        refs = refs[4:]
        xo_ref = refs[7]
        refs = refs[:7] + refs[8:]
        x = _moe_mix(x_ref, y_ref, slab_ref, modp_ref, TM)
        xo_ref[...] = x
    else:
        x = refs[0][...]
        refs = refs[1:]
    (g_ref, mod_ref, w_ref, wc_ref, gkh_ref, gkl_ref, gkb_ref,
     q_ref, k_ref, v_ref, gate_ref, lf_ref, lb_ref) = refs
    h = _normmod(x, g_ref[...], mod_ref[0:1, :], mod_ref[1:2, :]).astype(BF16)
    kd = KEY_DIM
    q_ref[...] = jnp.dot(h, w_ref[:, 0:kd], preferred_element_type=F32) * (HEAD_K ** -0.5)
    k_ref[...] = jnp.dot(h, w_ref[:, kd:2 * kd], preferred_element_type=F32)
    v_ref[...] = jnp.dot(h, w_ref[:, 2 * kd:2 * kd + D], preferred_element_type=F32).astype(BF16)
    gate_ref[...] = jnp.dot(h, w_ref[:, 2 * kd + D:2 * kd + 2 * D], preferred_element_type=F32)
    codes = jnp.dot(h, wc_ref[...], preferred_element_type=F32)
    z = _dot3(codes, gkh_ref[...], gkl_ref[...]) + gkb_ref[...]
    ls = _log_sigmoid(z) * (1.0 / GATE_NORM)
    lf_ref[...] = ls[:, :kd]
    lb_ref[...] = ls[:, kd:]


def _gla_proj(x, pending, g, mods, w, w_codes, gk, gkb):
    gk_hi, gk_lo = _split_bf16(gk)
    shapes = [jax.ShapeDtypeStruct((T, KEY_DIM), F32), jax.ShapeDtypeStruct((T, KEY_DIM), F32),
              jax.ShapeDtypeStruct((T, D), BF16), jax.ShapeDtypeStruct((T, D), F32),
              jax.ShapeDtypeStruct((T, KEY_DIM), F32), jax.ShapeDtypeStruct((T, KEY_DIM), F32)]
    out_specs = [_row_spec(KEY_DIM, TM), _row_spec(KEY_DIM, TM), _row_spec(D, TM), _row_spec(D, TM),
                 _row_spec(KEY_DIM, TM), _row_spec(KEY_DIM, TM)]
    if pending:
        shapes = [jax.ShapeDtypeStruct((T, D), F32)] + shapes
        out_specs = [_row_spec(D, TM)] + out_specs
    out = pl.pallas_call(
        functools.partial(_gla_proj_kernel, pending=pending is not None),
        grid=(NT,),
        in_specs=[_row_spec(D, TM)] + (_pending_specs(TM) if pending else [])
        + [_full_spec((1, D)), _mod_spec(TM), _full_spec((D, GLA_IN)), _full_spec((D, LANES)),
           _full_spec((LANES, 2 * KEY_DIM)), _full_spec((LANES, 2 * KEY_DIM)),
           _full_spec((1, 2 * KEY_DIM))],
        out_specs=out_specs,
        out_shape=shapes,
        compiler_params=_params("arbitrary"),
        name="gla_proj",
    )(x, *(pending or ()), g, mods, w, w_codes, gk_hi, gk_lo, gkb)
    return (out[0], out[1:]) if pending else (x, out)


def _scan_tile(q_ref, k_ref, v_ref, lg_ref, o_ref, st_ref, cum_ref, reverse):
    r = lax.broadcasted_iota(jnp.int32, (TM, TM), 0)
    c = lax.broadcasted_iota(jnp.int32, (TM, TM), 1)
    shift = CHUNK.bit_length() - 1
    same = (r >> shift) == (c >> shift)
    tri = jnp.where(same & ((c >= r) if reverse else (c <= r)), 1.0, 0.0).astype(BF16)
    lg = lg_ref[...]
    hi = lg.astype(BF16)
    r1 = lg - hi.astype(F32)
    mid = r1.astype(BF16)
    lo = (r1 - mid.astype(F32)).astype(BF16)
    cum_ref[...] = (jnp.dot(tri, hi, preferred_element_type=F32)
                    + jnp.dot(tri, mid, preferred_element_type=F32)
                    + jnp.dot(tri, lo, preferred_element_type=F32))

    rr = lax.broadcasted_iota(jnp.int32, (CHUNK, CHUNK), 0)
    cc = lax.broadcasted_iota(jnp.int32, (CHUNK, CHUNK), 1)
    mask = (cc >= rr) if reverse else (cc <= rr)
    i_last = 0 if reverse else CHUNK - 1
    i_ref = CHUNK // 2 - 1 if reverse else CHUNK // 2
    chunks = range(TM // CHUNK)
    for ci in (reversed(chunks) if reverse else chunks):
        r0 = ci * CHUNK
        for h in range(HEADS):
            k0, v0 = h * HEAD_K, h * HEAD_V
            cm = cum_ref[r0:r0 + CHUNK, k0:k0 + HEAD_K]
            last = cum_ref[r0 + i_last:r0 + i_last + 1, k0:k0 + HEAD_K]
            ref = cum_ref[r0 + i_ref:r0 + i_ref + 1, k0:k0 + HEAD_K]
            qc = q_ref[r0:r0 + CHUNK, k0:k0 + HEAD_K]
            kc = k_ref[r0:r0 + CHUNK, k0:k0 + HEAD_K]
            vc = v_ref[r0:r0 + CHUNK, v0:v0 + HEAD_V]
            qr = (qc * jnp.exp(cm - ref)).astype(BF16)
            kr = (kc * jnp.exp(ref - cm)).astype(BF16)
            att = lax.dot_general(qr, kr, _NT_DIMS, preferred_element_type=F32)
            att = jnp.where(mask, att, 0.0).astype(BF16)
            qd = (qc * jnp.exp(cm)).astype(BF16)
            st = st_ref[h]
            o = (jnp.dot(att, vc, preferred_element_type=F32)
                 + lax.dot_general(qd, st.astype(BF16), _NT_DIMS, preferred_element_type=F32))
            kdec = (kc * jnp.exp(last - cm)).astype(BF16)
            st_ref[h] = st * jnp.exp(last) + lax.dot_general(vc, kdec, _TN_DIMS,
                                                             preferred_element_type=F32)
            o_ref[r0:r0 + CHUNK, v0:v0 + HEAD_V] = o


def _gla_scan_kernel(qf_ref, kf_ref, vf_ref, lf_ref, qb_ref, kb_ref, vb_ref, lb_ref,
                     of_ref, ob_ref, st_ref, cum_ref):
    @pl.when(pl.program_id(1) == 0)
    def _():
        st_ref[...] = jnp.zeros_like(st_ref)

    _scan_tile(qf_ref, kf_ref, vf_ref, lf_ref, of_ref, st_ref.at[0], cum_ref.at[0], False)
    _scan_tile(qb_ref, kb_ref, vb_ref, lb_ref, ob_ref, st_ref.at[1], cum_ref.at[1], True)


def _gla_scan(q, k, v, lf, lb):
    def tile_f(b, j):
        return jnp.where(j == 0, b, CTX_TILES + LAT_TILES * b + j - 1)

    def tile_b(b, j):
        return jnp.where(j == 0, b, CTX_TILES + LAT_TILES * (b + 1) - j)

    def rows(width, tile):
        return pl.BlockSpec((TM, width), lambda b, j: (tile(b, j), 0))

    def scan_specs(tile):
        return [rows(KEY_DIM, tile), rows(KEY_DIM, tile), rows(D, tile), rows(KEY_DIM, tile)]

    return pl.pallas_call(
        _gla_scan_kernel,
        grid=(BATCH, 1 + LAT_TILES),
        in_specs=scan_specs(tile_f) + scan_specs(tile_b),
        out_specs=[rows(D, tile_f), rows(D, tile_b)],
        out_shape=[jax.ShapeDtypeStruct((T, D), F32), jax.ShapeDtypeStruct((T, D), F32)],
        scratch_shapes=[pltpu.VMEM((2, HEADS, HEAD_V, HEAD_K), F32), pltpu.VMEM((2, TM, KEY_DIM), F32)],
        compiler_params=_params("arbitrary", "arbitrary"),
        name="gla_scan",
    )(q, k, v, lf, q, k, v, lb)


def _gla_out_kernel(of_ref, ob_ref, gate_ref, hg_ref, w_ref, x_ref, mod_ref, g2_ref, wh_ref, wl_ref, br_ref,
                    o_ref, h_ref, slab_ref, counts_ref, cnt_ref):
    heads = []
    for h in range(HEADS):
        v0 = h * HEAD_V
        o = ob_ref[:, v0:v0 + HEAD_V] + of_ref[:, v0:v0 + HEAD_V]
        on = o * lax.rsqrt(jnp.mean(o * o, axis=-1, keepdims=True) + EPS)
        on = on * hg_ref[:, v0:v0 + HEAD_V]
        gt = gate_ref[:, v0:v0 + HEAD_V]
        heads.append((on * (gt * _sigmoid(gt))).astype(BF16))
    y = jnp.dot(jnp.concatenate(heads, axis=-1), w_ref[...], preferred_element_type=F32)
    x = x_ref[...] + mod_ref[2:3, :] * y
    o_ref[...] = x
    _route(x, TR, mod_ref, g2_ref, wh_ref, wl_ref, br_ref, h_ref, slab_ref, counts_ref, cnt_ref)


def _gla_out(o_f, o_b, gate, head_g, w, x, mods, route):
    return pl.pallas_call(
        _gla_out_kernel,
        grid=(NR,),
        in_specs=[_row_spec(D), _row_spec(D), _row_spec(D), _full_spec((1, D)), _full_spec((D, D)),
                  _row_spec(D), _mod_spec()] + _route_in_specs(),
        out_specs=[_row_spec(D)] + _route_out_specs(TR),
        out_shape=[jax.ShapeDtypeStruct((T, D), F32)] + _ROUTE_SHAPES,
        scratch_shapes=_ROUTE_SCRATCH,
        compiler_params=_params("arbitrary"),
        name="gla_out_proj",
    )(o_f, o_b, gate, head_g, w, x, mods, *route)


def _route(x, rows, mod_ref, g_ref, wh_ref, wl_ref, b_ref, h_ref, slab_ref, counts_ref, cnt_ref):
    h = _normmod(x, g_ref[...], mod_ref[3:4, :], mod_ref[4:5, :])
    for s in range(SUB):
        h_ref[pl.ds(s, rows, stride=SUB), :] = h[:, s * LANES:(s + 1) * LANES]
    lg = _dot3(h, wh_ref[...], wl_ref[...]) + b_ref[...]
    lane = lax.broadcasted_iota(jnp.int32, (rows, LANES), 1).astype(F32)
    neg = -jnp.inf
    far = float(LANES)
    is_group = (lane >= EXPERTS) & (lane < EXPERTS + GROUPS)
    glog = jnp.where(is_group, lg, neg)
    gmax = jnp.max(glog, axis=-1, keepdims=True)
    gidx = jnp.min(jnp.where(glog == gmax, lane, far), axis=-1, keepdims=True) - EXPERTS
    g_w = 1.0 / jnp.sum(jnp.exp(glog - gmax), axis=-1, keepdims=True)
    lo = gidx * PER_GROUP
    vals = jnp.where((lane >= lo) & (lane < lo + PER_GROUP), lg, neg)
    m1 = jnp.max(vals, axis=-1, keepdims=True)
    i1 = jnp.min(jnp.where(vals == m1, lane, far), axis=-1, keepdims=True)
    vals2 = jnp.where(lane == i1, neg, vals)
    m2 = jnp.max(vals2, axis=-1, keepdims=True)
    i2 = jnp.min(jnp.where(vals2 == m2, lane, far), axis=-1, keepdims=True)
    e = jnp.exp(m2 - m1)
    w1 = g_w / (1.0 + e)
    w2 = g_w * e / (1.0 + e)
    first_lo = i1 < i2
    a = jnp.minimum(i1, i2) - lo
    b = jnp.maximum(i1, i2) - lo
    cls = gidx * PAIRS + a * (PER_GROUP - 1) - a * (a - 1.0) * 0.5 + (b - a - 1.0)
    w_lo = jnp.where(first_lo, w1, w2)
    w_hi = jnp.where(first_lo, w2, w1)
    @pl.when(pl.program_id(0) == 0)
    def _():
        cnt_ref[...] = jnp.zeros_like(cnt_ref)

    onehot = jnp.where(lane == cls, 1.0, 0.0)
    r = lax.broadcasted_iota(jnp.int32, (rows, rows), 0)
    c = lax.broadcasted_iota(jnp.int32, (rows, rows), 1)
    before = jnp.where(c < r, 1.0, 0.0).astype(BF16)
    prefix = jnp.dot(before, onehot.astype(BF16), preferred_element_type=F32)
    base = cnt_ref[...]
    rank = jnp.sum(onehot * (prefix + base), axis=-1, keepdims=True)
    total = base + jnp.sum(onehot, axis=0, keepdims=True)
    cnt_ref[...] = total
    counts_ref[...] = total
    slab_ref[...] = jnp.where(lane == 0, cls, jnp.where(lane == 1, rank, jnp.where(
        lane == 2, w_lo, jnp.where(lane == 3, w_hi, 0.0))))


def _route_in_specs():
    return [_full_spec((1, D)), _full_spec((D, LANES)), _full_spec((D, LANES)), _full_spec((1, LANES))]


def _route_out_specs(rows):
    return [pl.BlockSpec((rows * SUB, LANES), lambda i: (i, 0)), _row_spec(LANES, rows),
            _full_spec((1, LANES))]


_ROUTE_SHAPES = [jax.ShapeDtypeStruct((T * SUB, LANES), F32), jax.ShapeDtypeStruct((T, LANES), F32),
                 jax.ShapeDtypeStruct((1, LANES), F32)]
_ROUTE_SCRATCH = [pltpu.VMEM((1, LANES), F32)]


def _ffn_kernel(lo_ref, hi_ref, nv_ref, tok_ref, out_ref, hs_ref, w1l_ref, w3l_ref, w2l_ref,
                w1h_ref, w3h_ref, w2h_ref, ys_ref, xg, yb, gsem, ssem):
    i = pl.program_id(0)
    slot = i % 2

    def groups(blk):
        return (nv_ref[jnp.clip(blk, 0, NB - 1)] + ROW_GROUP - 1) // ROW_GROUP

    def gather_copy(blk, sl, r):
        tok = tok_ref[blk * BM + r]
        return pltpu.make_async_copy(hs_ref.at[pl.ds(pl.multiple_of(tok * SUB, SUB), SUB)],
                                     xg.at[sl, pl.ds(r * XPITCH, SUB)], gsem.at[sl])

    def scatter_copy(blk, sl, r):
        row = out_ref[blk * BM + r]
        return pltpu.make_async_copy(yb.at[sl, pl.ds(r * YPITCH, YROWS)],
                                     ys_ref.at[pl.ds(pl.multiple_of(row * YROWS, YROWS), YROWS)],
                                     ssem.at[sl])

    def start_block(copy, blk, sl, priority):
        def group(gi, carry):
            for u in range(ROW_GROUP):
                copy(blk, sl, gi * ROW_GROUP + u).start(priority=priority)
            return carry
        lax.fori_loop(0, groups(blk), group, 0)

    def wait_gather(blk, sl):
        n = groups(blk) * (ROW_GROUP * SUB)

        @pl.when(n > 0)
        def _():
            pltpu.make_async_copy(hs_ref.at[pl.ds(0, n)], xg.at[sl, pl.ds(0, n)], gsem.at[sl]).wait()

    def wait_scatter(blk, sl):
        n = groups(blk) * (ROW_GROUP * YROWS)

        @pl.when(n > 0)
        def _():
            pltpu.make_async_copy(yb.at[sl, pl.ds(0, n)], ys_ref.at[pl.ds(0, n)], ssem.at[sl]).wait()

    @pl.when(i == 0)
    def _():
        xg[...] = jnp.zeros_like(xg)
        yb[...] = jnp.zeros_like(yb)
        for sl in range(2):
            spare = pltpu.make_async_copy(yb.at[sl, pl.ds(0, BM * YROWS)],
                                          ys_ref.at[pl.ds((T + sl * BM) * YROWS, BM * YROWS)], ssem.at[sl])
            spare.start()
            spare.wait()
        start_block(gather_copy, 0, 0, 0)

    @pl.when(i + 1 < NB)
    def _():
        start_block(gather_copy, i + 1, 1 - slot, 0)

    wait_gather(i, slot)

    @pl.when(i >= 2)
    def _():
        wait_scatter(i - 2, slot)

    def expert(xb, w1_ref, w3_ref, w2_ref):
        a = jnp.dot(xb, w1_ref[...], preferred_element_type=F32)
        b = jnp.dot(xb, w3_ref[...], preferred_element_type=F32)
        hh = (a * _sigmoid(a) * b).astype(BF16)
        return jnp.dot(hh, w2_ref[...], preferred_element_type=F32)

    def run_experts(rows):
        xb = jnp.concatenate([xg[slot, pl.ds(s, rows, stride=XPITCH), :] for s in range(SUB)],
                             axis=-1).astype(BF16)
        y_lo = expert(xb, w1l_ref, w3l_ref, w2l_ref)
        for s in range(SUB):
            yb[slot, pl.ds(s, rows, stride=YPITCH), :] = y_lo[:, s * LANES:(s + 1) * LANES]
        y_hi = expert(xb, w1h_ref, w3h_ref, w2h_ref)
        for s in range(SUB):
            yb[slot, pl.ds(SUB + s, rows, stride=YPITCH), :] = y_hi[:, s * LANES:(s + 1) * LANES]

    @pl.when(nv_ref[i] > BM // 2)
    def _():
        run_experts(BM)

    @pl.when((nv_ref[i] > 0) & (nv_ref[i] <= BM // 2))
    def _():
        run_experts(BM // 2)

    @pl.when(nv_ref[i] > 0)
    def _():
        start_block(scatter_copy, i, slot, 1)

    @pl.when(i == NB - 1)
    def _():
        wait_scatter(NB - 2, 1 - slot)
        wait_scatter(NB - 1, slot)


def _ffn(blk_lo, blk_hi, n_valid, tok_buf, out_buf, hs, w1, w3, w2, layer):
    def w_spec(shape, which):
        return pl.BlockSpec((None, None) + shape,
                            lambda i, lo, hi, nv, tok, out: (layer, (lo, hi)[which][i], 0, 0))

    return pl.pallas_call(
        _ffn_kernel,
        grid_spec=pltpu.PrefetchScalarGridSpec(
            num_scalar_prefetch=5,
            grid=(NB,),
            in_specs=[pl.BlockSpec(memory_space=pl.ANY),
                      w_spec((D, FF), 0), w_spec((D, FF), 0), w_spec((FF, D), 0),
                      w_spec((D, FF), 1), w_spec((D, FF), 1), w_spec((FF, D), 1)],
            out_specs=pl.BlockSpec(memory_space=pl.ANY),
            scratch_shapes=[pltpu.VMEM((2, BM * XPITCH, LANES), F32), pltpu.VMEM((2, BM * YPITCH, LANES), F32),
                            pltpu.SemaphoreType.DMA((2,)), pltpu.SemaphoreType.DMA((2,))]),
        out_shape=jax.ShapeDtypeStruct(((T + 2 * BM) * YROWS, LANES), F32),
        compiler_params=_params("arbitrary"),
        name="moe_expert_ffn",
    )(blk_lo, blk_hi, n_valid, tok_buf, out_buf, hs, w1, w3, w2, w1, w3, w2)


def _combine_kernel(x_ref, y_ref, slab_ref, mod_ref, fg_ref, o_ref):
    x = _moe_mix(x_ref, y_ref, slab_ref, mod_ref, TR)
    o_ref[...] = x * lax.rsqrt(jnp.mean(x * x, axis=-1, keepdims=True) + EPS) * fg_ref[...]


def _final_combine(x, ys, slab, mods, final_g):
    first = T_CTX // TR

    def rows(height, width):
        return pl.BlockSpec((height, width), lambda i: (i + first, 0))

    return pl.pallas_call(
        _combine_kernel,
        grid=(NR - first,),
        in_specs=[rows(TR, D), rows(TR * YROWS, LANES), rows(TR, LANES),
                  _mod_spec(TR, first), _full_spec((1, D))],
        out_specs=_row_spec(D),
        out_shape=jax.ShapeDtypeStruct((T - first * TR, D), F32),
        compiler_params=_params("arbitrary"),
        name="moe_combine",
    )(x, ys, slab, mods, final_g)


def _class_experts():
    lo, hi = [], []
    for g in range(GROUPS):
        for a in range(PER_GROUP):
            for b in range(a + 1, PER_GROUP):
                lo.append(g * PER_GROUP + a)
                hi.append(g * PER_GROUP + b)
    return jnp.array(lo, jnp.int32), jnp.array(hi, jnp.int32)


def _dispatch_plan(slab, counts):
    cls = slab[:, 0].astype(jnp.int32)
    rank = slab[:, 1].astype(jnp.int32)
    counts = counts[0, :NCLS].astype(jnp.int32)
    padded = ((counts + BM - 1) // BM) * BM
    pend = jnp.cumsum(padded)
    pstart = pend - padded
    ids = jnp.arange(NCLS, dtype=jnp.int32)

    def lookup(table, idx):
        return jnp.sum(jnp.where(idx[:, None] == ids[None, :], table[None, :], 0), axis=1)

    dest = lookup(pstart, cls) + rank
    tok_buf = (jnp.arange(P, dtype=jnp.int32) % T).at[dest].set(jnp.arange(T, dtype=jnp.int32))
    blk0 = jnp.arange(NB, dtype=jnp.int32) * BM
    blk_c = jnp.minimum(jnp.sum((pend[None, :] <= blk0[:, None]).astype(jnp.int32), axis=1), NCLS - 1)
    n_valid = jnp.clip(lookup(counts, blk_c) - (blk0 - lookup(pstart, blk_c)), 0, BM)
    cls_lo, cls_hi = _class_experts()
    slot = jnp.arange(P, dtype=jnp.int32)
    spare = T + ((slot // BM) % 2) * BM + slot % BM
    out_buf = jnp.where(slot % BM < jnp.repeat(n_valid, BM), tok_buf, spare)
    return tok_buf, out_buf, lookup(cls_lo, blk_c), lookup(cls_hi, blk_c), n_valid


def _moe(hs, slab, counts, w1, w3, w2, layer):
    tok_buf, out_buf, blk_lo, blk_hi, n_valid = _dispatch_plan(slab, counts)
    return _ffn(blk_lo, blk_hi, n_valid, tok_buf, out_buf, hs, w1, w3, w2, layer)


def _cast_kernel(w_ref, o_ref):
    o_ref[...] = w_ref[...].astype(BF16)


def _to_bf16(w, rows):
    n, cols = w.shape
    return pl.pallas_call(
        _cast_kernel,
        grid=(n // rows,),
        in_specs=[pl.BlockSpec((rows, cols), lambda i: (i, 0))],
        out_specs=pl.BlockSpec((rows, cols), lambda i: (i, 0)),
        out_shape=jax.ShapeDtypeStruct((n, cols), BF16),
        compiler_params=_params("arbitrary"),
        name="weights_to_bf16",
    )(w)


def _sincos_2d(rows, width, dim):
    quarter = dim // 4
    omega = 1.0 / (POS_BASE ** (jnp.arange(quarter, dtype=F32) / quarter))
    ar = jnp.arange(rows, dtype=F32)[:, None] * omega
    ac = jnp.arange(width, dtype=F32)[:, None] * omega
    row_part = jnp.concatenate([jnp.sin(ar), jnp.cos(ar)], axis=-1)
    col_part = jnp.concatenate([jnp.sin(ac), jnp.cos(ac)], axis=-1)
    row_part = jnp.broadcast_to(row_part[:, None, :], (rows, width, dim // 2))
    col_part = jnp.broadcast_to(col_part[None, :, :], (rows, width, dim // 2))
    return jnp.concatenate([row_part, col_part], axis=-1).reshape(rows * width, dim)


def kernel(x, c, ctx, c_ctx, ada_w, ada_b, norm_g, final_g, conv_pw1_w, conv_pw1_b, conv_dw_w, conv_dw_b, conv_ln_g, conv_ln_b, conv_pw2_w, conv_pw2_b, gla_w_in, gla_gk_up, gla_gk_b, gla_head_g, gla_w_o, moe_wg, moe_bg, moe_wr, moe_br, moe_w1, moe_w3, moe_w2):
    assert x.shape == (BATCH, SEQ, D) and ctx.shape == (BATCH, CTX_LEN, D)
    pos = _sincos_2d(SEQ // GRID_W, GRID_W, D)
    xt = jnp.concatenate([ctx.reshape(T_CTX, D), (x + pos[None]).reshape(T_LAT, D)], axis=0)

    c8 = jnp.concatenate([c_ctx[None], c, jnp.zeros((8 - 1 - BATCH, D), F32)], axis=0)
    mods = _ada_mods(c8, ada_w, ada_b)
    fg = final_g.reshape(1, D)
    w1b = _to_bf16(moe_w1.reshape(-1, FF), 4 * D).reshape(moe_w1.shape)
    w3b = _to_bf16(moe_w3.reshape(-1, FF), 4 * D).reshape(moe_w3.shape)
    w2b = _to_bf16(moe_w2.reshape(-1, D), 4 * FF).reshape(moe_w2.shape)

    pending = None
    for i in range(DEPTH):
        j = i // 2
        g_mix = norm_g[i, 0].reshape(1, D)
        wr_cat = jnp.concatenate([moe_wr[i], moe_wg[i],
                                  jnp.zeros((D, LANES - EXPERTS - GROUPS), F32)], axis=1)
        br_cat = jnp.concatenate([moe_br[i], moe_bg[i],
                                  jnp.zeros((LANES - EXPERTS - GROUPS,), F32)]).reshape(1, LANES)
        route = (norm_g[i, 1].reshape(1, D),) + _split_bf16(wr_cat) + (br_cat,)
        if i % 2 == 0:
            xt, u = _pw1(xt, pending, g_mix, mods[i], conv_pw1_w[j].astype(BF16),
                         conv_pw1_b[j].reshape(1, 2 * D))
            dww = jnp.concatenate([conv_dw_w[j], jnp.zeros((1, D), F32)], axis=0)
            xt, hs, slab, counts = _conv(u, xt, mods[i], dww, conv_dw_b[j].reshape(1, D),
                                         conv_ln_g[j].reshape(1, D), conv_ln_b[j].reshape(1, D),
                                         conv_pw2_w[j].astype(BF16), conv_pw2_b[j].reshape(1, D), route)
        else:
            w_in = _to_bf16(gla_w_in[j], TM)
            w_codes = jnp.pad(gla_w_in[j][:, GLA_IN - 2 * GATE_RANK:],
                              ((0, 0), (0, LANES - 2 * GATE_RANK))).astype(BF16)
            gk = jnp.zeros((LANES, 2 * KEY_DIM), F32)
            gk = gk.at[0:GATE_RANK, 0:KEY_DIM].set(gla_gk_up[j, 0])
            gk = gk.at[GATE_RANK:2 * GATE_RANK, KEY_DIM:].set(gla_gk_up[j, 1])
            gkb = gla_gk_b[j].reshape(1, 2 * KEY_DIM)
            xt, (q, k, v, gate, lf, lb) = _gla_proj(xt, pending, g_mix, mods[i], w_in, w_codes, gk, gkb)
            o_f, o_b = _gla_scan(q, k, v, lf, lb)
            xt, hs, slab, counts = _gla_out(o_f, o_b, gate, gla_head_g[j].reshape(1, D),
                                            gla_w_o[j].astype(BF16), xt, mods[i], route)
        ys = _moe(hs, slab, counts, w1b, w3b, w2b, i)
        pending = (ys, slab, mods[i])
    return _final_combine(xt, *pending, fg).reshape(BATCH, SEQ, D)
```

```python
import functools

import jax
import jax.numpy as jnp
from jax import lax
from jax.experimental import pallas as pl
from jax.experimental.pallas import tpu as pltpu

F32 = jnp.float32
BF16 = jnp.bfloat16
HIGHEST = lax.Precision.HIGHEST

D = 1024
BATCH = 2
SEQ = 16384
CTX_LEN = 256
DEPTH = 4
GRID_W = 64
POS_BASE = 10000.0
EPS = 1e-6
CONV_WIDTH = 31
CONV_PAD = CONV_WIDTH // 2
HEADS = 4
KEY_DIM = D // 2
HEAD_K = KEY_DIM // HEADS
HEAD_V = D // HEADS
GATE_RANK = 16
GATE_NORM = 16.0
CHUNK = 64
GROUPS = 4
PER_GROUP = 8
EXPERTS = GROUPS * PER_GROUP
FF = D // 2

TM = 256
T_CTX = BATCH * CTX_LEN
T_LAT = BATCH * SEQ
T = T_CTX + T_LAT
NT = T // TM
TR = 512
NR = T // TR
CTX_TILES = T_CTX // TM
LAT_TILES = SEQ // TM
HALO = 16
CONV_ROWS = 128
CONV_WIN = CONV_ROWS + 2 * HALO
SUBLANES = 8
BM = 256
ROW_GROUP = 16
PAIRS = PER_GROUP * (PER_GROUP - 1) // 2
NCLS = GROUPS * PAIRS
NB = T // BM + NCLS
P = NB * BM
LANES = 128
SUB = D // LANES
YROWS = 2 * SUB
XPITCH = 12
YPITCH = 20
GLA_IN = 2 * KEY_DIM + 2 * D + 2 * GATE_RANK
VMEM_LIMIT = 48 * 1024 * 1024

_NT_DIMS = (((1,), (1,)), ((), ()))
_TN_DIMS = (((0,), (0,)), ((), ()))


def _params(*sem):
    return pltpu.CompilerParams(dimension_semantics=sem, vmem_limit_bytes=VMEM_LIMIT)


def _seg(i, rows):
    ctx_tiles = T_CTX // rows
    return jnp.where(i < ctx_tiles, 0, jnp.where(i < ctx_tiles + SEQ // rows, 1, 2))


def _mod_spec(rows=TR, first=0):
    return pl.BlockSpec((None, 6, D), lambda i: (_seg(i + first, rows), 0, 0))


def _row_spec(width, rows=TR):
    return pl.BlockSpec((rows, width), lambda i: (i, 0))


def _full_spec(shape):
    return pl.BlockSpec(shape, lambda i: (0,) * len(shape))


def _normmod(x, g, shift, scale):
    ms = jnp.mean(x * x, axis=-1, keepdims=True)
    return (x * lax.rsqrt(ms + EPS) * g) * (1.0 + scale) + shift


def _sigmoid(x):
    return 1.0 / (1.0 + jnp.exp(-x))


def _split_bf16(a):
    hi = a.astype(BF16)
    return hi, (a - hi.astype(F32)).astype(BF16)


def _dot3(a, w_hi, w_lo):
    a_hi, a_lo = _split_bf16(a)
    return (jnp.dot(a_hi, w_hi, preferred_element_type=F32)
            + jnp.dot(a_lo, w_hi, preferred_element_type=F32)
            + jnp.dot(a_hi, w_lo, preferred_element_type=F32))


def _moe_mix(x_ref, y_ref, slab_ref, mod_ref, rows):
    w_lo = slab_ref[:, 2:3]
    w_hi = slab_ref[:, 3:4]
    y = jnp.concatenate([w_lo * y_ref[pl.ds(s, rows, stride=YROWS), :]
                         + w_hi * y_ref[pl.ds(SUB + s, rows, stride=YROWS), :] for s in range(SUB)], axis=-1)
    return x_ref[...] + mod_ref[5:6, :] * y


def _pending_specs(rows):
    return [pl.BlockSpec((rows * YROWS, LANES), lambda i: (i, 0)), _row_spec(LANES, rows), _mod_spec(rows)]


def _ada_kernel(c_ref, w_ref, b_ref, o_ref):
    c = c_ref[...]
    s = c * _sigmoid(c)
    o_ref[...] = jnp.dot(s, w_ref[...], precision=HIGHEST, preferred_element_type=F32) + b_ref[...]


def _ada_mods(c8, ada_w, ada_b):
    out = pl.pallas_call(
        _ada_kernel,
        grid=(DEPTH, 6),
        in_specs=[pl.BlockSpec((8, D), lambda l, n: (0, 0)),
                  pl.BlockSpec((None, D, D), lambda l, n: (l, 0, n)),
                  pl.BlockSpec((None, 1, D), lambda l, n: (l, 0, n))],
        out_specs=pl.BlockSpec((None, 8, D), lambda l, n: (l, 0, n)),
        out_shape=jax.ShapeDtypeStruct((DEPTH, 8, 6 * D), F32),
        compiler_params=_params("arbitrary", "arbitrary"),
        name="ada_mods",
    )(c8, ada_w, ada_b.reshape(DEPTH, 1, 6 * D))
    return out[:, :3].reshape(DEPTH, 3, 6, D)


def _pw1_kernel(*refs, pending):
    if pending:
        x_ref, y_ref, slab_ref, modp_ref, g_ref, mod_ref, w_ref, b_ref, xo_ref, u_ref = refs
        x = _moe_mix(x_ref, y_ref, slab_ref, modp_ref, TR)
        xo_ref[...] = x
    else:
        x_ref, g_ref, mod_ref, w_ref, b_ref, u_ref = refs
        x = x_ref[...]
    h = _normmod(x, g_ref[...], mod_ref[0:1, :], mod_ref[1:2, :]).astype(BF16)
    a = jnp.dot(h, w_ref[:, :D], preferred_element_type=F32) + b_ref[:, :D]
    g = jnp.dot(h, w_ref[:, D:], preferred_element_type=F32) + b_ref[:, D:]
    u_ref[...] = a * _sigmoid(g)


def _pw1(x, pending, g, mods, w, b):
    row = jax.ShapeDtypeStruct((T, D), F32)
    out = pl.pallas_call(
        functools.partial(_pw1_kernel, pending=pending is not None),
        grid=(NR,),
        in_specs=[_row_spec(D)] + (_pending_specs(TR) if pending else [])
        + [_full_spec((1, D)), _mod_spec(), _full_spec((D, 2 * D)), _full_spec((1, 2 * D))],
        out_specs=[_row_spec(D)] * 2 if pending else _row_spec(D),
        out_shape=[row, row] if pending else row,
        compiler_params=_params("arbitrary"),
        name="conv_pw1_glu",
    )(x, *(pending or ()), g, mods, w, b)
    return out if pending else (x, out)


def _conv_kernel(prev_ref, cur_ref, next_ref, x_ref, mod_ref, dww_ref, dwb_ref, lng_ref, lnb_ref,
                 w_ref, b_ref, g2_ref, wh_ref, wl_ref, br_ref, o_ref, h_ref, slab_ref, counts_ref,
                 ext_ref, conv_ref, cnt_ref):
    i = pl.program_id(0)
    lat0 = CTX_TILES
    lat1 = CTX_TILES + LAT_TILES
    is_start = (i < CTX_TILES) | (i == lat0) | (i == lat1)
    is_end = (i < CTX_TILES) | (i == lat1 - 1) | (i == NT - 1)
    ext_ref[0:HALO, :] = jnp.where(is_start, 0.0, prev_ref[...])
    ext_ref[HALO:HALO + TM, :] = cur_ref[...]
    ext_ref[HALO + TM:, :] = jnp.where(is_end, 0.0, next_ref[...])

    def lane_block(j, carry):
        lo = pl.multiple_of(j * LANES, LANES)
        for rb in range(TM // CONV_ROWS):
            r0 = rb * CONV_ROWS
            win = ext_ref[r0:r0 + CONV_WIN, pl.ds(lo, LANES)]
            acc = jnp.broadcast_to(dwb_ref[:, pl.ds(lo, LANES)], (CONV_ROWS, LANES))
            for rho in range(SUBLANES):
                shifted = win if rho == 0 else pltpu.roll(win, shift=CONV_WIN - rho, axis=0)
                for k in range(CONV_WIDTH):
                    off = k + HALO - CONV_PAD - rho
                    if off % SUBLANES == 0:
                        acc = acc + dww_ref[k:k + 1, pl.ds(lo, LANES)] * shifted[off:off + CONV_ROWS]
            conv_ref[r0:r0 + CONV_ROWS, pl.ds(lo, LANES)] = acc
        return carry

    lax.fori_loop(0, D // LANES, lane_block, 0)
    v = conv_ref[...]
    mu = jnp.mean(v, axis=-1, keepdims=True)
    vc = v - mu
    var = jnp.mean(vc * vc, axis=-1, keepdims=True)
    y = vc * lax.rsqrt(var + EPS) * lng_ref[...] + lnb_ref[...]
    y = (y * _sigmoid(y)).astype(BF16)
    out = jnp.dot(y, w_ref[...], preferred_element_type=F32) + b_ref[...]
    x = x_ref[...] + mod_ref[2:3, :] * out
    o_ref[...] = x
    _route(x, TM, mod_ref, g2_ref, wh_ref, wl_ref, br_ref, h_ref, slab_ref, counts_ref, cnt_ref)


def _conv(u, x, mods, dww, dwb, lng, lnb, w, b, route):
    per = TM // HALO
    return pl.pallas_call(
        _conv_kernel,
        grid=(NT,),
        in_specs=[pl.BlockSpec((HALO, D), lambda i: (jnp.maximum(i * per - 1, 0), 0)),
                  _row_spec(D, TM),
                  pl.BlockSpec((HALO, D), lambda i: (jnp.minimum((i + 1) * per, T // HALO - 1), 0)),
                  _row_spec(D, TM), _mod_spec(TM),
                  _full_spec((CONV_WIDTH + 1, D)), _full_spec((1, D)), _full_spec((1, D)),
                  _full_spec((1, D)), _full_spec((D, D)), _full_spec((1, D))] + _route_in_specs(),
        out_specs=[_row_spec(D, TM)] + _route_out_specs(TM),
        out_shape=[jax.ShapeDtypeStruct((T, D), F32)] + _ROUTE_SHAPES,
        scratch_shapes=[pltpu.VMEM((TM + 2 * HALO, D), F32), pltpu.VMEM((TM, D), F32)] + _ROUTE_SCRATCH,
        compiler_params=_params("arbitrary"),
        name="conv_dw_ln_pw2",
    )(u, u, u, x, mods, dww, dwb, lng, lnb, w, b, *route)


def _log_sigmoid(z):
    return jnp.minimum(z, 0.0) - jnp.log(1.0 + jnp.exp(-jnp.abs(z)))


def _gla_proj_kernel(*refs, pending):
    if pending:
        x_ref, y_ref, slab_ref, modp_ref = refs[:4]
        refs = refs[4:]
        xo_ref = refs[7]
        refs = refs[:7] + refs[8:]
        x = _moe_mix(x_ref, y_ref, slab_ref, modp_ref, TM)
        xo_ref[...] = x
    else:
        x = refs[0][...]
        refs = refs[1:]
    (g_ref, mod_ref, w_ref, wc_ref, gkh_ref, gkl_ref, gkb_ref,
     q_ref, k_ref, v_ref, gate_ref, lf_ref, lb_ref) = refs
    h = _normmod(x, g_ref[...], mod_ref[0:1, :], mod_ref[1:2, :]).astype(BF16)
    kd = KEY_DIM
    q_ref[...] = jnp.dot(h, w_ref[:, 0:kd], preferred_element_type=F32) * (HEAD_K ** -0.5)
    k_ref[...] = jnp.dot(h, w_ref[:, kd:2 * kd], preferred_element_type=F32)
    v_ref[...] = jnp.dot(h, w_ref[:, 2 * kd:2 * kd + D], preferred_element_type=F32).astype(BF16)
    gate_ref[...] = jnp.dot(h, w_ref[:, 2 * kd + D:2 * kd + 2 * D], preferred_element_type=F32)
    codes = jnp.dot(h, wc_ref[...], preferred_element_type=F32)
    z = _dot3(codes, gkh_ref[...], gkl_ref[...]) + gkb_ref[...]
    ls = _log_sigmoid(z) * (1.0 / GATE_NORM)
    lf_ref[...] = ls[:, :kd]
    lb_ref[...] = ls[:, kd:]


def _gla_proj(x, pending, g, mods, w, w_codes, gk, gkb):
    gk_hi, gk_lo = _split_bf16(gk)
    shapes = [jax.ShapeDtypeStruct((T, KEY_DIM), F32), jax.ShapeDtypeStruct((T, KEY_DIM), F32),
              jax.ShapeDtypeStruct((T, D), BF16), jax.ShapeDtypeStruct((T, D), F32),
              jax.ShapeDtypeStruct((T, KEY_DIM), F32), jax.ShapeDtypeStruct((T, KEY_DIM), F32)]
    out_specs = [_row_spec(KEY_DIM, TM), _row_spec(KEY_DIM, TM), _row_spec(D, TM), _row_spec(D, TM),
                 _row_spec(KEY_DIM, TM), _row_spec(KEY_DIM, TM)]
    if pending:
        shapes = [jax.ShapeDtypeStruct((T, D), F32)] + shapes
        out_specs = [_row_spec(D, TM)] + out_specs
    out = pl.pallas_call(
        functools.partial(_gla_proj_kernel, pending=pending is not None),
        grid=(NT,),
        in_specs=[_row_spec(D, TM)] + (_pending_specs(TM) if pending else [])
        + [_full_spec((1, D)), _mod_spec(TM), _full_spec((D, GLA_IN)), _full_spec((D, LANES)),
           _full_spec((LANES, 2 * KEY_DIM)), _full_spec((LANES, 2 * KEY_DIM)),
           _full_spec((1, 2 * KEY_DIM))],
        out_specs=out_specs,
        out_shape=shapes,
        compiler_params=_params("arbitrary"),
        name="gla_proj",
    )(x, *(pending or ()), g, mods, w, w_codes, gk_hi, gk_lo, gkb)
    return (out[0], out[1:]) if pending else (x, out)


def _scan_tile(q_ref, k_ref, v_ref, lg_ref, o_ref, st_ref, cum_ref, reverse):
    r = lax.broadcasted_iota(jnp.int32, (TM, TM), 0)
    c = lax.broadcasted_iota(jnp.int32, (TM, TM), 1)
    shift = CHUNK.bit_length() - 1
    same = (r >> shift) == (c >> shift)
    tri = jnp.where(same & ((c >= r) if reverse else (c <= r)), 1.0, 0.0).astype(BF16)
    lg = lg_ref[...]
    hi = lg.astype(BF16)
    r1 = lg - hi.astype(F32)
    mid = r1.astype(BF16)
    lo = (r1 - mid.astype(F32)).astype(BF16)
    cum_ref[...] = (jnp.dot(tri, hi, preferred_element_type=F32)
                    + jnp.dot(tri, mid, preferred_element_type=F32)
                    + jnp.dot(tri, lo, preferred_element_type=F32))

    rr = lax.broadcasted_iota(jnp.int32, (CHUNK, CHUNK), 0)
    cc = lax.broadcasted_iota(jnp.int32, (CHUNK, CHUNK), 1)
    mask = (cc >= rr) if reverse else (cc <= rr)
    i_last = 0 if reverse else CHUNK - 1
    i_ref = CHUNK // 2 - 1 if reverse else CHUNK // 2
    chunks = range(TM // CHUNK)
    for ci in (reversed(chunks) if reverse else chunks):
        r0 = ci * CHUNK
        for h in range(HEADS):
            k0, v0 = h * HEAD_K, h * HEAD_V
            cm = cum_ref[r0:r0 + CHUNK, k0:k0 + HEAD_K]
            last = cum_ref[r0 + i_last:r0 + i_last + 1, k0:k0 + HEAD_K]
            ref = cum_ref[r0 + i_ref:r0 + i_ref + 1, k0:k0 + HEAD_K]
            qc = q_ref[r0:r0 + CHUNK, k0:k0 + HEAD_K]
            kc = k_ref[r0:r0 + CHUNK, k0:k0 + HEAD_K]
            vc = v_ref[r0:r0 + CHUNK, v0:v0 + HEAD_V]
            qr = (qc * jnp.exp(cm - ref)).astype(BF16)
            kr = (kc * jnp.exp(ref - cm)).astype(BF16)
            att = lax.dot_general(qr, kr, _NT_DIMS, preferred_element_type=F32)
            att = jnp.where(mask, att, 0.0).astype(BF16)
            qd = (qc * jnp.exp(cm)).astype(BF16)
            st = st_ref[h]
            o = (jnp.dot(att, vc, preferred_element_type=F32)
                 + lax.dot_general(qd, st.astype(BF16), _NT_DIMS, preferred_element_type=F32))
            kdec = (kc * jnp.exp(last - cm)).astype(BF16)
            st_ref[h] = st * jnp.exp(last) + lax.dot_general(vc, kdec, _TN_DIMS,
                                                             preferred_element_type=F32)
            o_ref[r0:r0 + CHUNK, v0:v0 + HEAD_V] = o


def _gla_scan_kernel(qf_ref, kf_ref, vf_ref, lf_ref, qb_ref, kb_ref, vb_ref, lb_ref,
                     of_ref, ob_ref, st_ref, cum_ref):
    @pl.when(pl.program_id(1) == 0)
    def _():
        st_ref[...] = jnp.zeros_like(st_ref)

    _scan_tile(qf_ref, kf_ref, vf_ref, lf_ref, of_ref, st_ref.at[0], cum_ref.at[0], False)
    _scan_tile(qb_ref, kb_ref, vb_ref, lb_ref, ob_ref, st_ref.at[1], cum_ref.at[1], True)


def _gla_scan(q, k, v, lf, lb):
    def tile_f(b, j):
        return jnp.where(j == 0, b, CTX_TILES + LAT_TILES * b + j - 1)

    def tile_b(b, j):
        return jnp.where(j == 0, b, CTX_TILES + LAT_TILES * (b + 1) - j)

    def rows(width, tile):
        return pl.BlockSpec((TM, width), lambda b, j: (tile(b, j), 0))

    def scan_specs(tile):
        return [rows(KEY_DIM, tile), rows(KEY_DIM, tile), rows(D, tile), rows(KEY_DIM, tile)]

    return pl.pallas_call(
        _gla_scan_kernel,
        grid=(BATCH, 1 + LAT_TILES),
        in_specs=scan_specs(tile_f) + scan_specs(tile_b),
        out_specs=[rows(D, tile_f), rows(D, tile_b)],
        out_shape=[jax.ShapeDtypeStruct((T, D), F32), jax.ShapeDtypeStruct((T, D), F32)],
        scratch_shapes=[pltpu.VMEM((2, HEADS, HEAD_V, HEAD_K), F32), pltpu.VMEM((2, TM, KEY_DIM), F32)],
        compiler_params=_params("arbitrary", "arbitrary"),
        name="gla_scan",
    )(q, k, v, lf, q, k, v, lb)


def _gla_out_kernel(of_ref, ob_ref, gate_ref, hg_ref, w_ref, x_ref, mod_ref, g2_ref, wh_ref, wl_ref, br_ref,
                    o_ref, h_ref, slab_ref, counts_ref, cnt_ref):
    heads = []
    for h in range(HEADS):
        v0 = h * HEAD_V
        o = ob_ref[:, v0:v0 + HEAD_V] + of_ref[:, v0:v0 + HEAD_V]
        on = o * lax.rsqrt(jnp.mean(o * o, axis=-1, keepdims=True) + EPS)
        on = on * hg_ref[:, v0:v0 + HEAD_V]
        gt = gate_ref[:, v0:v0 + HEAD_V]
        heads.append((on * (gt * _sigmoid(gt))).astype(BF16))
    y = jnp.dot(jnp.concatenate(heads, axis=-1), w_ref[...], preferred_element_type=F32)
    x = x_ref[...] + mod_ref[2:3, :] * y
    o_ref[...] = x
    _route(x, TR, mod_ref, g2_ref, wh_ref, wl_ref, br_ref, h_ref, slab_ref, counts_ref, cnt_ref)


def _gla_out(o_f, o_b, gate, head_g, w, x, mods, route):
    return pl.pallas_call(
        _gla_out_kernel,
        grid=(NR,),
        in_specs=[_row_spec(D), _row_spec(D), _row_spec(D), _full_spec((1, D)), _full_spec((D, D)),
                  _row_spec(D), _mod_spec()] + _route_in_specs(),
        out_specs=[_row_spec(D)] + _route_out_specs(TR),
        out_shape=[jax.ShapeDtypeStruct((T, D), F32)] + _ROUTE_SHAPES,
        scratch_shapes=_ROUTE_SCRATCH,
        compiler_params=_params("arbitrary"),
        name="gla_out_proj",
    )(o_f, o_b, gate, head_g, w, x, mods, *route)


def _route(x, rows, mod_ref, g_ref, wh_ref, wl_ref, b_ref, h_ref, slab_ref, counts_ref, cnt_ref):
    h = _normmod(x, g_ref[...], mod_ref[3:4, :], mod_ref[4:5, :])
    for s in range(SUB):
        h_ref[pl.ds(s, rows, stride=SUB), :] = h[:, s * LANES:(s + 1) * LANES]
    lg = _dot3(h, wh_ref[...], wl_ref[...]) + b_ref[...]
    lane = lax.broadcasted_iota(jnp.int32, (rows, LANES), 1).astype(F32)
    neg = -jnp.inf
    far = float(LANES)
    is_group = (lane >= EXPERTS) & (lane < EXPERTS + GROUPS)
    glog = jnp.where(is_group, lg, neg)
    gmax = jnp.max(glog, axis=-1, keepdims=True)
    gidx = jnp.min(jnp.where(glog == gmax, lane, far), axis=-1, keepdims=True) - EXPERTS
    g_w = 1.0 / jnp.sum(jnp.exp(glog - gmax), axis=-1, keepdims=True)
    lo = gidx * PER_GROUP
    vals = jnp.where((lane >= lo) & (lane < lo + PER_GROUP), lg, neg)
    m1 = jnp.max(vals, axis=-1, keepdims=True)
    i1 = jnp.min(jnp.where(vals == m1, lane, far), axis=-1, keepdims=True)
    vals2 = jnp.where(lane == i1, neg, vals)
    m2 = jnp.max(vals2, axis=-1, keepdims=True)
    i2 = jnp.min(jnp.where(vals2 == m2, lane, far), axis=-1, keepdims=True)
    e = jnp.exp(m2 - m1)
    w1 = g_w / (1.0 + e)
    w2 = g_w * e / (1.0 + e)
    first_lo = i1 < i2
    a = jnp.minimum(i1, i2) - lo
    b = jnp.maximum(i1, i2) - lo
    cls = gidx * PAIRS + a * (PER_GROUP - 1) - a * (a - 1.0) * 0.5 + (b - a - 1.0)
    w_lo = jnp.where(first_lo, w1, w2)
    w_hi = jnp.where(first_lo, w2, w1)
    @pl.when(pl.program_id(0) == 0)
    def _():
        cnt_ref[...] = jnp.zeros_like(cnt_ref)

    onehot = jnp.where(lane == cls, 1.0, 0.0)
    r = lax.broadcasted_iota(jnp.int32, (rows, rows), 0)
    c = lax.broadcasted_iota(jnp.int32, (rows, rows), 1)
    before = jnp.where(c < r, 1.0, 0.0).astype(BF16)
    prefix = jnp.dot(before, onehot.astype(BF16), preferred_element_type=F32)
    base = cnt_ref[...]
    rank = jnp.sum(onehot * (prefix + base), axis=-1, keepdims=True)
    total = base + jnp.sum(onehot, axis=0, keepdims=True)
    cnt_ref[...] = total
    counts_ref[...] = total
    slab_ref[...] = jnp.where(lane == 0, cls, jnp.where(lane == 1, rank, jnp.where(
        lane == 2, w_lo, jnp.where(lane == 3, w_hi, 0.0))))


def _route_in_specs():
    return [_full_spec((1, D)), _full_spec((D, LANES)), _full_spec((D, LANES)), _full_spec((1, LANES))]


def _route_out_specs(rows):
    return [pl.BlockSpec((rows * SUB, LANES), lambda i: (i, 0)), _row_spec(LANES, rows),
            _full_spec((1, LANES))]


_ROUTE_SHAPES = [jax.ShapeDtypeStruct((T * SUB, LANES), F32), jax.ShapeDtypeStruct((T, LANES), F32),
                 jax.ShapeDtypeStruct((1, LANES), F32)]
_ROUTE_SCRATCH = [pltpu.VMEM((1, LANES), F32)]


def _ffn_kernel(lo_ref, hi_ref, nv_ref, tok_ref, out_ref, hs_ref, w1l_ref, w3l_ref, w2l_ref,
                w1h_ref, w3h_ref, w2h_ref, ys_ref, xg, yb, gsem, ssem):
    i = pl.program_id(0)
    slot = i % 2

    def groups(blk):
        return (nv_ref[jnp.clip(blk, 0, NB - 1)] + ROW_GROUP - 1) // ROW_GROUP

    def gather_copy(blk, sl, r):
        tok = tok_ref[blk * BM + r]
        return pltpu.make_async_copy(hs_ref.at[pl.ds(pl.multiple_of(tok * SUB, SUB), SUB)],
                                     xg.at[sl, pl.ds(r * XPITCH, SUB)], gsem.at[sl])

    def scatter_copy(blk, sl, r):
        row = out_ref[blk * BM + r]
        return pltpu.make_async_copy(yb.at[sl, pl.ds(r * YPITCH, YROWS)],
                                     ys_ref.at[pl.ds(pl.multiple_of(row * YROWS, YROWS), YROWS)],
                                     ssem.at[sl])

    def start_block(copy, blk, sl, priority):
        def group(gi, carry):
            for u in range(ROW_GROUP):
                copy(blk, sl, gi * ROW_GROUP + u).start(priority=priority)
            return carry
        lax.fori_loop(0, groups(blk), group, 0)

    def wait_gather(blk, sl):
        n = groups(blk) * (ROW_GROUP * SUB)

        @pl.when(n > 0)
        def _():
            pltpu.make_async_copy(hs_ref.at[pl.ds(0, n)], xg.at[sl, pl.ds(0, n)], gsem.at[sl]).wait()

    def wait_scatter(blk, sl):
        n = groups(blk) * (ROW_GROUP * YROWS)

        @pl.when(n > 0)
        def _():
            pltpu.make_async_copy(yb.at[sl, pl.ds(0, n)], ys_ref.at[pl.ds(0, n)], ssem.at[sl]).wait()

    @pl.when(i == 0)
    def _():
        xg[...] = jnp.zeros_like(xg)
        yb[...] = jnp.zeros_like(yb)
        for sl in range(2):
            spare = pltpu.make_async_copy(yb.at[sl, pl.ds(0, BM * YROWS)],
                                          ys_ref.at[pl.ds((T + sl * BM) * YROWS, BM * YROWS)], ssem.at[sl])
            spare.start()
            spare.wait()
        start_block(gather_copy, 0, 0, 0)

    @pl.when(i + 1 < NB)
    def _():
        start_block(gather_copy, i + 1, 1 - slot, 0)

    wait_gather(i, slot)

    @pl.when(i >= 2)
    def _():
        wait_scatter(i - 2, slot)

    def expert(xb, w1_ref, w3_ref, w2_ref):
        a = jnp.dot(xb, w1_ref[...], preferred_element_type=F32)
        b = jnp.dot(xb, w3_ref[...], preferred_element_type=F32)
        hh = (a * _sigmoid(a) * b).astype(BF16)
        return jnp.dot(hh, w2_ref[...], preferred_element_type=F32)

    def run_experts(rows):
        xb = jnp.concatenate([xg[slot, pl.ds(s, rows, stride=XPITCH), :] for s in range(SUB)],
                             axis=-1).astype(BF16)
        y_lo = expert(xb, w1l_ref, w3l_ref, w2l_ref)
        for s in range(SUB):
            yb[slot, pl.ds(s, rows, stride=YPITCH), :] = y_lo[:, s * LANES:(s + 1) * LANES]
        y_hi = expert(xb, w1h_ref, w3h_ref, w2h_ref)
        for s in range(SUB):
            yb[slot, pl.ds(SUB + s, rows, stride=YPITCH), :] = y_hi[:, s * LANES:(s + 1) * LANES]

    @pl.when(nv_ref[i] > BM // 2)
    def _():
        run_experts(BM)

    @pl.when((nv_ref[i] > 0) & (nv_ref[i] <= BM // 2))
    def _():
        run_experts(BM // 2)

    @pl.when(nv_ref[i] > 0)
    def _():
        start_block(scatter_copy, i, slot, 1)

    @pl.when(i == NB - 1)
    def _():
        wait_scatter(NB - 2, 1 - slot)
        wait_scatter(NB - 1, slot)


def _ffn(blk_lo, blk_hi, n_valid, tok_buf, out_buf, hs, w1, w3, w2, layer):
    def w_spec(shape, which):
        return pl.BlockSpec((None, None) + shape,
                            lambda i, lo, hi, nv, tok, out: (layer, (lo, hi)[which][i], 0, 0))

    return pl.pallas_call(
        _ffn_kernel,
        grid_spec=pltpu.PrefetchScalarGridSpec(
            num_scalar_prefetch=5,
            grid=(NB,),
            in_specs=[pl.BlockSpec(memory_space=pl.ANY),
                      w_spec((D, FF), 0), w_spec((D, FF), 0), w_spec((FF, D), 0),
                      w_spec((D, FF), 1), w_spec((D, FF), 1), w_spec((FF, D), 1)],
            out_specs=pl.BlockSpec(memory_space=pl.ANY),
            scratch_shapes=[pltpu.VMEM((2, BM * XPITCH, LANES), F32), pltpu.VMEM((2, BM * YPITCH, LANES), F32),
                            pltpu.SemaphoreType.DMA((2,)), pltpu.SemaphoreType.DMA((2,))]),
        out_shape=jax.ShapeDtypeStruct(((T + 2 * BM) * YROWS, LANES), F32),
        compiler_params=_params("arbitrary"),
        name="moe_expert_ffn",
    )(blk_lo, blk_hi, n_valid, tok_buf, out_buf, hs, w1, w3, w2, w1, w3, w2)


def _combine_kernel(x_ref, y_ref, slab_ref, mod_ref, fg_ref, o_ref):
    x = _moe_mix(x_ref, y_ref, slab_ref, mod_ref, TR)
    o_ref[...] = x * lax.rsqrt(jnp.mean(x * x, axis=-1, keepdims=True) + EPS) * fg_ref[...]


def _final_combine(x, ys, slab, mods, final_g):
    first = T_CTX // TR

    def rows(height, width):
        return pl.BlockSpec((height, width), lambda i: (i + first, 0))

    return pl.pallas_call(
        _combine_kernel,
        grid=(NR - first,),
        in_specs=[rows(TR, D), rows(TR * YROWS, LANES), rows(TR, LANES),
                  _mod_spec(TR, first), _full_spec((1, D))],
        out_specs=_row_spec(D),
        out_shape=jax.ShapeDtypeStruct((T - first * TR, D), F32),
        compiler_params=_params("arbitrary"),
        name="moe_combine",
    )(x, ys, slab, mods, final_g)


def _class_experts():
    lo, hi = [], []
    for g in range(GROUPS):
        for a in range(PER_GROUP):
            for b in range(a + 1, PER_GROUP):
                lo.append(g * PER_GROUP + a)
                hi.append(g * PER_GROUP + b)
    return jnp.array(lo, jnp.int32), jnp.array(hi, jnp.int32)


def _dispatch_plan(slab, counts):
    cls = slab[:, 0].astype(jnp.int32)
    rank = slab[:, 1].astype(jnp.int32)
    counts = counts[0, :NCLS].astype(jnp.int32)
    padded = ((counts + BM - 1) // BM) * BM
    pend = jnp.cumsum(padded)
    pstart = pend - padded
    ids = jnp.arange(NCLS, dtype=jnp.int32)

    def lookup(table, idx):
        return jnp.sum(jnp.where(idx[:, None] == ids[None, :], table[None, :], 0), axis=1)

    dest = lookup(pstart, cls) + rank
    tok_buf = (jnp.arange(P, dtype=jnp.int32) % T).at[dest].set(
        jnp.arange(T, dtype=jnp.int32), unique_indices=True, mode='promise_in_bounds')
    blk0 = jnp.arange(NB, dtype=jnp.int32) * BM
    blk_c = jnp.minimum(jnp.sum((pend[None, :] <= blk0[:, None]).astype(jnp.int32), axis=1), NCLS - 1)
    n_valid = jnp.clip(lookup(counts, blk_c) - (blk0 - lookup(pstart, blk_c)), 0, BM)
    cls_lo, cls_hi = _class_experts()
    slot = jnp.arange(P, dtype=jnp.int32)
    spare = T + ((slot // BM) % 2) * BM + slot % BM
    out_buf = jnp.where(slot % BM < jnp.repeat(n_valid, BM), tok_buf, spare)
    return tok_buf, out_buf, lookup(cls_lo, blk_c), lookup(cls_hi, blk_c), n_valid


def _moe(hs, slab, counts, w1, w3, w2, layer):
    tok_buf, out_buf, blk_lo, blk_hi, n_valid = _dispatch_plan(slab, counts)
    return _ffn(blk_lo, blk_hi, n_valid, tok_buf, out_buf, hs, w1, w3, w2, layer)


def _cast_kernel(w_ref, o_ref):
    o_ref[...] = w_ref[...].astype(BF16)


def _to_bf16(w, rows):
    n, cols = w.shape
    return pl.pallas_call(
        _cast_kernel,
        grid=(n // rows,),
        in_specs=[pl.BlockSpec((rows, cols), lambda i: (i, 0))],
        out_specs=pl.BlockSpec((rows, cols), lambda i: (i, 0)),
        out_shape=jax.ShapeDtypeStruct((n, cols), BF16),
        compiler_params=_params("arbitrary"),
        name="weights_to_bf16",
    )(w)


def _sincos_2d(rows, width, dim):
    quarter = dim // 4
    omega = 1.0 / (POS_BASE ** (jnp.arange(quarter, dtype=F32) / quarter))
    ar = jnp.arange(rows, dtype=F32)[:, None] * omega
    ac = jnp.arange(width, dtype=F32)[:, None] * omega
    row_part = jnp.concatenate([jnp.sin(ar), jnp.cos(ar)], axis=-1)
    col_part = jnp.concatenate([jnp.sin(ac), jnp.cos(ac)], axis=-1)
    row_part = jnp.broadcast_to(row_part[:, None, :], (rows, width, dim // 2))
    col_part = jnp.broadcast_to(col_part[None, :, :], (rows, width, dim // 2))
    return jnp.concatenate([row_part, col_part], axis=-1).reshape(rows * width, dim)


def kernel(x, c, ctx, c_ctx, ada_w, ada_b, norm_g, final_g, conv_pw1_w, conv_pw1_b, conv_dw_w, conv_dw_b, conv_ln_g, conv_ln_b, conv_pw2_w, conv_pw2_b, gla_w_in, gla_gk_up, gla_gk_b, gla_head_g, gla_w_o, moe_wg, moe_bg, moe_wr, moe_br, moe_w1, moe_w3, moe_w2):
    assert x.shape == (BATCH, SEQ, D) and ctx.shape == (BATCH, CTX_LEN, D)
    pos = _sincos_2d(SEQ // GRID_W, GRID_W, D)
    xt = jnp.concatenate([ctx.reshape(T_CTX, D), (x + pos[None]).reshape(T_LAT, D)], axis=0)

    c8 = jnp.concatenate([c_ctx[None], c, jnp.zeros((8 - 1 - BATCH, D), F32)], axis=0)
    mods = _ada_mods(c8, ada_w, ada_b)
    fg = final_g.reshape(1, D)
    w1b = _to_bf16(moe_w1.reshape(-1, FF), 4 * D).reshape(moe_w1.shape)
    w3b = _to_bf16(moe_w3.reshape(-1, FF), 4 * D).reshape(moe_w3.shape)
    w2b = _to_bf16(moe_w2.reshape(-1, D), 4 * FF).reshape(moe_w2.shape)

    pending = None
    for i in range(DEPTH):
        j = i // 2
        g_mix = norm_g[i, 0].reshape(1, D)
        wr_cat = jnp.concatenate([moe_wr[i], moe_wg[i],
                                  jnp.zeros((D, LANES - EXPERTS - GROUPS), F32)], axis=1)
        br_cat = jnp.concatenate([moe_br[i], moe_bg[i],
                                  jnp.zeros((LANES - EXPERTS - GROUPS,), F32)]).reshape(1, LANES)
        route = (norm_g[i, 1].reshape(1, D),) + _split_bf16(wr_cat) + (br_cat,)
        if i % 2 == 0:
            xt, u = _pw1(xt, pending, g_mix, mods[i], conv_pw1_w[j].astype(BF16),
                         conv_pw1_b[j].reshape(1, 2 * D))
            dww = jnp.concatenate([conv_dw_w[j], jnp.zeros((1, D), F32)], axis=0)
            xt, hs, slab, counts = _conv(u, xt, mods[i], dww, conv_dw_b[j].reshape(1, D),
                                         conv_ln_g[j].reshape(1, D), conv_ln_b[j].reshape(1, D),
                                         conv_pw2_w[j].astype(BF16), conv_pw2_b[j].reshape(1, D), route)
        else:
            w_in = _to_bf16(gla_w_in[j], TM)
            w_codes = jnp.pad(gla_w_in[j][:, GLA_IN - 2 * GATE_RANK:],
                              ((0, 0), (0, LANES - 2 * GATE_RANK))).astype(BF16)
            gk = jnp.zeros((LANES, 2 * KEY_DIM), F32)
            gk = gk.at[0:GATE_RANK, 0:KEY_DIM].set(gla_gk_up[j, 0])
            gk = gk.at[GATE_RANK:2 * GATE_RANK, KEY_DIM:].set(gla_gk_up[j, 1])
            gkb = gla_gk_b[j].reshape(1, 2 * KEY_DIM)
            xt, (q, k, v, gate, lf, lb) = _gla_proj(xt, pending, g_mix, mods[i], w_in, w_codes, gk, gkb)
            o_f, o_b = _gla_scan(q, k, v, lf, lb)
            xt, hs, slab, counts = _gla_out(o_f, o_b, gate, gla_head_g[j].reshape(1, D),
                                            gla_w_o[j].astype(BF16), xt, mods[i], route)
        ys = _moe(hs, slab, counts, w1b, w3b, w2b, i)
        pending = (ys, slab, mods[i])
    return _final_combine(xt, *pending, fg).reshape(BATCH, SEQ, D)
```

```python
import functools

import jax
import jax.numpy as jnp
from jax import lax
from jax.experimental import pallas as pl
from jax.experimental.pallas import tpu as pltpu

F32 = jnp.float32
BF16 = jnp.bfloat16
HIGHEST = lax.Precision.HIGHEST

D = 1024
BATCH = 2
SEQ = 16384
CTX_LEN = 256
DEPTH = 4
GRID_W = 64
POS_BASE = 10000.0
EPS = 1e-6
CONV_WIDTH = 31
CONV_PAD = CONV_WIDTH // 2
HEADS = 4
KEY_DIM = D // 2
HEAD_K = KEY_DIM // HEADS
HEAD_V = D // HEADS
GATE_RANK = 16
GATE_NORM = 16.0
CHUNK = 64
GROUPS = 4
PER_GROUP = 8
EXPERTS = GROUPS * PER_GROUP
FF = D // 2

TM = 256
T_CTX = BATCH * CTX_LEN
T_LAT = BATCH * SEQ
T = T_CTX + T_LAT
NT = T // TM
TR = 512
NR = T // TR
CTX_TILES = T_CTX // TM
LAT_TILES = SEQ // TM
HALO = 16
CONV_ROWS = 128
CONV_WIN = CONV_ROWS + 2 * HALO
SUBLANES = 8
BM = 256
ROW_GROUP = 16
PAIRS = PER_GROUP * (PER_GROUP - 1) // 2
NCLS = GROUPS * PAIRS
NB = T // BM + NCLS
P = NB * BM
LANES = 128
SUB = D // LANES
YROWS = 2 * SUB
XPITCH = 12
YPITCH = 20
GLA_IN = 2 * KEY_DIM + 2 * D + 2 * GATE_RANK
VMEM_LIMIT = 48 * 1024 * 1024

_NT_DIMS = (((1,), (1,)), ((), ()))
_TN_DIMS = (((0,), (0,)), ((), ()))


def _params(*sem):
    return pltpu.CompilerParams(dimension_semantics=sem, vmem_limit_bytes=VMEM_LIMIT)


def _seg(i, rows):
    ctx_tiles = T_CTX // rows
    return jnp.where(i < ctx_tiles, 0, jnp.where(i < ctx_tiles + SEQ // rows, 1, 2))


def _mod_spec(rows=TR, first=0):
    return pl.BlockSpec((None, 6, D), lambda i: (_seg(i + first, rows), 0, 0))


def _row_spec(width, rows=TR):
    return pl.BlockSpec((rows, width), lambda i: (i, 0))


def _full_spec(shape):
    return pl.BlockSpec(shape, lambda i: (0,) * len(shape))


def _normmod(x, g, shift, scale):
    ms = jnp.mean(x * x, axis=-1, keepdims=True)
    return (x * lax.rsqrt(ms + EPS) * g) * (1.0 + scale) + shift


def _sigmoid(x):
    return 1.0 / (1.0 + jnp.exp(-x))


def _split_bf16(a):
    hi = a.astype(BF16)
    return hi, (a - hi.astype(F32)).astype(BF16)


def _dot3(a, w_hi, w_lo):
    a_hi, a_lo = _split_bf16(a)
    return (jnp.dot(a_hi, w_hi, preferred_element_type=F32)
            + jnp.dot(a_lo, w_hi, preferred_element_type=F32)
            + jnp.dot(a_hi, w_lo, preferred_element_type=F32))


def _moe_mix(x_ref, y_ref, slab_ref, mod_ref, rows):
    w_lo = slab_ref[:, 2:3]
    w_hi = slab_ref[:, 3:4]
    y = jnp.concatenate([w_lo * y_ref[pl.ds(s, rows, stride=YROWS), :]
                         + w_hi * y_ref[pl.ds(SUB + s, rows, stride=YROWS), :] for s in range(SUB)], axis=-1)
    return x_ref[...] + mod_ref[5:6, :] * y


def _pending_specs(rows):
    return [pl.BlockSpec((rows * YROWS, LANES), lambda i: (i, 0)), _row_spec(LANES, rows), _mod_spec(rows)]


def _ada_kernel(c_ref, w_ref, b_ref, o_ref):
    c = c_ref[...]
    s = c * _sigmoid(c)
    o_ref[...] = jnp.dot(s, w_ref[...], precision=HIGHEST, preferred_element_type=F32) + b_ref[...]


def _ada_mods(c8, ada_w, ada_b):
    out = pl.pallas_call(
        _ada_kernel,
        grid=(DEPTH, 6),
        in_specs=[pl.BlockSpec((8, D), lambda l, n: (0, 0)),
                  pl.BlockSpec((None, D, D), lambda l, n: (l, 0, n)),
                  pl.BlockSpec((None, 1, D), lambda l, n: (l, 0, n))],
        out_specs=pl.BlockSpec((None, 8, D), lambda l, n: (l, 0, n)),
        out_shape=jax.ShapeDtypeStruct((DEPTH, 8, 6 * D), F32),
        compiler_params=_params("arbitrary", "arbitrary"),
        name="ada_mods",
    )(c8, ada_w, ada_b.reshape(DEPTH, 1, 6 * D))
    return out[:, :3].reshape(DEPTH, 3, 6, D)


def _pw1_kernel(*refs, pending):
    if pending:
        x_ref, y_ref, slab_ref, modp_ref, g_ref, mod_ref, w_ref, b_ref, xo_ref, u_ref = refs
        x = _moe_mix(x_ref, y_ref, slab_ref, modp_ref, TR)
        xo_ref[...] = x
    else:
        x_ref, g_ref, mod_ref, w_ref, b_ref, u_ref = refs
        x = x_ref[...]
    h = _normmod(x, g_ref[...], mod_ref[0:1, :], mod_ref[1:2, :]).astype(BF16)
    a = jnp.dot(h, w_ref[:, :D], preferred_element_type=F32) + b_ref[:, :D]
    g = jnp.dot(h, w_ref[:, D:], preferred_element_type=F32) + b_ref[:, D:]
    u_ref[...] = a * _sigmoid(g)


def _pw1(x, pending, g, mods, w, b):
    row = jax.ShapeDtypeStruct((T, D), F32)
    out = pl.pallas_call(
        functools.partial(_pw1_kernel, pending=pending is not None),
        grid=(NR,),
        in_specs=[_row_spec(D)] + (_pending_specs(TR) if pending else [])
        + [_full_spec((1, D)), _mod_spec(), _full_spec((D, 2 * D)), _full_spec((1, 2 * D))],
        out_specs=[_row_spec(D)] * 2 if pending else _row_spec(D),
        out_shape=[row, row] if pending else row,
        compiler_params=_params("arbitrary"),
        name="conv_pw1_glu",
    )(x, *(pending or ()), g, mods, w, b)
    return out if pending else (x, out)


def _conv_kernel(prev_ref, cur_ref, next_ref, x_ref, mod_ref, dww_ref, dwb_ref, lng_ref, lnb_ref,
                 w_ref, b_ref, g2_ref, wh_ref, wl_ref, br_ref, o_ref, h_ref, slab_ref, counts_ref,
                 ext_ref, conv_ref, cnt_ref):
    i = pl.program_id(0)
    lat0 = CTX_TILES
    lat1 = CTX_TILES + LAT_TILES
    is_start = (i < CTX_TILES) | (i == lat0) | (i == lat1)
    is_end = (i < CTX_TILES) | (i == lat1 - 1) | (i == NT - 1)
    ext_ref[0:HALO, :] = jnp.where(is_start, 0.0, prev_ref[...])
    ext_ref[HALO:HALO + TM, :] = cur_ref[...]
    ext_ref[HALO + TM:, :] = jnp.where(is_end, 0.0, next_ref[...])

    def lane_block(j, carry):
        lo = pl.multiple_of(j * LANES, LANES)
        for rb in range(TM // CONV_ROWS):
            r0 = rb * CONV_ROWS
            win = ext_ref[r0:r0 + CONV_WIN, pl.ds(lo, LANES)]
            acc = jnp.broadcast_to(dwb_ref[:, pl.ds(lo, LANES)], (CONV_ROWS, LANES))
            for rho in range(SUBLANES):
                shifted = win if rho == 0 else pltpu.roll(win, shift=CONV_WIN - rho, axis=0)
                for k in range(CONV_WIDTH):
                    off = k + HALO - CONV_PAD - rho
                    if off % SUBLANES == 0:
                        acc = acc + dww_ref[k:k + 1, pl.ds(lo, LANES)] * shifted[off:off + CONV_ROWS]
            conv_ref[r0:r0 + CONV_ROWS, pl.ds(lo, LANES)] = acc
        return carry

    lax.fori_loop(0, D // LANES, lane_block, 0)
    v = conv_ref[...]
    mu = jnp.mean(v, axis=-1, keepdims=True)
    vc = v - mu
    var = jnp.mean(vc * vc, axis=-1, keepdims=True)
    y = vc * lax.rsqrt(var + EPS) * lng_ref[...] + lnb_ref[...]
    y = (y * _sigmoid(y)).astype(BF16)
    out = jnp.dot(y, w_ref[...], preferred_element_type=F32) + b_ref[...]
    x = x_ref[...] + mod_ref[2:3, :] * out
    o_ref[...] = x
    _route(x, TM, mod_ref, g2_ref, wh_ref, wl_ref, br_ref, h_ref, slab_ref, counts_ref, cnt_ref)


def _conv(u, x, mods, dww, dwb, lng, lnb, w, b, route):
    per = TM // HALO
    return pl.pallas_call(
        _conv_kernel,
        grid=(NT,),
        in_specs=[pl.BlockSpec((HALO, D), lambda i: (jnp.maximum(i * per - 1, 0), 0)),
                  _row_spec(D, TM),
                  pl.BlockSpec((HALO, D), lambda i: (jnp.minimum((i + 1) * per, T // HALO - 1), 0)),
                  _row_spec(D, TM), _mod_spec(TM),
                  _full_spec((CONV_WIDTH + 1, D)), _full_spec((1, D)), _full_spec((1, D)),
                  _full_spec((1, D)), _full_spec((D, D)), _full_spec((1, D))] + _route_in_specs(),
        out_specs=[_row_spec(D, TM)] + _route_out_specs(TM),
        out_shape=[jax.ShapeDtypeStruct((T, D), F32)] + _ROUTE_SHAPES,
        scratch_shapes=[pltpu.VMEM((TM + 2 * HALO, D), F32), pltpu.VMEM((TM, D), F32)] + _ROUTE_SCRATCH,
        compiler_params=_params("arbitrary"),
        name="conv_dw_ln_pw2",
    )(u, u, u, x, mods, dww, dwb, lng, lnb, w, b, *route)


def _log_sigmoid(z):
    return jnp.minimum(z, 0.0) - jnp.log(1.0 + jnp.exp(-jnp.abs(z)))


def _gla_proj_kernel(*refs, pending):
    if pending:
        x_ref, y_ref, slab_ref, modp_ref = refs[:4]
        refs = refs[4:]
        xo_ref = refs[7]
        refs = refs[:7] + refs[8:]
        x = _moe_mix(x_ref, y_ref, slab_ref, modp_ref, TM)
        xo_ref[...] = x
    else:
        x = refs[0][...]
        refs = refs[1:]
    (g_ref, mod_ref, w_ref, wc_ref, gkh_ref, gkl_ref, gkb_ref,
     q_ref, k_ref, v_ref, gate_ref, lf_ref, lb_ref) = refs
    h = _normmod(x, g_ref[...], mod_ref[0:1, :], mod_ref[1:2, :]).astype(BF16)
    kd = KEY_DIM
    q_ref[...] = jnp.dot(h, w_ref[:, 0:kd], preferred_element_type=F32) * (HEAD_K ** -0.5)
    k_ref[...] = jnp.dot(h, w_ref[:, kd:2 * kd], preferred_element_type=F32)
    v_ref[...] = jnp.dot(h, w_ref[:, 2 * kd:2 * kd + D], preferred_element_type=F32).astype(BF16)
    gate_ref[...] = jnp.dot(h, w_ref[:, 2 * kd + D:2 * kd + 2 * D], preferred_element_type=F32)
    codes = jnp.dot(h, wc_ref[...], preferred_element_type=F32)
    z = _dot3(codes, gkh_ref[...], gkl_ref[...]) + gkb_ref[...]
    ls = _log_sigmoid(z) * (1.0 / GATE_NORM)
    lf_ref[...] = ls[:, :kd]
    lb_ref[...] = ls[:, kd:]


def _gla_proj(x, pending, g, mods, w, w_codes, gk, gkb):
    gk_hi, gk_lo = _split_bf16(gk)
    shapes = [jax.ShapeDtypeStruct((T, KEY_DIM), F32), jax.ShapeDtypeStruct((T, KEY_DIM), F32),
              jax.ShapeDtypeStruct((T, D), BF16), jax.ShapeDtypeStruct((T, D), F32),
              jax.ShapeDtypeStruct((T, KEY_DIM), F32), jax.ShapeDtypeStruct((T, KEY_DIM), F32)]
    out_specs = [_row_spec(KEY_DIM, TM), _row_spec(KEY_DIM, TM), _row_spec(D, TM), _row_spec(D, TM),
                 _row_spec(KEY_DIM, TM), _row_spec(KEY_DIM, TM)]
    if pending:
        shapes = [jax.ShapeDtypeStruct((T, D), F32)] + shapes
        out_specs = [_row_spec(D, TM)] + out_specs
    out = pl.pallas_call(
        functools.partial(_gla_proj_kernel, pending=pending is not None),
        grid=(NT,),
        in_specs=[_row_spec(D, TM)] + (_pending_specs(TM) if pending else [])
        + [_full_spec((1, D)), _mod_spec(TM), _full_spec((D, GLA_IN)), _full_spec((D, LANES)),
           _full_spec((LANES, 2 * KEY_DIM)), _full_spec((LANES, 2 * KEY_DIM)),
           _full_spec((1, 2 * KEY_DIM))],
        out_specs=out_specs,
        out_shape=shapes,
        compiler_params=_params("arbitrary"),
        name="gla_proj",
    )(x, *(pending or ()), g, mods, w, w_codes, gk_hi, gk_lo, gkb)
    return (out[0], out[1:]) if pending else (x, out)


def _scan_tile(q_ref, k_ref, v_ref, lg_ref, o_ref, st_ref, cum_ref, reverse):
    r = lax.broadcasted_iota(jnp.int32, (TM, TM), 0)
    c = lax.broadcasted_iota(jnp.int32, (TM, TM), 1)
    shift = CHUNK.bit_length() - 1
    same = (r >> shift) == (c >> shift)
    tri = jnp.where(same & ((c >= r) if reverse else (c <= r)), 1.0, 0.0).astype(BF16)
    lg = lg_ref[...]
    hi = lg.astype(BF16)
    r1 = lg - hi.astype(F32)
    mid = r1.astype(BF16)
    lo = (r1 - mid.astype(F32)).astype(BF16)
    cum_ref[...] = (jnp.dot(tri, hi, preferred_element_type=F32)
                    + jnp.dot(tri, mid, preferred_element_type=F32)
                    + jnp.dot(tri, lo, preferred_element_type=F32))

    rr = lax.broadcasted_iota(jnp.int32, (CHUNK, CHUNK), 0)
    cc = lax.broadcasted_iota(jnp.int32, (CHUNK, CHUNK), 1)
    mask = (cc >= rr) if reverse else (cc <= rr)
    i_last = 0 if reverse else CHUNK - 1
    i_ref = CHUNK // 2 - 1 if reverse else CHUNK // 2
    chunks = range(TM // CHUNK)
    for ci in (reversed(chunks) if reverse else chunks):
        r0 = ci * CHUNK
        for h in range(HEADS):
            k0, v0 = h * HEAD_K, h * HEAD_V
            cm = cum_ref[r0:r0 + CHUNK, k0:k0 + HEAD_K]
            last = cum_ref[r0 + i_last:r0 + i_last + 1, k0:k0 + HEAD_K]
            ref = cum_ref[r0 + i_ref:r0 + i_ref + 1, k0:k0 + HEAD_K]
            qc = q_ref[r0:r0 + CHUNK, k0:k0 + HEAD_K]
            kc = k_ref[r0:r0 + CHUNK, k0:k0 + HEAD_K]
            vc = v_ref[r0:r0 + CHUNK, v0:v0 + HEAD_V]
            qr = (qc * jnp.exp(cm - ref)).astype(BF16)
            kr = (kc * jnp.exp(ref - cm)).astype(BF16)
            att = lax.dot_general(qr, kr, _NT_DIMS, preferred_element_type=F32)
            att = jnp.where(mask, att, 0.0).astype(BF16)
            qd = (qc * jnp.exp(cm)).astype(BF16)
            st = st_ref[h]
            o = (jnp.dot(att, vc, preferred_element_type=F32)
                 + lax.dot_general(qd, st.astype(BF16), _NT_DIMS, preferred_element_type=F32))
            kdec = (kc * jnp.exp(last - cm)).astype(BF16)
            st_ref[h] = st * jnp.exp(last) + lax.dot_general(vc, kdec, _TN_DIMS,
                                                             preferred_element_type=F32)
            o_ref[r0:r0 + CHUNK, v0:v0 + HEAD_V] = o


def _gla_scan_kernel(qf_ref, kf_ref, vf_ref, lf_ref, qb_ref, kb_ref, vb_ref, lb_ref,
                     of_ref, ob_ref, st_ref, cum_ref):
    @pl.when(pl.program_id(1) == 0)
    def _():
        st_ref[...] = jnp.zeros_like(st_ref)

    _scan_tile(qf_ref, kf_ref, vf_ref, lf_ref, of_ref, st_ref.at[0], cum_ref.at[0], False)
    _scan_tile(qb_ref, kb_ref, vb_ref, lb_ref, ob_ref, st_ref.at[1], cum_ref.at[1], True)


def _gla_scan(q, k, v, lf, lb):
    def tile_f(b, j):
        return jnp.where(j == 0, b, CTX_TILES + LAT_TILES * b + j - 1)

    def tile_b(b, j):
        return jnp.where(j == 0, b, CTX_TILES + LAT_TILES * (b + 1) - j)

    def rows(width, tile):
        return pl.BlockSpec((TM, width), lambda b, j: (tile(b, j), 0))

    def scan_specs(tile):
        return [rows(KEY_DIM, tile), rows(KEY_DIM, tile), rows(D, tile), rows(KEY_DIM, tile)]

    return pl.pallas_call(
        _gla_scan_kernel,
        grid=(BATCH, 1 + LAT_TILES),
        in_specs=scan_specs(tile_f) + scan_specs(tile_b),
        out_specs=[rows(D, tile_f), rows(D, tile_b)],
        out_shape=[jax.ShapeDtypeStruct((T, D), F32), jax.ShapeDtypeStruct((T, D), F32)],
        scratch_shapes=[pltpu.VMEM((2, HEADS, HEAD_V, HEAD_K), F32), pltpu.VMEM((2, TM, KEY_DIM), F32)],
        compiler_params=_params("arbitrary", "arbitrary"),
        name="gla_scan",
    )(q, k, v, lf, q, k, v, lb)


def _gla_out_kernel(of_ref, ob_ref, gate_ref, hg_ref, w_ref, x_ref, mod_ref, g2_ref, wh_ref, wl_ref, br_ref,
                    o_ref, h_ref, slab_ref, counts_ref, cnt_ref):
    heads = []
    for h in range(HEADS):
        v0 = h * HEAD_V
        o = ob_ref[:, v0:v0 + HEAD_V] + of_ref[:, v0:v0 + HEAD_V]
        on = o * lax.rsqrt(jnp.mean(o * o, axis=-1, keepdims=True) + EPS)
        on = on * hg_ref[:, v0:v0 + HEAD_V]
        gt = gate_ref[:, v0:v0 + HEAD_V]
        heads.append((on * (gt * _sigmoid(gt))).astype(BF16))
    y = jnp.dot(jnp.concatenate(heads, axis=-1), w_ref[...], preferred_element_type=F32)
    x = x_ref[...] + mod_ref[2:3, :] * y
    o_ref[...] = x
    _route(x, TR, mod_ref, g2_ref, wh_ref, wl_ref, br_ref, h_ref, slab_ref, counts_ref, cnt_ref)


def _gla_out(o_f, o_b, gate, head_g, w, x, mods, route):
    return pl.pallas_call(
        _gla_out_kernel,
        grid=(NR,),
        in_specs=[_row_spec(D), _row_spec(D), _row_spec(D), _full_spec((1, D)), _full_spec((D, D)),
                  _row_spec(D), _mod_spec()] + _route_in_specs(),
        out_specs=[_row_spec(D)] + _route_out_specs(TR),
        out_shape=[jax.ShapeDtypeStruct((T, D), F32)] + _ROUTE_SHAPES,
        scratch_shapes=_ROUTE_SCRATCH,
        compiler_params=_params("arbitrary"),
        name="gla_out_proj",
    )(o_f, o_b, gate, head_g, w, x, mods, *route)


def _route(x, rows, mod_ref, g_ref, wh_ref, wl_ref, b_ref, h_ref, slab_ref, counts_ref, cnt_ref):
    h = _normmod(x, g_ref[...], mod_ref[3:4, :], mod_ref[4:5, :])
    for s in range(SUB):
        h_ref[pl.ds(s, rows, stride=SUB), :] = h[:, s * LANES:(s + 1) * LANES]
    lg = _dot3(h, wh_ref[...], wl_ref[...]) + b_ref[...]
    lane = lax.broadcasted_iota(jnp.int32, (rows, LANES), 1).astype(F32)
    neg = -jnp.inf
    far = float(LANES)
    is_group = (lane >= EXPERTS) & (lane < EXPERTS + GROUPS)
    glog = jnp.where(is_group, lg, neg)
    gmax = jnp.max(glog, axis=-1, keepdims=True)
    gidx = jnp.min(jnp.where(glog == gmax, lane, far), axis=-1, keepdims=True) - EXPERTS
    g_w = 1.0 / jnp.sum(jnp.exp(glog - gmax), axis=-1, keepdims=True)
    lo = gidx * PER_GROUP
    vals = jnp.where((lane >= lo) & (lane < lo + PER_GROUP), lg, neg)
    m1 = jnp.max(vals, axis=-1, keepdims=True)
    i1 = jnp.min(jnp.where(vals == m1, lane, far), axis=-1, keepdims=True)
    vals2 = jnp.where(lane == i1, neg, vals)
    m2 = jnp.max(vals2, axis=-1, keepdims=True)
    i2 = jnp.min(jnp.where(vals2 == m2, lane, far), axis=-1, keepdims=True)
    e = jnp.exp(m2 - m1)
    w1 = g_w / (1.0 + e)
    w2 = g_w * e / (1.0 + e)
    first_lo = i1 < i2
    a = jnp.minimum(i1, i2) - lo
    b = jnp.maximum(i1, i2) - lo
    cls = gidx * PAIRS + a * (PER_GROUP - 1) - a * (a - 1.0) * 0.5 + (b - a - 1.0)
    w_lo = jnp.where(first_lo, w1, w2)
    w_hi = jnp.where(first_lo, w2, w1)
    @pl.when(pl.program_id(0) == 0)
    def _():
        cnt_ref[...] = jnp.zeros_like(cnt_ref)

    onehot = jnp.where(lane == cls, 1.0, 0.0)
    r = lax.broadcasted_iota(jnp.int32, (rows, rows), 0)
    c = lax.broadcasted_iota(jnp.int32, (rows, rows), 1)
    before = jnp.where(c < r, 1.0, 0.0).astype(BF16)
    prefix = jnp.dot(before, onehot.astype(BF16), preferred_element_type=F32)
    base = cnt_ref[...]
    rank = jnp.sum(onehot * (prefix + base), axis=-1, keepdims=True)
    total = base + jnp.sum(onehot, axis=0, keepdims=True)
    cnt_ref[...] = total
    counts_ref[...] = total
    slab_ref[...] = jnp.where(lane == 0, cls, jnp.where(lane == 1, rank, jnp.where(
        lane == 2, w_lo, jnp.where(lane == 3, w_hi, 0.0))))


def _route_in_specs():
    return [_full_spec((1, D)), _full_spec((D, LANES)), _full_spec((D, LANES)), _full_spec((1, LANES))]


def _route_out_specs(rows):
    return [pl.BlockSpec((rows * SUB, LANES), lambda i: (i, 0)), _row_spec(LANES, rows),
            _full_spec((1, LANES))]


_ROUTE_SHAPES = [jax.ShapeDtypeStruct((T * SUB, LANES), F32), jax.ShapeDtypeStruct((T, LANES), F32),
                 jax.ShapeDtypeStruct((1, LANES), F32)]
_ROUTE_SCRATCH = [pltpu.VMEM((1, LANES), F32)]


def _ffn_kernel(lo_ref, hi_ref, nv_ref, tok_ref, out_ref, hs_ref, w1l_ref, w3l_ref, w2l_ref,
                w1h_ref, w3h_ref, w2h_ref, ys_ref, xg, yb, gsem, ssem):
    i = pl.program_id(0)
    slot = i % 2

    def groups(blk):
        return (nv_ref[jnp.clip(blk, 0, NB - 1)] + ROW_GROUP - 1) // ROW_GROUP

    def gather_copy(blk, sl, r):
        tok = tok_ref[blk * BM + r]
        return pltpu.make_async_copy(hs_ref.at[pl.ds(pl.multiple_of(tok * SUB, SUB), SUB)],
                                     xg.at[sl, pl.ds(r * XPITCH, SUB)], gsem.at[sl])

    def scatter_copy(blk, sl, r):
        row = out_ref[blk * BM + r]
        return pltpu.make_async_copy(yb.at[sl, pl.ds(r * YPITCH, YROWS)],
                                     ys_ref.at[pl.ds(pl.multiple_of(row * YROWS, YROWS), YROWS)],
                                     ssem.at[sl])

    def start_block(copy, blk, sl):
        def group(gi, carry):
            for u in range(ROW_GROUP):
                copy(blk, sl, gi * ROW_GROUP + u).start()
            return carry
        lax.fori_loop(0, groups(blk), group, 0)

    def wait_gather(blk, sl):
        n = groups(blk) * (ROW_GROUP * SUB)

        @pl.when(n > 0)
        def _():
            pltpu.make_async_copy(hs_ref.at[pl.ds(0, n)], xg.at[sl, pl.ds(0, n)], gsem.at[sl]).wait()

    def wait_scatter(blk, sl):
        n = groups(blk) * (ROW_GROUP * YROWS)

        @pl.when(n > 0)
        def _():
            pltpu.make_async_copy(yb.at[sl, pl.ds(0, n)], ys_ref.at[pl.ds(0, n)], ssem.at[sl]).wait()

    @pl.when(i == 0)
    def _():
        xg[...] = jnp.zeros_like(xg)
        yb[...] = jnp.zeros_like(yb)
        for sl in range(2):
            spare = pltpu.make_async_copy(yb.at[sl, pl.ds(0, BM * YROWS)],
                                          ys_ref.at[pl.ds((T + sl * BM) * YROWS, BM * YROWS)], ssem.at[sl])
            spare.start()
            spare.wait()
        start_block(gather_copy, 0, 0)

    @pl.when(i + 1 < NB)
    def _():
        start_block(gather_copy, i + 1, 1 - slot)

    wait_gather(i, slot)

    @pl.when(i >= 2)
    def _():
        wait_scatter(i - 2, slot)

    def expert(xb, w1_ref, w3_ref, w2_ref):
        a = jnp.dot(xb, w1_ref[...], preferred_element_type=F32)
        b = jnp.dot(xb, w3_ref[...], preferred_element_type=F32)
        hh = (a * _sigmoid(a) * b).astype(BF16)
        return jnp.dot(hh, w2_ref[...], preferred_element_type=F32)

    def run_experts(rows):
        xb = jnp.concatenate([xg[slot, pl.ds(s, rows, stride=XPITCH), :] for s in range(SUB)],
                             axis=-1).astype(BF16)
        y_lo = expert(xb, w1l_ref, w3l_ref, w2l_ref)
        for s in range(SUB):
            yb[slot, pl.ds(s, rows, stride=YPITCH), :] = y_lo[:, s * LANES:(s + 1) * LANES]
        y_hi = expert(xb, w1h_ref, w3h_ref, w2h_ref)
        for s in range(SUB):
            yb[slot, pl.ds(SUB + s, rows, stride=YPITCH), :] = y_hi[:, s * LANES:(s + 1) * LANES]

    @pl.when(nv_ref[i] > BM // 2)
    def _():
        run_experts(BM)

    @pl.when((nv_ref[i] > 0) & (nv_ref[i] <= BM // 2))
    def _():
        run_experts(BM // 2)

    @pl.when(nv_ref[i] > 0)
    def _():
        start_block(scatter_copy, i, slot)

    @pl.when(i == NB - 1)
    def _():
        wait_scatter(NB - 2, 1 - slot)
        wait_scatter(NB - 1, slot)


def _ffn(blk_lo, blk_hi, n_valid, tok_buf, out_buf, hs, w1, w3, w2, layer):
    def w_spec(shape, which):
        return pl.BlockSpec((None, None) + shape,
                            lambda i, lo, hi, nv, tok, out: (layer, (lo, hi)[which][i], 0, 0))

    return pl.pallas_call(
        _ffn_kernel,
        grid_spec=pltpu.PrefetchScalarGridSpec(
            num_scalar_prefetch=5,
            grid=(NB,),
            in_specs=[pl.BlockSpec(memory_space=pl.ANY),
                      w_spec((D, FF), 0), w_spec((D, FF), 0), w_spec((FF, D), 0),
                      w_spec((D, FF), 1), w_spec((D, FF), 1), w_spec((FF, D), 1)],
            out_specs=pl.BlockSpec(memory_space=pl.ANY),
            scratch_shapes=[pltpu.VMEM((2, BM * XPITCH, LANES), F32), pltpu.VMEM((2, BM * YPITCH, LANES), F32),
                            pltpu.SemaphoreType.DMA((2,)), pltpu.SemaphoreType.DMA((2,))]),
        out_shape=jax.ShapeDtypeStruct(((T + 2 * BM) * YROWS, LANES), F32),
        compiler_params=_params("arbitrary"),
        name="moe_expert_ffn",
    )(blk_lo, blk_hi, n_valid, tok_buf, out_buf, hs, w1, w3, w2, w1, w3, w2)


def _combine_kernel(x_ref, y_ref, slab_ref, mod_ref, fg_ref, o_ref):
    x = _moe_mix(x_ref, y_ref, slab_ref, mod_ref, TR)
    o_ref[...] = x * lax.rsqrt(jnp.mean(x * x, axis=-1, keepdims=True) + EPS) * fg_ref[...]


def _final_combine(x, ys, slab, mods, final_g):
    first = T_CTX // TR

    def rows(height, width):
        return pl.BlockSpec((height, width), lambda i: (i + first, 0))

    return pl.pallas_call(
        _combine_kernel,
        grid=(NR - first,),
        in_specs=[rows(TR, D), rows(TR * YROWS, LANES), rows(TR, LANES),
                  _mod_spec(TR, first), _full_spec((1, D))],
        out_specs=_row_spec(D),
        out_shape=jax.ShapeDtypeStruct((T - first * TR, D), F32),
        compiler_params=_params("arbitrary"),
        name="moe_combine",
    )(x, ys, slab, mods, final_g)


def _class_experts():
    lo, hi = [], []
    for g in range(GROUPS):
        for a in range(PER_GROUP):
            for b in range(a + 1, PER_GROUP):
                lo.append(g * PER_GROUP + a)
                hi.append(g * PER_GROUP + b)
    return jnp.array(lo, jnp.int32), jnp.array(hi, jnp.int32)


def _dispatch_plan(slab, counts):
    cls = slab[:, 0].astype(jnp.int32)
    rank = slab[:, 1].astype(jnp.int32)
    counts = counts[0, :NCLS].astype(jnp.int32)
    padded = ((counts + BM - 1) // BM) * BM
    pend = jnp.cumsum(padded)
    pstart = pend - padded
    ids = jnp.arange(NCLS, dtype=jnp.int32)

    def lookup(table, idx):
        return jnp.sum(jnp.where(idx[:, None] == ids[None, :], table[None, :], 0), axis=1)

    dest = lookup(pstart, cls) + rank
    tok_buf = (jnp.arange(P, dtype=jnp.int32) % T).at[dest].set(jnp.arange(T, dtype=jnp.int32))
    blk0 = jnp.arange(NB, dtype=jnp.int32) * BM
    blk_c = jnp.minimum(jnp.sum((pend[None, :] <= blk0[:, None]).astype(jnp.int32), axis=1), NCLS - 1)
    n_valid = jnp.clip(lookup(counts, blk_c) - (blk0 - lookup(pstart, blk_c)), 0, BM)
    cls_lo, cls_hi = _class_experts()
    slot = jnp.arange(P, dtype=jnp.int32)
    spare = T + ((slot // BM) % 2) * BM + slot % BM
    out_buf = jnp.where(slot % BM < jnp.repeat(n_valid, BM), tok_buf, spare)
    return tok_buf, out_buf, lookup(cls_lo, blk_c), lookup(cls_hi, blk_c), n_valid


def _moe(hs, slab, counts, w1, w3, w2, layer):
    tok_buf, out_buf, blk_lo, blk_hi, n_valid = _dispatch_plan(slab, counts)
    return _ffn(blk_lo, blk_hi, n_valid, tok_buf, out_buf, hs, w1, w3, w2, layer)


def _cast_kernel(w_ref, o_ref):
    o_ref[...] = w_ref[...].astype(BF16)


def _to_bf16(w, rows):
    n, cols = w.shape
    return pl.pallas_call(
        _cast_kernel,
        grid=(n // rows,),
        in_specs=[pl.BlockSpec((rows, cols), lambda i: (i, 0))],
        out_specs=pl.BlockSpec((rows, cols), lambda i: (i, 0)),
        out_shape=jax.ShapeDtypeStruct((n, cols), BF16),
        compiler_params=_params("arbitrary"),
        name="weights_to_bf16",
    )(w)


def _sincos_2d(rows, width, dim):
    quarter = dim // 4
    omega = 1.0 / (POS_BASE ** (jnp.arange(quarter, dtype=F32) / quarter))
    ar = jnp.arange(rows, dtype=F32)[:, None] * omega
    ac = jnp.arange(width, dtype=F32)[:, None] * omega
    row_part = jnp.concatenate([jnp.sin(ar), jnp.cos(ar)], axis=-1)
    col_part = jnp.concatenate([jnp.sin(ac), jnp.cos(ac)], axis=-1)
    row_part = jnp.broadcast_to(row_part[:, None, :], (rows, width, dim // 2))
    col_part = jnp.broadcast_to(col_part[None, :, :], (rows, width, dim // 2))
    return jnp.concatenate([row_part, col_part], axis=-1).reshape(rows * width, dim)


def kernel(x, c, ctx, c_ctx, ada_w, ada_b, norm_g, final_g, conv_pw1_w, conv_pw1_b, conv_dw_w, conv_dw_b, conv_ln_g, conv_ln_b, conv_pw2_w, conv_pw2_b, gla_w_in, gla_gk_up, gla_gk_b, gla_head_g, gla_w_o, moe_wg, moe_bg, moe_wr, moe_br, moe_w1, moe_w3, moe_w2):
    assert x.shape == (BATCH, SEQ, D) and ctx.shape == (BATCH, CTX_LEN, D)
    pos = _sincos_2d(SEQ // GRID_W, GRID_W, D)
    xt = jnp.concatenate([ctx.reshape(T_CTX, D), (x + pos[None]).reshape(T_LAT, D)], axis=0)

    c8 = jnp.concatenate([c_ctx[None], c, jnp.zeros((8 - 1 - BATCH, D), F32)], axis=0)
    mods = _ada_mods(c8, ada_w, ada_b)
    fg = final_g.reshape(1, D)
    w1b = _to_bf16(moe_w1.reshape(-1, FF), 4 * D).reshape(moe_w1.shape)
    w3b = _to_bf16(moe_w3.reshape(-1, FF), 4 * D).reshape(moe_w3.shape)
    w2b = _to_bf16(moe_w2.reshape(-1, D), 4 * FF).reshape(moe_w2.shape)

    pending = None
    for i in range(DEPTH):
        j = i // 2
        g_mix = norm_g[i, 0].reshape(1, D)
        wr_cat = jnp.concatenate([moe_wr[i], moe_wg[i],
                                  jnp.zeros((D, LANES - EXPERTS - GROUPS), F32)], axis=1)
        br_cat = jnp.concatenate([moe_br[i], moe_bg[i],
                                  jnp.zeros((LANES - EXPERTS - GROUPS,), F32)]).reshape(1, LANES)
        route = (norm_g[i, 1].reshape(1, D),) + _split_bf16(wr_cat) + (br_cat,)
        if i % 2 == 0:
            xt, u = _pw1(xt, pending, g_mix, mods[i], conv_pw1_w[j].astype(BF16),
                         conv_pw1_b[j].reshape(1, 2 * D))
            dww = jnp.concatenate([conv_dw_w[j], jnp.zeros((1, D), F32)], axis=0)
            xt, hs, slab, counts = _conv(u, xt, mods[i], dww, conv_dw_b[j].reshape(1, D),
                                         conv_ln_g[j].reshape(1, D), conv_ln_b[j].reshape(1, D),
                                         conv_pw2_w[j].astype(BF16), conv_pw2_b[j].reshape(1, D), route)
        else:
            w_in = _to_bf16(gla_w_in[j], TM)
            w_codes = jnp.pad(gla_w_in[j][:, GLA_IN - 2 * GATE_RANK:],
                              ((0, 0), (0, LANES - 2 * GATE_RANK))).astype(BF16)
            gk = jnp.zeros((LANES, 2 * KEY_DIM), F32)
            gk = gk.at[0:GATE_RANK, 0:KEY_DIM].set(gla_gk_up[j, 0])
            gk = gk.at[GATE_RANK:2 * GATE_RANK, KEY_DIM:].set(gla_gk_up[j, 1])
            gkb = gla_gk_b[j].reshape(1, 2 * KEY_DIM)
            xt, (q, k, v, gate, lf, lb) = _gla_proj(xt, pending, g_mix, mods[i], w_in, w_codes, gk, gkb)
            o_f, o_b = _gla_scan(q, k, v, lf, lb)
            xt, hs, slab, counts = _gla_out(o_f, o_b, gate, gla_head_g[j].reshape(1, D),
                                            gla_w_o[j].astype(BF16), xt, mods[i], route)
        ys = _moe(hs, slab, counts, w1b, w3b, w2b, i)
        pending = (ys, slab, mods[i])
    return _final_combine(xt, *pending, fg).reshape(BATCH, SEQ, D)
```
